```python
import math
import jax, jax.numpy as jnp
from jax import lax
import numpy as np

D_MODEL = 2048
BATCH = 4
SEQ = 2048
DEPTH = 4

N_EVEN = (DEPTH + 1) // 2
N_ODD = DEPTH // 2
BLOCK = 128
ROPE_THETA = 10000.0
NORM_EPS = 1e-6

A_HEAD_DIM = 128
A_HEADS = D_MODEL // 2 // A_HEAD_DIM
A_WIDTH = A_HEADS * A_HEAD_DIM
A_PATTERNS = ((128, 1), (512, 4), (2048, 16))
B_WIDTH = D_MODEL // 2
B_BLOCKS = 8
B_BLOCK_DIM = B_WIDTH // B_BLOCKS
B_CONV = 4
LRU_C = 8.0
EVEN_IN = 3 * A_WIDTH + 2 * B_WIDTH
EVEN_MIX = A_WIDTH + B_WIDTH

C_HEAD_DIM = 64
C_HEADS = D_MODEL // 2 // C_HEAD_DIM
C_KV_HEADS = C_HEADS // 8
C_GROUP = C_HEADS // C_KV_HEADS
C_WIDTH = C_HEADS * C_HEAD_DIM
C_KV_WIDTH = C_KV_HEADS * C_HEAD_DIM
C_WINDOW = 128
D_WIDTH = D_MODEL // 2
D_GROUP_DIM = 16
D_GROUPS = D_WIDTH // D_GROUP_DIM
D_STATE = 64
ODD_IN = C_WIDTH + 2 * C_KV_WIDTH + D_WIDTH
ODD_MIX = C_WIDTH + D_WIDTH

D_FF = ((8 * D_MODEL // 3 + 127) // 128) * 128
FFN_CONV = 3

kernel_name = 'hybrid_dilated_lru_swa_s5_trunk'

F32 = jnp.float32


def rmsnorm(x, g):
    x32 = x.astype(F32)
    y = x32 * lax.rsqrt(jnp.mean(x32 * x32, axis=-1, keepdims=True) + NORM_EPS)
    return (y * g.astype(F32)).astype(x.dtype)


def modulate(h, shift, scale):
    return (h.astype(F32) * (1.0 + scale[:, None]) + shift[:, None]).astype(h.dtype)


def rope(x, positions):
    half = x.shape[-1] // 2
    inv = ROPE_THETA ** (-jnp.arange(half, dtype=F32) / half)
    ang = positions.astype(F32)[..., None] * inv
    cos, sin = jnp.cos(ang)[:, :, None, :], jnp.sin(ang)[:, :, None, :]
    x1, x2 = x[..., :half].astype(F32), x[..., half:].astype(F32)
    return jnp.concatenate([x1 * cos - x2 * sin, x2 * cos + x1 * sin], axis=-1)


def causal_dwconv(x, w, b):
    k, s = w.shape[0], x.shape[1]
    xp = jnp.pad(x, ((0, 0), (k - 1, 0), (0, 0)))
    out = b
    for i in range(k):
        out = out + w[i] * xp[:, i:i + s]
    return out


def linear_scan(a, b):
    def combine(l, r):
        al, bl = l
        ar, br = r
        return ar * al, ar * bl + br
    _, h = lax.associative_scan(combine, (a, b), axis=1)
    return h


def banded_window_attention(q, k, v, max_dist):
    n, r, l, dh = q.shape
    blk = min(BLOCK, l)
    nb = l // blk
    pad = -(-max_dist // blk) * blk
    span = pad + blk
    kp = jnp.pad(k.astype(F32), ((0, 0), (pad, 0), (0, 0)))
    vp = jnp.pad(v.astype(F32), ((0, 0), (pad, 0), (0, 0)))
    idx = jnp.arange(nb)[:, None] * blk + jnp.arange(span)[None, :]
    kb, vb = kp[:, idx], vp[:, idx]
    qb = q.astype(F32).reshape(n, r, nb, blk, dh)
    s = jnp.einsum('nrbqd,nbkd->nrbqk', qb, kb) * (dh ** -0.5)
    qi = jnp.arange(blk)[:, None]
    kj = jnp.arange(span)[None, :]
    dist = qi + pad - kj
    kpos = jnp.arange(nb)[:, None, None] * blk + kj[None] - pad
    valid = (dist >= 0) & (dist <= max_dist) & (kpos >= 0)
    s = jnp.where(valid, s, -jnp.inf)
    m = jnp.max(s, axis=-1, keepdims=True)
    p = jnp.exp(s - m)
    den = jnp.sum(p, axis=-1)
    o = jnp.einsum('nrbqk,nbkd->nrbqd', p, vb) / den[..., None]
    lse = m[..., 0] + jnp.log(den)
    return o.reshape(n, r, l, dh), lse.reshape(n, r, l)


def dilated_window_attention(q, k, v):
    b, s, h, dh = q.shape
    outs, lses = [], []
    for window, dil in A_PATTERNS:
        l = s // dil
        def to_sub(t):
            return t.reshape(b, l, dil, h, dh).transpose(0, 2, 3, 1, 4).reshape(b * dil * h, l, dh)
        o, lse = banded_window_attention(to_sub(q)[:, None], to_sub(k), to_sub(v), window // dil)
        outs.append(o.reshape(b, dil, h, l, dh).transpose(0, 3, 1, 2, 4).reshape(b, s, h, dh))
        lses.append(lse.reshape(b, dil, h, l).transpose(0, 3, 1, 2).reshape(b, s, h))
    w = jax.nn.softmax(jnp.stack(lses, axis=0), axis=0)
    return jnp.einsum('pbsh,pbshd->bshd', w, jnp.stack(outs, axis=0))


def rg_lru(xb, conv_w, conv_b, ga_w, ga_b, gx_w, gx_b, lam):
    b, s, _ = xb.shape
    xc = causal_dwconv(xb.astype(F32), conv_w.astype(F32), conv_b.astype(F32))
    xh = xc.reshape(b, s, B_BLOCKS, B_BLOCK_DIM)
    r = jax.nn.sigmoid(jnp.einsum('bshi,hij->bshj', xh, ga_w.astype(F32)).reshape(b, s, B_WIDTH) + ga_b.astype(F32))
    i = jax.nn.sigmoid(jnp.einsum('bshi,hij->bshj', xh, gx_w.astype(F32)).reshape(b, s, B_WIDTH) + gx_b.astype(F32))
    log_a = -LRU_C * r * jax.nn.softplus(-lam.astype(F32))
    a = jnp.exp(log_a)
    mult = jnp.sqrt(-jnp.expm1(2.0 * log_a))
    return linear_scan(a, mult * (i * xc))


def sink_window_attention(q, k, v, sinks):
    b, s, _, dh = q.shape
    qg = q.reshape(b, s, C_KV_HEADS, C_GROUP, dh).transpose(0, 2, 3, 1, 4).reshape(b * C_KV_HEADS, C_GROUP, s, dh)
    kg = k.transpose(0, 2, 1, 3).reshape(b * C_KV_HEADS, s, dh)
    vg = v.transpose(0, 2, 1, 3).reshape(b * C_KV_HEADS, s, dh)
    o, lse = banded_window_attention(qg, kg, vg, C_WINDOW - 1)
    sink = sinks.astype(F32).reshape(1, C_KV_HEADS, C_GROUP, 1)
    keep = jax.nn.sigmoid(lse.reshape(b, C_KV_HEADS, C_GROUP, s) - sink)
    o = o.reshape(b, C_KV_HEADS, C_GROUP, s, dh) * keep[..., None]
    return o.transpose(0, 3, 1, 2, 4).reshape(b, s, C_WIDTH)


def s5_ssm(u, a_re, a_im, b_re, b_im, c_re, c_im, d_skip, log_dt, glu_w, glu_b):
    b, s, _ = u.shape
    u32 = u.astype(F32).reshape(b, s, D_GROUPS, D_GROUP_DIM)
    lam = lax.complex(a_re.astype(F32), a_im.astype(F32))
    dt = jnp.exp(log_dt.astype(F32))[:, None]
    a_bar = jnp.exp(lam * dt)
    b_mat = lax.complex(b_re.astype(F32), b_im.astype(F32))
    b_bar = ((a_bar - 1.0) / lam)[..., None] * b_mat
    bu = jnp.einsum('bsgc,gpc->bsgp', u32.astype(jnp.complex64), b_bar)
    state = linear_scan(jnp.broadcast_to(a_bar, bu.shape), bu)
    c_mat = lax.complex(c_re.astype(F32), c_im.astype(F32))
    y = jnp.einsum('bsgp,gcp->bsgc', state, c_mat).real + d_skip.astype(F32).reshape(D_GROUPS, D_GROUP_DIM) * u32
    z = jax.nn.gelu(y.reshape(b, s, D_WIDTH))
    return z * jax.nn.sigmoid(z @ glu_w.astype(F32) + glu_b.astype(F32))


def even_mixer(h, positions, w_in, conv_w, conv_b, ga_w, ga_b, gx_w, gx_b, lam, w_out):
    b, s, _ = h.shape
    proj = h @ w_in
    q, k, v, xb, yb = jnp.split(proj, [A_WIDTH, 2 * A_WIDTH, 3 * A_WIDTH, 3 * A_WIDTH + B_WIDTH], axis=-1)
    q = rope(q.reshape(b, s, A_HEADS, A_HEAD_DIM), positions)
    k = rope(k.reshape(b, s, A_HEADS, A_HEAD_DIM), positions)
    attn = dilated_window_attention(q, k, v.reshape(b, s, A_HEADS, A_HEAD_DIM)).reshape(b, s, A_WIDTH)
    lru = rg_lru(xb, conv_w, conv_b, ga_w, ga_b, gx_w, gx_b, lam) * jax.nn.gelu(yb.astype(F32))
    return jnp.concatenate([attn, lru], axis=-1).astype(h.dtype) @ w_out


def odd_mixer(h, positions, w_in, sinks, a_re, a_im, b_re, b_im, c_re, c_im, d_skip, log_dt, glu_w, glu_b, w_out):
    b, s, _ = h.shape
    proj = h @ w_in
    q, k, v, u = jnp.split(proj, [C_WIDTH, C_WIDTH + C_KV_WIDTH, C_WIDTH + 2 * C_KV_WIDTH], axis=-1)
    q = rope(q.reshape(b, s, C_HEADS, C_HEAD_DIM), positions)
    k = rope(k.reshape(b, s, C_KV_HEADS, C_HEAD_DIM), positions)
    attn = sink_window_attention(q, k, v.reshape(b, s, C_KV_HEADS, C_HEAD_DIM), sinks)
    ssm = s5_ssm(u, a_re, a_im, b_re, b_im, c_re, c_im, d_skip, log_dt, glu_w, glu_b)
    return jnp.concatenate([attn, ssm], axis=-1).astype(h.dtype) @ w_out


def conv_ffn(h, w_in, conv_w, conv_b, w_out):
    u = causal_dwconv((h @ w_in).astype(F32), conv_w.astype(F32), conv_b.astype(F32))
    g, v = jnp.split(u, 2, axis=-1)
    return (jax.nn.gelu(g) * v).astype(h.dtype) @ w_out


def setup_inputs(seed: int = 0) -> dict:
    key = jax.random.key(seed)
    ks = iter(jax.random.split(key, 48))

    def nrm(shape, scale):
        return scale * jax.random.normal(next(ks), shape, F32)

    def unif(shape, lo, hi):
        return jax.random.uniform(next(ks), shape, F32, lo, hi)

    x = nrm((BATCH, SEQ, D_MODEL), 1.0)
    c = nrm((BATCH, D_MODEL), 1.0)
    positions = (jax.random.randint(next(ks), (BATCH, 1), 0, 1024, dtype=jnp.int32)
                 + jnp.arange(SEQ, dtype=jnp.int32)[None, :])
    gate_offset = jnp.repeat(jnp.array([0.0, 0.0, 1.0, 0.0, 0.0, 1.0], F32), D_MODEL)
    ada_w = nrm((DEPTH, D_MODEL, 6 * D_MODEL), 0.1 * D_MODEL ** -0.5)
    ada_b = nrm((DEPTH, 6 * D_MODEL), 0.02) + gate_offset
    norm_mix = 1.0 + nrm((DEPTH, D_MODEL), 0.05)
    norm_ffn = 1.0 + nrm((DEPTH, D_MODEL), 0.05)
    norm_final = 1.0 + nrm((D_MODEL,), 0.05)

    ev_w_in = nrm((N_EVEN, D_MODEL, EVEN_IN), D_MODEL ** -0.5)
    ev_conv_w = nrm((N_EVEN, B_CONV, B_WIDTH), B_CONV ** -0.5)
    ev_conv_b = nrm((N_EVEN, B_WIDTH), 0.02)
    ev_gate_a_w = nrm((N_EVEN, B_BLOCKS, B_BLOCK_DIM, B_BLOCK_DIM), B_BLOCK_DIM ** -0.5)
    ev_gate_a_b = nrm((N_EVEN, B_WIDTH), 0.02)
    ev_gate_x_w = nrm((N_EVEN, B_BLOCKS, B_BLOCK_DIM, B_BLOCK_DIM), B_BLOCK_DIM ** -0.5)
    ev_gate_x_b = nrm((N_EVEN, B_WIDTH), 0.02)
    a_pow_c = unif((N_EVEN, B_WIDTH), 0.9, 0.999)
    a_base = a_pow_c ** (1.0 / LRU_C)
    ev_lambda = jnp.log(a_base) - jnp.log1p(-a_base)
    ev_w_out = nrm((N_EVEN, EVEN_MIX, D_MODEL), EVEN_MIX ** -0.5)

    od_w_in = nrm((N_ODD, D_MODEL, ODD_IN), D_MODEL ** -0.5)
    od_sinks = 3.0 + nrm((N_ODD, C_HEADS), 1.0)
    od_a_re = -0.5 + nrm((N_ODD, D_GROUPS, D_STATE), 0.01)
    od_a_im = math.pi * jnp.arange(D_STATE, dtype=F32) + nrm((N_ODD, D_GROUPS, D_STATE), 0.01)
    od_b_re = nrm((N_ODD, D_GROUPS, D_STATE, D_GROUP_DIM), (2.0 * D_GROUP_DIM) ** -0.5)
    od_b_im = nrm((N_ODD, D_GROUPS, D_STATE, D_GROUP_DIM), (2.0 * D_GROUP_DIM) ** -0.5)
    od_c_re = nrm((N_ODD, D_GROUPS, D_GROUP_DIM, D_STATE), (2.0 * D_STATE) ** -0.5)
    od_c_im = nrm((N_ODD, D_GROUPS, D_GROUP_DIM, D_STATE), (2.0 * D_STATE) ** -0.5)
    od_d = nrm((N_ODD, D_WIDTH), 0.5)
    od_log_dt = unif((N_ODD, D_GROUPS), math.log(1e-3), math.log(1e-1))
    od_glu_w = nrm((N_ODD, D_WIDTH, D_WIDTH), D_WIDTH ** -0.5)
    od_glu_b = nrm((N_ODD, D_WIDTH), 0.02)
    od_w_out = nrm((N_ODD, ODD_MIX, D_MODEL), ODD_MIX ** -0.5)

    ffn_w_in = nrm((DEPTH, D_MODEL, 2 * D_FF), D_MODEL ** -0.5)
    ffn_conv_w = nrm((DEPTH, FFN_CONV, 2 * D_FF), FFN_CONV ** -0.5)
    ffn_conv_b = nrm((DEPTH, 2 * D_FF), 0.02)
    ffn_w_out = nrm((DEPTH, D_FF, D_MODEL), D_FF ** -0.5)

    return {'x': x, 'c': c, 'positions': positions,
            'ada_w': ada_w, 'ada_b': ada_b, 'norm_mix': norm_mix, 'norm_ffn': norm_ffn, 'norm_final': norm_final,
            'ev_w_in': ev_w_in, 'ev_conv_w': ev_conv_w, 'ev_conv_b': ev_conv_b,
            'ev_gate_a_w': ev_gate_a_w, 'ev_gate_a_b': ev_gate_a_b, 'ev_gate_x_w': ev_gate_x_w, 'ev_gate_x_b': ev_gate_x_b,
            'ev_lambda': ev_lambda, 'ev_w_out': ev_w_out,
            'od_w_in': od_w_in, 'od_sinks': od_sinks, 'od_a_re': od_a_re, 'od_a_im': od_a_im,
            'od_b_re': od_b_re, 'od_b_im': od_b_im, 'od_c_re': od_c_re, 'od_c_im': od_c_im,
            'od_d': od_d, 'od_log_dt': od_log_dt, 'od_glu_w': od_glu_w, 'od_glu_b': od_glu_b, 'od_w_out': od_w_out,
            'ffn_w_in': ffn_w_in, 'ffn_conv_w': ffn_conv_w, 'ffn_conv_b': ffn_conv_b, 'ffn_w_out': ffn_w_out}


def reference(x, c, positions, ada_w, ada_b, norm_mix, norm_ffn, norm_final,
              ev_w_in, ev_conv_w, ev_conv_b, ev_gate_a_w, ev_gate_a_b, ev_gate_x_w, ev_gate_x_b, ev_lambda, ev_w_out,
              od_w_in, od_sinks, od_a_re, od_a_im, od_b_re, od_b_im, od_c_re, od_c_im, od_d, od_log_dt,
              od_glu_w, od_glu_b, od_w_out,
              ffn_w_in, ffn_conv_w, ffn_conv_b, ffn_w_out):
    cond = jax.nn.silu(c.astype(F32))
    for layer in range(DEPTH):
        mod = cond @ ada_w[layer].astype(F32) + ada_b[layer].astype(F32)
        sh1, sc1, g1, sh2, sc2, g2 = jnp.split(mod, 6, axis=-1)
        h = modulate(rmsnorm(x, norm_mix[layer]), sh1, sc1)
        if layer % 2 == 0:
            e = layer // 2
            y = even_mixer(h, positions, ev_w_in[e], ev_conv_w[e], ev_conv_b[e], ev_gate_a_w[e], ev_gate_a_b[e],
                           ev_gate_x_w[e], ev_gate_x_b[e], ev_lambda[e], ev_w_out[e])
        else:
            o = layer // 2
            y = odd_mixer(h, positions, od_w_in[o], od_sinks[o], od_a_re[o], od_a_im[o], od_b_re[o], od_b_im[o],
                          od_c_re[o], od_c_im[o], od_d[o], od_log_dt[o], od_glu_w[o], od_glu_b[o], od_w_out[o])
        x = x + (g1[:, None] * y.astype(F32)).astype(x.dtype)
        h = modulate(rmsnorm(x, norm_ffn[layer]), sh2, sc2)
        f = conv_ffn(h, ffn_w_in[layer], ffn_conv_w[layer], ffn_conv_b[layer], ffn_w_out[layer])
        x = x + (g2[:, None] * f.astype(F32)).astype(x.dtype)
    return rmsnorm(x, norm_final)
```

```python
import functools
import math

import jax
import jax.numpy as jnp
import numpy as np
from jax import lax
from jax.experimental import pallas as pl
from jax.experimental.pallas import tpu as pltpu

F32 = jnp.float32
BF16 = jnp.bfloat16

D_MODEL = 2048
BATCH = 4
SEQ = 2048
TOKENS = BATCH * SEQ
DEPTH = 4
ROPE_THETA = 10000.0
NORM_EPS = 1e-6
LANES = 128
SUBLANES = 8
BF16_ROWS = 16

A_HEAD_DIM = 128
A_HEADS = 8
A_WIDTH = 1024
A_PATTERNS = ((128, 1), (512, 4), (2048, 16))
B_WIDTH = 1024
B_BLOCKS = 8
B_CONV = 4
LRU_C = 8.0
EVEN_IN = 3 * A_WIDTH + 2 * B_WIDTH

C_HEAD_DIM = 64
C_HEADS = 16
C_KV_HEADS = 2
C_GROUP = 8
C_WIDTH = 1024
C_KV_WIDTH = 128
C_WINDOW = 128
D_WIDTH = 1024
D_GROUP_DIM = 16
D_GROUPS = 64
D_STATE = 64
ODD_IN = C_WIDTH + 2 * C_KV_WIDTH + D_WIDTH

D_FF = 5504
D_FF_PAD = 5632
FFN_CONV = 3

NEG = -1e30

VMEM_LIMIT = 56 * 1024 * 1024


def _cparams(sem, vmem=VMEM_LIMIT):
    return pltpu.CompilerParams(dimension_semantics=sem, vmem_limit_bytes=vmem)


ADA_TN = 1024


def _ada_kernel(c_ref, w_ref, b_ref, o_ref):
    c = c_ref[...]
    cond = (c * jax.nn.sigmoid(c)).astype(BF16)
    o_ref[...] = jnp.dot(cond, w_ref[...].astype(BF16), preferred_element_type=F32) + b_ref[...]


def _ada_mod(c, ada_w, ada_b):
    c8 = jnp.zeros((SUBLANES, D_MODEL), F32).at[:BATCH].set(c.astype(F32))
    n = 6 * D_MODEL
    return pl.pallas_call(
        _ada_kernel,
        out_shape=jax.ShapeDtypeStruct((DEPTH, SUBLANES, n), F32),
        grid=(DEPTH, n // ADA_TN),
        in_specs=[
            pl.BlockSpec((SUBLANES, D_MODEL), lambda l, j: (0, 0)),
            pl.BlockSpec((None, D_MODEL, ADA_TN), lambda l, j: (l, 0, j)),
            pl.BlockSpec((None, 1, ADA_TN), lambda l, j: (l, 0, j)),
        ],
        out_specs=pl.BlockSpec((None, SUBLANES, ADA_TN), lambda l, j: (l, 0, j)),
        compiler_params=_cparams(("parallel", "parallel")),
        name="ada_mod",
    )(c8, ada_w, ada_b.reshape(DEPTH, 1, n))


ROPE_TM = 1024


def _rope_kernel(pos_ref, inva_ref, invc_ref, ca_ref, sa_ref, cc_ref, scp_ref, scm_ref):
    pos = pos_ref[...].astype(F32)
    lane = lax.broadcasted_iota(jnp.int32, (ROPE_TM, LANES), 1)
    ang = pos * inva_ref[...]
    s = jnp.sin(ang)
    ca_ref[...] = jnp.cos(ang)
    sa_ref[...] = jnp.where(lane < A_HEAD_DIM // 2, -s, s)
    ang = pos * invc_ref[...]
    s = jnp.sin(ang)
    cc_ref[...] = jnp.cos(ang)
    second = (lane % C_HEAD_DIM) >= C_HEAD_DIM // 2
    scp_ref[...] = jnp.where(second, s, 0.0)
    scm_ref[...] = jnp.where(second, 0.0, -s)


def _rope_tables(positions):
    half_a, half_c = A_HEAD_DIM // 2, C_HEAD_DIM // 2
    inv_a = ROPE_THETA ** (-jnp.arange(half_a, dtype=F32) / half_a)
    inv_c = ROPE_THETA ** (-jnp.arange(half_c, dtype=F32) / half_c)
    inv_a = jnp.tile(inv_a, LANES // half_a).reshape(1, LANES)
    inv_c = jnp.tile(inv_c, LANES // half_c).reshape(1, LANES)
    tab = jax.ShapeDtypeStruct((TOKENS, LANES), F32)
    row = pl.BlockSpec((ROPE_TM, LANES), lambda i: (i, 0))
    vec = pl.BlockSpec((1, LANES), lambda i: (0, 0))
    return pl.pallas_call(
        _rope_kernel,
        out_shape=(tab,) * 5,
        grid=(TOKENS // ROPE_TM,),
        in_specs=[pl.BlockSpec((ROPE_TM, 1), lambda i: (i, 0)), vec, vec],
        out_specs=(row,) * 5,
        compiler_params=_cparams(("parallel",)),
        name="rope_tables",
    )(positions.reshape(TOKENS, 1), inv_a, inv_c)


def _norm_mod(x, g, sh, sc):
    ms = jnp.mean(x * x, axis=-1, keepdims=True)
    y = x * lax.rsqrt(ms + NORM_EPS) * g
    return y * (1.0 + sc) + sh


def _rmsnorm(x, g):
    ms = jnp.mean(x * x, axis=-1, keepdims=True)
    return x * lax.rsqrt(ms + NORM_EPS) * g


NORM_TM = 512


def _prenorm_kernel(x_ref, g_ref, sh_ref, sc_ref, h_ref):
    h_ref[...] = _norm_mod(x_ref[...], g_ref[...], sh_ref[...], sc_ref[...]).astype(BF16)


def _mod_spec(layer, chunk, tm):
    return pl.BlockSpec((None, None, None, 1, D_MODEL),
                        lambda i, *_: (layer, chunk, (i * tm) // SEQ, 0, 0))


def _prenorm(x, norm_g, mod, layer):
    vec = pl.BlockSpec((None, 1, D_MODEL), lambda i: (layer, 0, 0))
    return pl.pallas_call(
        _prenorm_kernel,
        out_shape=jax.ShapeDtypeStruct((TOKENS, D_MODEL), BF16),
        grid=(TOKENS // NORM_TM,),
        in_specs=[pl.BlockSpec((NORM_TM, D_MODEL), lambda i: (i, 0)), vec,
                  _mod_spec(layer, 0, NORM_TM), _mod_spec(layer, 1, NORM_TM)],
        out_specs=pl.BlockSpec((NORM_TM, D_MODEL), lambda i: (i, 0)),
        compiler_params=_cparams(("parallel",)),
        name="prenorm",
    )(x, norm_g.reshape(DEPTH, 1, D_MODEL), mod, mod)


MM_TM = 1024
MM_TN = 512


def _mm_kernel(a_ref, w_ref, o_ref):
    o_ref[...] = jnp.dot(a_ref[...], w_ref[...], preferred_element_type=F32).astype(o_ref.dtype)


def _matmul(a, w, idx, out_dtype, name):
    m, k = a.shape
    n = w.shape[-1]
    tn = MM_TN if n % MM_TN == 0 else 256
    return pl.pallas_call(
        _mm_kernel,
        out_shape=jax.ShapeDtypeStruct((m, n), out_dtype),
        grid=(m // MM_TM, n // tn),
        in_specs=[pl.BlockSpec((MM_TM, k), lambda i, j: (i, 0)),
                  pl.BlockSpec((None, k, tn), lambda i, j: (idx, 0, j))],
        out_specs=pl.BlockSpec((MM_TM, tn), lambda i, j: (i, j)),
        compiler_params=_cparams(("parallel", "parallel")),
        name=name,
    )(a, w)


RES_TM = 512
RES_TK = 512


def _mm_res_kernel(a_ref, w_ref, x_ref, gate_ref, g_ref, sh_ref, sc_ref, *rest, final):
    if final:
        out_ref, acc_ref = rest
    else:
        xo_ref, ho_ref, acc_ref = rest
    k = pl.program_id(1)

    @pl.when(k == 0)
    def _():
        acc_ref[...] = jnp.zeros_like(acc_ref)

    acc_ref[...] += jnp.dot(a_ref[...], w_ref[...], preferred_element_type=F32)

    @pl.when(k == pl.num_programs(1) - 1)
    def _():
        xn = x_ref[...] + gate_ref[...] * acc_ref[...]
        if final:
            out_ref[...] = _rmsnorm(xn, g_ref[...])
        else:
            xo_ref[...] = xn
            ho_ref[...] = _norm_mod(xn, g_ref[...], sh_ref[...], sc_ref[...]).astype(BF16)


def _mm_res(a, w, widx, x, mod, gate_layer, gate_chunk, norm_g, norm_idx, mod_layer, mod_chunk, final, name):
    m, kdim = a.shape
    row = pl.BlockSpec((RES_TM, D_MODEL), lambda i, k: (i, 0))
    if final:
        gvec = pl.BlockSpec((1, D_MODEL), lambda i, k: (0, 0))
        g_arr = norm_g.reshape(1, D_MODEL)
        out_shape = jax.ShapeDtypeStruct((m, D_MODEL), F32)
        out_specs = row
    else:
        gvec = pl.BlockSpec((None, 1, D_MODEL), lambda i, k: (norm_idx, 0, 0))
        g_arr = norm_g.reshape(DEPTH, 1, D_MODEL)
        out_shape = (jax.ShapeDtypeStruct((m, D_MODEL), F32), jax.ShapeDtypeStruct((m, D_MODEL), BF16))
        out_specs = (row, row)
    return pl.pallas_call(
        functools.partial(_mm_res_kernel, final=final),
        out_shape=out_shape,
        grid=(m // RES_TM, kdim // RES_TK),
        in_specs=[pl.BlockSpec((RES_TM, RES_TK), lambda i, k: (i, k)),
                  pl.BlockSpec((None, RES_TK, D_MODEL), lambda i, k: (widx, k, 0)),
                  row,
                  _mod_spec(gate_layer, gate_chunk, RES_TM),
                  gvec,
                  _mod_spec(mod_layer, mod_chunk, RES_TM),
                  _mod_spec(mod_layer, mod_chunk + 1, RES_TM)],
        out_specs=out_specs,
        scratch_shapes=[pltpu.VMEM((RES_TM, D_MODEL), F32)],
        compiler_params=_cparams(("parallel", "arbitrary")),
        name=name,
    )(a, w, x, mod, g_arr, mod, mod)


FFN_TM = 1024
FFN_TF = 512
FFN_SUB = 256
FFN_HALO = BF16_ROWS


def _ffn_up_kernel(h_ref, halo_ref, wg_ref, wv_ref, cwg_ref, cwv_ref, cbg_ref, cbv_ref, o_ref, hcat_ref):
    @pl.when(pl.program_id(1) == 0)
    def _():
        hcat_ref[0:FFN_HALO, :] = halo_ref[...]
        hcat_ref[FFN_HALO:, :] = h_ref[...]

    lhs = hcat_ref[...]

    def conv(u, w, b):
        base = FFN_HALO - (FFN_CONV - 1)
        out = b
        for i in range(FFN_CONV):
            out = out + w[i:i + 1, :] * u[base + i:base + i + FFN_TM, :]
        return out

    for c in range(FFN_TF // FFN_SUB):
        sl = slice(c * FFN_SUB, (c + 1) * FFN_SUB)
        ug = jnp.dot(lhs, wg_ref[:, sl], preferred_element_type=F32)
        uv = jnp.dot(lhs, wv_ref[:, sl], preferred_element_type=F32)
        g = conv(ug, cwg_ref[:, sl], cbg_ref[:, sl])
        v = conv(uv, cwv_ref[:, sl], cbv_ref[:, sl])
        o_ref[:, sl] = (jax.nn.gelu(g) * v).astype(BF16)


def _ffn_up(h, w_in, conv_w, conv_b, layer):
    nt = TOKENS // FFN_TM
    nf = D_FF_PAD // FFN_TF
    tiles = h.reshape(nt, FFN_TM, D_MODEL)[:, FFN_TM - FFN_HALO:, :]
    prev = jnp.concatenate([jnp.zeros_like(tiles[:1]), tiles[:-1]], axis=0)
    starts_seq = (jnp.arange(nt) * FFN_TM) % SEQ == 0
    halo = jnp.where(starts_seq[:, None, None], jnp.zeros_like(prev), prev)
    return pl.pallas_call(
        _ffn_up_kernel,
        out_shape=jax.ShapeDtypeStruct((TOKENS, D_FF_PAD), BF16),
        grid=(nt, nf),
        in_specs=[pl.BlockSpec((FFN_TM, D_MODEL), lambda i, j: (i, 0)),
                  pl.BlockSpec((None, FFN_HALO, D_MODEL), lambda i, j: (i, 0, 0)),
                  pl.BlockSpec((None, D_MODEL, FFN_TF), lambda i, j: (layer, 0, j)),
                  pl.BlockSpec((None, D_MODEL, FFN_TF), lambda i, j: (layer, 0, j + nf)),
                  pl.BlockSpec((None, FFN_CONV, FFN_TF), lambda i, j: (layer, 0, j)),
                  pl.BlockSpec((None, FFN_CONV, FFN_TF), lambda i, j: (layer, 0, j + nf)),
                  pl.BlockSpec((None, 1, FFN_TF), lambda i, j: (layer, 0, j)),
                  pl.BlockSpec((None, 1, FFN_TF), lambda i, j: (layer, 0, j + nf))],
        out_specs=pl.BlockSpec((FFN_TM, FFN_TF), lambda i, j: (i, j)),
        scratch_shapes=[pltpu.VMEM((FFN_HALO + FFN_TM, D_MODEL), BF16)],
        compiler_params=_cparams(("parallel", "arbitrary")),
        name="ffn_up",
    )(h, halo, w_in, w_in, conv_w, conv_w, conv_b, conv_b)


ATT_T = 256
ATT_NBIAS = 4


def _dilated_bias_tiles():
    tiles = np.zeros((ATT_NBIAS, ATT_T, ATT_T), np.float32)
    qi = np.arange(ATT_T)[:, None]
    kj = np.arange(ATT_T)[None, :]
    for off in range(ATT_NBIAS):
        delta = off * ATT_T + qi - kj
        count = np.zeros_like(delta)
        for window, dil in A_PATTERNS:
            count += ((delta >= 0) & (delta <= window) & (delta % dil == 0)).astype(delta.dtype)
        tiles[off] = np.where(count > 0, np.log(np.maximum(count, 1)), NEG)
    return tiles


def _attn_a_kernel(q_ref, k_ref, v_ref, cos_ref, sin_ref, bias_ref, o_ref, qs_ref, ks_ref, vs_ref):
    cos = cos_ref[...]
    sin = sin_ref[...]
    half = A_HEAD_DIM // 2
    q = q_ref[...]
    k = k_ref[...]
    scale = A_HEAD_DIM ** -0.5
    qs_ref[...] = ((q * cos + pltpu.roll(q, half, 1) * sin) * scale).astype(BF16)
    ks_ref[...] = (k * cos + pltpu.roll(k, half, 1) * sin).astype(BF16)
    vs_ref[...] = v_ref[...].astype(BF16)

    def q_body(i, carry):
        q_blk = qs_ref[pl.ds(pl.multiple_of(i * ATT_T, ATT_T), ATT_T), :]

        def kv_body(j, st):
            m, l, acc = st
            rows = pl.ds(pl.multiple_of(j * ATT_T, ATT_T), ATT_T)
            s = lax.dot_general(q_blk, ks_ref[rows, :], (((1,), (1,)), ((), ())),
                                preferred_element_type=F32)
            s = s + bias_ref[jnp.minimum(i - j, ATT_NBIAS - 1)]
            m_new = jnp.maximum(m, jnp.max(s, axis=-1, keepdims=True))
            alpha = jnp.exp(m - m_new)
            p = jnp.exp(s - m_new)
            l = alpha * l + jnp.sum(p, axis=-1, keepdims=True)
            acc = alpha * acc + jnp.dot(p.astype(BF16), vs_ref[rows, :], preferred_element_type=F32)
            return m_new, l, acc

        init = (jnp.full((ATT_T, 1), NEG, F32), jnp.zeros((ATT_T, 1), F32),
                jnp.zeros((ATT_T, A_HEAD_DIM), F32))
        _, l, acc = lax.fori_loop(0, i + 1, kv_body, init)
        o_ref[pl.ds(pl.multiple_of(i * ATT_T, ATT_T), ATT_T), :] = (acc / l).astype(o_ref.dtype)
        return carry

    lax.fori_loop(0, SEQ // ATT_T, q_body, 0)


def _attn_a(proj, cos_a, sin_a):
    bias = jnp.asarray(_dilated_bias_tiles())
    tab = pl.BlockSpec((SEQ, LANES), lambda b, h: (b, 0))
    return pl.pallas_call(
        _attn_a_kernel,
        out_shape=jax.ShapeDtypeStruct((TOKENS, A_WIDTH), BF16),
        grid=(BATCH, A_HEADS),
        in_specs=[pl.BlockSpec((SEQ, A_HEAD_DIM), lambda b, h: (b, h)),
                  pl.BlockSpec((SEQ, A_HEAD_DIM), lambda b, h: (b, A_HEADS + h)),
                  pl.BlockSpec((SEQ, A_HEAD_DIM), lambda b, h: (b, 2 * A_HEADS + h)),
                  tab, tab,
                  pl.BlockSpec((ATT_NBIAS, ATT_T, ATT_T), lambda b, h: (0, 0, 0))],
        out_specs=pl.BlockSpec((SEQ, A_HEAD_DIM), lambda b, h: (b, h)),
        scratch_shapes=[pltpu.VMEM((SEQ, A_HEAD_DIM), BF16)] * 3,
        compiler_params=_cparams(("parallel", "parallel")),
        name="attn_dilated",
    )(proj, proj, proj, cos_a, sin_a, bias)


LRU_TS = 512
LRU_HALO = SUBLANES


def _lru_kernel(xb_ref, yb_ref, cw_ref, cb_ref, ga_ref, gab_ref, gx_ref, gxb_ref, lam_ref, o_ref,
                ext_ref, a_ref, b_ref, carry_ref):
    t = pl.program_id(1)

    @pl.when(t == 0)
    def _():
        ext_ref[0:LRU_HALO, :] = jnp.zeros((LRU_HALO, B_WIDTH), F32)
        carry_ref[...] = jnp.zeros_like(carry_ref)

    ext_ref[LRU_HALO:, :] = xb_ref[...]
    ext = ext_ref[...]
    base = LRU_HALO - (B_CONV - 1)
    xc = cb_ref[...]
    for i in range(B_CONV):
        xc = xc + cw_ref[i:i + 1, :] * ext[base + i:base + i + LRU_TS, :]
    ext_ref[0:LRU_HALO, :] = ext[LRU_TS:LRU_TS + LRU_HALO, :]

    lam = lam_ref[...]
    neg_sp = -LRU_C * (jnp.maximum(-lam, 0.0) + jnp.log1p(jnp.exp(-jnp.abs(lam))))
    width = B_WIDTH // B_BLOCKS
    for blk in range(B_BLOCKS):
        sl = slice(blk * width, (blk + 1) * width)
        xh = xc[:, sl]
        xh16 = xh.astype(BF16)
        r = jax.nn.sigmoid(jnp.dot(xh16, ga_ref[blk], preferred_element_type=F32) + gab_ref[:, sl])
        gi = jax.nn.sigmoid(jnp.dot(xh16, gx_ref[blk], preferred_element_type=F32) + gxb_ref[:, sl])
        log_a = r * neg_sp[:, sl]
        a_ref[:, sl] = jnp.exp(log_a)
        th = jnp.tanh(log_a)
        b_ref[:, sl] = jnp.sqrt(-2.0 * th / (1.0 - th)) * (gi * xh)

    row = lax.broadcasted_iota(jnp.int32, (SUBLANES, B_WIDTH), 0)

    def scan_body(g, h_prev):
        rows = pl.ds(pl.multiple_of(g * SUBLANES, SUBLANES), SUBLANES)
        a = a_ref[rows, :]
        b = b_ref[rows, :]
        for s in (1, 2, 4):
            keep = row >= s
            a_sh = jnp.where(keep, pltpu.roll(a, s, 0), 1.0)
            b_sh = jnp.where(keep, pltpu.roll(b, s, 0), 0.0)
            b = a * b_sh + b
            a = a * a_sh
        h = a * h_prev + b
        b_ref[rows, :] = h
        return jnp.broadcast_to(h[SUBLANES - 1:SUBLANES, :], (SUBLANES, B_WIDTH))

    carry_ref[...] = lax.fori_loop(0, LRU_TS // SUBLANES, scan_body, carry_ref[...])
    o_ref[...] = (b_ref[...] * jax.nn.gelu(yb_ref[...])).astype(o_ref.dtype)


def _lru(proj, conv_w, conv_b, ga_w, ga_b, gx_w, gx_b, lam, e):
    nts = SEQ // LRU_TS
    vec = pl.BlockSpec((None, 1, B_WIDTH), lambda b, t: (e, 0, 0))
    gate = pl.BlockSpec((None, B_BLOCKS, B_WIDTH // B_BLOCKS, B_WIDTH // B_BLOCKS), lambda b, t: (e, 0, 0, 0))
    r3 = lambda a: a.reshape(a.shape[0], 1, B_WIDTH)
    return pl.pallas_call(
        _lru_kernel,
        out_shape=jax.ShapeDtypeStruct((TOKENS, B_WIDTH), BF16),
        grid=(BATCH, nts),
        in_specs=[pl.BlockSpec((LRU_TS, B_WIDTH), lambda b, t: (b * nts + t, 3)),
                  pl.BlockSpec((LRU_TS, B_WIDTH), lambda b, t: (b * nts + t, 4)),
                  pl.BlockSpec((None, B_CONV, B_WIDTH), lambda b, t: (e, 0, 0)),
                  vec, gate, vec, gate, vec, vec],
        out_specs=pl.BlockSpec((LRU_TS, B_WIDTH), lambda b, t: (b * nts + t, 0)),
        scratch_shapes=[pltpu.VMEM((LRU_HALO + LRU_TS, B_WIDTH), F32),
                        pltpu.VMEM((LRU_TS, B_WIDTH), F32),
                        pltpu.VMEM((LRU_TS, B_WIDTH), F32),
                        pltpu.VMEM((SUBLANES, B_WIDTH), F32)],
        compiler_params=_cparams(("parallel", "arbitrary")),
        name="rg_lru",
    )(proj, proj, conv_w, r3(conv_b), ga_w, r3(ga_b), gx_w, r3(gx_b), r3(lam))


SWA_T = 128
SWA_PAIRS = C_GROUP // 2


def _swa_bias_tiles():
    qi = np.tile(np.arange(SWA_T), SWA_PAIRS)[:, None]
    kj = np.arange(2 * SWA_T)[None, :]
    delta = qi + SWA_T - kj
    band = (delta >= 0) & (delta <= C_WINDOW - 1)
    tiles = np.zeros((2, SWA_PAIRS * SWA_T, 2 * SWA_T), np.float32)
    tiles[0] = np.where(band & (kj >= SWA_T), 0.0, NEG)
    tiles[1] = np.where(band, 0.0, NEG)
    return tiles


def _swa_kernel(sink_ref, q_ref, kv_ref, cos_ref, sp_ref, sm_ref, bias_ref, o_ref,
                qs_ref, ka_ref, kb_ref, va_ref, vb_ref):
    kvh = pl.program_id(1)
    cos = cos_ref[...]
    s_plus = sp_ref[...]
    s_minus = sm_ref[...]
    quarter = C_HEAD_DIM // 2

    def rope(x):
        return x * cos + pltpu.roll(x, quarter, 1) * s_plus + pltpu.roll(x, LANES - quarter, 1) * s_minus

    scale = C_HEAD_DIM ** -0.5
    for j in range(SWA_PAIRS):
        sl = slice(j * LANES, (j + 1) * LANES)
        qs_ref[:, sl] = (rope(q_ref[:, sl]) * scale).astype(BF16)

    lane = lax.broadcasted_iota(jnp.int32, (SEQ, LANES), 1)
    low = lane < C_HEAD_DIM
    kk = rope(kv_ref[:, 0:LANES])
    vv = kv_ref[:, LANES:2 * LANES]
    kk = jnp.where(kvh == 0, kk, pltpu.roll(kk, C_HEAD_DIM, 1))
    vv = jnp.where(kvh == 0, vv, pltpu.roll(vv, C_HEAD_DIM, 1))
    k_lo = jnp.where(low, kk, 0.0)
    v_lo = jnp.where(low, vv, 0.0)
    zeros = jnp.zeros((SWA_T, LANES), BF16)
    for ref, val in ((ka_ref, k_lo), (kb_ref, pltpu.roll(k_lo, C_HEAD_DIM, 1)),
                     (va_ref, v_lo), (vb_ref, pltpu.roll(v_lo, C_HEAD_DIM, 1))):
        ref[0:SWA_T, :] = zeros
        ref[SWA_T:, :] = val.astype(BF16)

    rows_st = SWA_PAIRS * SWA_T
    pair = lax.broadcasted_iota(jnp.int32, (rows_st, 1), 0) // SWA_T
    sink_a = jnp.zeros((rows_st, 1), F32)
    sink_b = jnp.zeros((rows_st, 1), F32)
    for j in range(SWA_PAIRS):
        sink_a = jnp.where(pair == j, sink_ref[kvh * C_GROUP + 2 * j], sink_a)
        sink_b = jnp.where(pair == j, sink_ref[kvh * C_GROUP + 2 * j + 1], sink_b)

    def q_body(i, carry):
        r0 = pl.multiple_of(i * SWA_T, SWA_T)
        q_st = jnp.concatenate([qs_ref[pl.ds(r0, SWA_T), j * LANES:(j + 1) * LANES]
                                for j in range(SWA_PAIRS)], axis=0)
        bias = bias_ref[jnp.minimum(i, 1)]
        win = pl.ds(r0, 2 * SWA_T)
        out = jnp.zeros((rows_st, LANES), F32)
        for k_ref, v_ref, sink in ((ka_ref, va_ref, sink_a), (kb_ref, vb_ref, sink_b)):
            s = lax.dot_general(q_st, k_ref[win, :], (((1,), (1,)), ((), ())),
                                preferred_element_type=F32) + bias
            m = jnp.maximum(jnp.max(s, axis=-1, keepdims=True), sink)
            p = jnp.exp(s - m)
            den = jnp.sum(p, axis=-1, keepdims=True) + jnp.exp(sink - m)
            out = out + jnp.dot(p.astype(BF16), v_ref[win, :], preferred_element_type=F32) / den
        for j in range(SWA_PAIRS):
            o_ref[pl.ds(r0, SWA_T), j * LANES:(j + 1) * LANES] = out[j * SWA_T:(j + 1) * SWA_T, :].astype(o_ref.dtype)
        return carry

    lax.fori_loop(0, SEQ // SWA_T, q_body, 0)


def _swa(proj, sinks, cos_c, sin_cp, sin_cm, o):
    bias = jnp.asarray(_swa_bias_tiles())
    qw = C_WIDTH // C_KV_HEADS
    tab = pl.BlockSpec((SEQ, LANES), lambda b, g, *_: (b, 0))
    kv_block = C_WIDTH // (2 * C_KV_WIDTH)
    grid_spec = pltpu.PrefetchScalarGridSpec(
        num_scalar_prefetch=1,
        grid=(BATCH, C_KV_HEADS),
        in_specs=[pl.BlockSpec((SEQ, qw), lambda b, g, *_: (b, g)),
                  pl.BlockSpec((SEQ, 2 * C_KV_WIDTH), lambda b, g, *_: (b, kv_block)),
                  tab, tab, tab,
                  pl.BlockSpec((2, SWA_PAIRS * SWA_T, 2 * SWA_T), lambda b, g, *_: (0, 0, 0))],
        out_specs=pl.BlockSpec((SEQ, qw), lambda b, g, *_: (b, g)),
        scratch_shapes=[pltpu.VMEM((SEQ, qw), BF16)] + [pltpu.VMEM((SWA_T + SEQ, LANES), BF16)] * 4,
    )
    return pl.pallas_call(
        _swa_kernel,
        out_shape=jax.ShapeDtypeStruct((TOKENS, C_WIDTH), BF16),
        grid_spec=grid_spec,
        compiler_params=_cparams(("parallel", "parallel")),
        name="attn_swa",
    )(sinks[o].astype(F32), proj, proj, cos_c, sin_cp, sin_cm, bias)


S5_TS = 512
S5_GPB = LANES // D_GROUP_DIM
S5_NB = D_WIDTH // LANES
S5_SW = S5_GPB * D_STATE


def _s5_kernel(u_ref, wre_ref, wim_ref, cre_ref, cim_ref, dec_ref, d_ref, o_ref, sre_ref, sim_ref, carry_ref):
    t = pl.program_id(2)

    @pl.when(t == 0)
    def _():
        carry_ref[...] = jnp.zeros_like(carry_ref)

    u = u_ref[...]
    u16 = u.astype(BF16)
    ng = S5_TS // SUBLANES
    x_re = jnp.dot(u16, wre_ref[...], preferred_element_type=F32).reshape(ng, SUBLANES, S5_SW)
    x_im = jnp.dot(u16, wim_ref[...], preferred_element_type=F32).reshape(ng, SUBLANES, S5_SW)
    for idx, s in enumerate((1, 2, 4)):
        m_re = dec_ref[2 * idx]
        m_im = dec_ref[2 * idx + 1]
        r_re = pltpu.roll(x_re, s, 1)
        r_im = pltpu.roll(x_im, s, 1)
        x_re, x_im = x_re + (m_re * r_re - m_im * r_im), x_im + (m_re * r_im + m_im * r_re)
    sre_ref[...] = x_re.reshape(S5_TS, S5_SW)
    sim_ref[...] = x_im.reshape(S5_TS, S5_SW)
    p_re = dec_ref[6]
    p_im = dec_ref[7]

    def carry_body(g, c):
        c_re, c_im = c
        rows = pl.ds(pl.multiple_of(g * SUBLANES, SUBLANES), SUBLANES)
        s_re = sre_ref[rows, :] + (p_re * c_re - p_im * c_im)
        s_im = sim_ref[rows, :] + (p_re * c_im + p_im * c_re)
        sre_ref[rows, :] = s_re
        sim_ref[rows, :] = s_im
        last = slice(SUBLANES - 1, SUBLANES)
        return (jnp.broadcast_to(s_re[last, :], (SUBLANES, S5_SW)),
                jnp.broadcast_to(s_im[last, :], (SUBLANES, S5_SW)))

    c_re, c_im = lax.fori_loop(0, ng, carry_body, (carry_ref[0], carry_ref[1]))
    carry_ref[0] = c_re
    carry_ref[1] = c_im
    y = (jnp.dot(sre_ref[...].astype(BF16), cre_ref[...], preferred_element_type=F32)
         - jnp.dot(sim_ref[...].astype(BF16), cim_ref[...], preferred_element_type=F32)
         + d_ref[...] * u)
    o_ref[...] = jax.nn.gelu(y).astype(o_ref.dtype)


def _s5_tables(a_re, a_im, b_re, b_im, c_re, c_im, log_dt):
    dt = jnp.exp(log_dt.astype(F32))[:, None]
    lr, li = a_re.astype(F32), a_im.astype(F32)
    zr, zi = lr * dt, li * dt

    def a_pow(k):
        mag = jnp.exp(k * zr)
        return mag * jnp.cos(k * zi), mag * jnp.sin(k * zi)

    ar, ai = a_pow(1.0)
    den = lr * lr + li * li
    cr = ((ar - 1.0) * lr + ai * li) / den
    ci = (ai * lr - (ar - 1.0) * li) / den
    bb_re = cr[..., None] * b_re - ci[..., None] * b_im
    bb_im = cr[..., None] * b_im + ci[..., None] * b_re
    eye = jnp.eye(S5_GPB, dtype=F32)

    def in_blocks(bb):
        x = bb.reshape(S5_NB, S5_GPB, D_STATE, D_GROUP_DIM).transpose(0, 1, 3, 2)
        return jnp.einsum('ngcp,gh->ngchp', x, eye).reshape(S5_NB, LANES, S5_SW).astype(BF16)

    def out_blocks(cc):
        x = cc.astype(F32).reshape(S5_NB, S5_GPB, D_GROUP_DIM, D_STATE).transpose(0, 1, 3, 2)
        return jnp.einsum('ngpc,gh->ngphc', x, eye).reshape(S5_NB, S5_SW, LANES).astype(BF16)

    rows = jnp.arange(SUBLANES)[:, None]
    tabs = []
    for s in (1, 2, 4):
        pr, pi = a_pow(float(s))
        for p in (pr, pi):
            tabs.append(jnp.where(rows >= s, p.reshape(S5_NB, 1, S5_SW), 0.0))
    zr_b, zi_b = zr.reshape(S5_NB, 1, S5_SW), zi.reshape(S5_NB, 1, S5_SW)
    kk = (rows + 1).astype(F32)
    mag = jnp.exp(kk * zr_b)
    tabs.append(mag * jnp.cos(kk * zi_b))
    tabs.append(mag * jnp.sin(kk * zi_b))
    dec = jnp.stack(tabs, axis=1)
    return in_blocks(bb_re), in_blocks(bb_im), out_blocks(c_re), out_blocks(c_im), dec


def _s5(proj, tables, d_skip, o):
    wre, wim, cre, cim, dec = tables
    nts = SEQ // S5_TS
    ublock = (C_WIDTH + 2 * C_KV_WIDTH) // LANES
    win = pl.BlockSpec((None, LANES, S5_SW), lambda b, n, t: (n, 0, 0))
    wout = pl.BlockSpec((None, S5_SW, LANES), lambda b, n, t: (n, 0, 0))
    return pl.pallas_call(
        _s5_kernel,
        out_shape=jax.ShapeDtypeStruct((TOKENS, D_WIDTH), BF16),
        grid=(BATCH, S5_NB, nts),
        in_specs=[pl.BlockSpec((S5_TS, LANES), lambda b, n, t: (b * nts + t, ublock + n)),
                  win, win, wout, wout,
                  pl.BlockSpec((None, 8, SUBLANES, S5_SW), lambda b, n, t: (n, 0, 0, 0)),
                  pl.BlockSpec((None, 1, LANES), lambda b, n, t: (o, 0, n))],
        out_specs=pl.BlockSpec((S5_TS, LANES), lambda b, n, t: (b * nts + t, n)),
        scratch_shapes=[pltpu.VMEM((S5_TS, S5_SW), F32), pltpu.VMEM((S5_TS, S5_SW), F32),
                        pltpu.VMEM((2, SUBLANES, S5_SW), F32)],
        compiler_params=_cparams(("parallel", "parallel", "arbitrary")),
        name="s5_ssm",
    )(proj, wre, wim, cre, cim, dec, d_skip.reshape(-1, 1, D_WIDTH))


GLU_TN = 512


def _glu_kernel(z_ref, w_ref, b_ref, zc_ref, o_ref):
    gate = jax.nn.sigmoid(jnp.dot(z_ref[...], w_ref[...], preferred_element_type=F32) + b_ref[...])
    o_ref[...] = (zc_ref[...].astype(F32) * gate).astype(o_ref.dtype)


def _glu(z, w, b, o):
    return pl.pallas_call(
        _glu_kernel,
        out_shape=jax.ShapeDtypeStruct((TOKENS, D_WIDTH), BF16),
        grid=(TOKENS // MM_TM, D_WIDTH // GLU_TN),
        in_specs=[pl.BlockSpec((MM_TM, D_WIDTH), lambda i, j: (i, 0)),
                  pl.BlockSpec((None, D_WIDTH, GLU_TN), lambda i, j: (o, 0, j)),
                  pl.BlockSpec((None, 1, GLU_TN), lambda i, j: (o, 0, j)),
                  pl.BlockSpec((MM_TM, GLU_TN), lambda i, j: (i, j))],
        out_specs=pl.BlockSpec((MM_TM, GLU_TN), lambda i, j: (i, j)),
        compiler_params=_cparams(("parallel", "parallel")),
        name="s5_glu",
    )(z, w, b.reshape(-1, 1, D_WIDTH), z)


def _pad_ff(a, axis):
    pad = [(0, 0)] * a.ndim
    pad[axis] = (0, D_FF_PAD - D_FF)
    return jnp.pad(a, pad)


def kernel(x, c, positions, ada_w, ada_b, norm_mix, norm_ffn, norm_final, ev_w_in, ev_conv_w, ev_conv_b, ev_gate_a_w, ev_gate_a_b, ev_gate_x_w, ev_gate_x_b, ev_lambda, ev_w_out, od_w_in, od_sinks, od_a_re, od_a_im, od_b_re, od_b_im, od_c_re, od_c_im, od_d, od_log_dt, od_glu_w, od_glu_b, od_w_out, ffn_w_in, ffn_conv_w, ffn_conv_b, ffn_w_out):
    ev_w_in16 = ev_w_in.astype(BF16)
    ev_w_out16 = ev_w_out.astype(BF16)
    od_w_in16 = od_w_in.astype(BF16)
    od_w_out16 = od_w_out.astype(BF16)
    od_glu16 = od_glu_w.astype(BF16)
    ga16 = ev_gate_a_w.astype(BF16)
    gx16 = ev_gate_x_w.astype(BF16)
    ffn_in16 = jnp.concatenate([_pad_ff(ffn_w_in[..., :D_FF].astype(BF16), 2),
                                _pad_ff(ffn_w_in[..., D_FF:].astype(BF16), 2)], axis=2)
    ffn_out16 = _pad_ff(ffn_w_out.astype(BF16), 1)
    ffn_cw = jnp.concatenate([_pad_ff(ffn_conv_w[..., :D_FF], 2), _pad_ff(ffn_conv_w[..., D_FF:], 2)], axis=2)
    ffn_cb = jnp.concatenate([_pad_ff(ffn_conv_b[..., :D_FF], 1), _pad_ff(ffn_conv_b[..., D_FF:], 1)],
                             axis=1).reshape(DEPTH, 1, 2 * D_FF_PAD)

    mod = _ada_mod(c, ada_w, ada_b)
    mod = mod.reshape(DEPTH, SUBLANES, 6, 1, D_MODEL).transpose(0, 2, 1, 3, 4)
    cos_a, sin_a, cos_c, sin_cp, sin_cm = _rope_tables(positions)

    xt = x.reshape(TOKENS, D_MODEL).astype(F32)
    h = _prenorm(xt, norm_mix, mod, 0)
    for layer in range(DEPTH):
        idx = layer // 2
        if layer % 2 == 0:
            proj = _matmul(h, ev_w_in16, idx, F32, "even_in_proj")
            attn = _attn_a(proj, cos_a, sin_a)
            lru = _lru(proj, ev_conv_w, ev_conv_b, ga16, ev_gate_a_b, gx16, ev_gate_x_b, ev_lambda, idx)
            mix = jnp.concatenate([attn, lru], axis=1)
            w_out = ev_w_out16
        else:
            proj = _matmul(h, od_w_in16, idx, F32, "odd_in_proj")
            attn = _swa(proj, od_sinks, cos_c, sin_cp, sin_cm, idx)
            tables = _s5_tables(od_a_re[idx], od_a_im[idx], od_b_re[idx], od_b_im[idx],
                                od_c_re[idx], od_c_im[idx], od_log_dt[idx])
            z = _s5(proj, tables, od_d, idx)
            ssm = _glu(z, od_glu16, od_glu_b, idx)
            mix = jnp.concatenate([attn, ssm], axis=1)
            w_out = od_w_out16
        xt, h2 = _mm_res(mix, w_out, idx, xt, mod, layer, 2, norm_ffn, layer, layer, 3, False, "mix_out_proj")
        act = _ffn_up(h2, ffn_in16, ffn_cw, ffn_cb, layer)
        if layer + 1 < DEPTH:
            xt, h = _mm_res(act, ffn_out16, layer, xt, mod, layer, 5, norm_mix, layer + 1, layer + 1, 0,
                            False, "ffn_down_proj")
        else:
            out = _mm_res(act, ffn_out16, layer, xt, mod, layer, 5, norm_final, 0, layer, 0, True,
                          "ffn_down_final")
    return out.reshape(BATCH, SEQ, D_MODEL).astype(x.dtype)
```

```python
import functools
import math

import jax
import jax.numpy as jnp
import numpy as np
from jax import lax
from jax.experimental import pallas as pl
from jax.experimental.pallas import tpu as pltpu

F32 = jnp.float32
BF16 = jnp.bfloat16

D_MODEL = 2048
BATCH = 4
SEQ = 2048
TOKENS = BATCH * SEQ
DEPTH = 4
ROPE_THETA = 10000.0
NORM_EPS = 1e-6
LANES = 128
SUBLANES = 8
BF16_ROWS = 16

A_HEAD_DIM = 128
A_HEADS = 8
A_WIDTH = 1024
A_PATTERNS = ((128, 1), (512, 4), (2048, 16))
B_WIDTH = 1024
B_BLOCKS = 8
B_CONV = 4
LRU_C = 8.0
EVEN_IN = 3 * A_WIDTH + 2 * B_WIDTH

C_HEAD_DIM = 64
C_HEADS = 16
C_KV_HEADS = 2
C_GROUP = 8
C_WIDTH = 1024
C_KV_WIDTH = 128
C_WINDOW = 128
D_WIDTH = 1024
D_GROUP_DIM = 16
D_GROUPS = 64
D_STATE = 64
ODD_IN = C_WIDTH + 2 * C_KV_WIDTH + D_WIDTH

D_FF = 5504
D_FF_PAD = 5632
FFN_CONV = 3

NEG = -1e30

VMEM_LIMIT = 56 * 1024 * 1024


def _cparams(sem, vmem=VMEM_LIMIT, flags=None):
    return pltpu.CompilerParams(dimension_semantics=sem, vmem_limit_bytes=vmem, flags=flags)


ADA_TN = 1024


def _ada_kernel(c_ref, w_ref, b_ref, o_ref):
    c = c_ref[...]
    cond = (c * jax.nn.sigmoid(c)).astype(BF16)
    o_ref[...] = jnp.dot(cond, w_ref[...].astype(BF16), preferred_element_type=F32) + b_ref[...]


def _ada_mod(c, ada_w, ada_b):
    c8 = jnp.zeros((SUBLANES, D_MODEL), F32).at[:BATCH].set(c.astype(F32))
    n = 6 * D_MODEL
    return pl.pallas_call(
        _ada_kernel,
        out_shape=jax.ShapeDtypeStruct((DEPTH, SUBLANES, n), F32),
        grid=(DEPTH, n // ADA_TN),
        in_specs=[
            pl.BlockSpec((SUBLANES, D_MODEL), lambda l, j: (0, 0)),
            pl.BlockSpec((None, D_MODEL, ADA_TN), lambda l, j: (l, 0, j)),
            pl.BlockSpec((None, 1, ADA_TN), lambda l, j: (l, 0, j)),
        ],
        out_specs=pl.BlockSpec((None, SUBLANES, ADA_TN), lambda l, j: (l, 0, j)),
        compiler_params=_cparams(("parallel", "parallel")),
        name="ada_mod",
    )(c8, ada_w, ada_b.reshape(DEPTH, 1, n))


ROPE_TM = 1024


def _rope_kernel(pos_ref, inva_ref, invc_ref, ca_ref, sa_ref, cc_ref, scp_ref, scm_ref):
    pos = pos_ref[...].astype(F32)
    lane = lax.broadcasted_iota(jnp.int32, (ROPE_TM, LANES), 1)
    ang = pos * inva_ref[...]
    s = jnp.sin(ang)
    ca_ref[...] = jnp.cos(ang)
    sa_ref[...] = jnp.where(lane < A_HEAD_DIM // 2, -s, s)
    ang = pos * invc_ref[...]
    s = jnp.sin(ang)
    cc_ref[...] = jnp.cos(ang)
    second = (lane % C_HEAD_DIM) >= C_HEAD_DIM // 2
    scp_ref[...] = jnp.where(second, s, 0.0)
    scm_ref[...] = jnp.where(second, 0.0, -s)


def _rope_tables(positions):
    half_a, half_c = A_HEAD_DIM // 2, C_HEAD_DIM // 2
    inv_a = ROPE_THETA ** (-jnp.arange(half_a, dtype=F32) / half_a)
    inv_c = ROPE_THETA ** (-jnp.arange(half_c, dtype=F32) / half_c)
    inv_a = jnp.tile(inv_a, LANES // half_a).reshape(1, LANES)
    inv_c = jnp.tile(inv_c, LANES // half_c).reshape(1, LANES)
    tab = jax.ShapeDtypeStruct((TOKENS, LANES), F32)
    row = pl.BlockSpec((ROPE_TM, LANES), lambda i: (i, 0))
    vec = pl.BlockSpec((1, LANES), lambda i: (0, 0))
    return pl.pallas_call(
        _rope_kernel,
        out_shape=(tab,) * 5,
        grid=(TOKENS // ROPE_TM,),
        in_specs=[pl.BlockSpec((ROPE_TM, 1), lambda i: (i, 0)), vec, vec],
        out_specs=(row,) * 5,
        compiler_params=_cparams(("parallel",)),
        name="rope_tables",
    )(positions.reshape(TOKENS, 1), inv_a, inv_c)


def _norm_mod(x, g, sh, sc):
    ms = jnp.mean(x * x, axis=-1, keepdims=True)
    y = x * lax.rsqrt(ms + NORM_EPS) * g
    return y * (1.0 + sc) + sh


def _rmsnorm(x, g):
    ms = jnp.mean(x * x, axis=-1, keepdims=True)
    return x * lax.rsqrt(ms + NORM_EPS) * g


NORM_TM = 512


def _prenorm_kernel(x_ref, g_ref, sh_ref, sc_ref, h_ref):
    h_ref[...] = _norm_mod(x_ref[...], g_ref[...], sh_ref[...], sc_ref[...]).astype(BF16)


def _mod_spec(layer, chunk, tm):
    return pl.BlockSpec((None, None, None, 1, D_MODEL),
                        lambda i, *_: (layer, chunk, (i * tm) // SEQ, 0, 0))


def _prenorm(x, norm_g, mod, layer):
    vec = pl.BlockSpec((None, 1, D_MODEL), lambda i: (layer, 0, 0))
    return pl.pallas_call(
        _prenorm_kernel,
        out_shape=jax.ShapeDtypeStruct((TOKENS, D_MODEL), BF16),
        grid=(TOKENS // NORM_TM,),
        in_specs=[pl.BlockSpec((NORM_TM, D_MODEL), lambda i: (i, 0)), vec,
                  _mod_spec(layer, 0, NORM_TM), _mod_spec(layer, 1, NORM_TM)],
        out_specs=pl.BlockSpec((NORM_TM, D_MODEL), lambda i: (i, 0)),
        compiler_params=_cparams(("parallel",)),
        name="prenorm",
    )(x, norm_g.reshape(DEPTH, 1, D_MODEL), mod, mod)


MM_TM = 1024
MM_TN = 512


def _mm_kernel(a_ref, w_ref, o_ref):
    o_ref[...] = jnp.dot(a_ref[...], w_ref[...], preferred_element_type=F32).astype(o_ref.dtype)


def _matmul(a, w, idx, out_dtype, name):
    m, k = a.shape
    n = w.shape[-1]
    tn = MM_TN if n % MM_TN == 0 else 256
    return pl.pallas_call(
        _mm_kernel,
        out_shape=jax.ShapeDtypeStruct((m, n), out_dtype),
        grid=(m // MM_TM, n // tn),
        in_specs=[pl.BlockSpec((MM_TM, k), lambda i, j: (i, 0)),
                  pl.BlockSpec((None, k, tn), lambda i, j: (idx, 0, j))],
        out_specs=pl.BlockSpec((MM_TM, tn), lambda i, j: (i, j)),
        compiler_params=_cparams(("parallel", "parallel")),
        name=name,
    )(a, w)


def _mm_res_kernel(a_ref, w_ref, x_ref, gate_ref, g_ref, sh_ref, sc_ref, *outs, final):
    kdim = w_ref.shape[0]
    y = jnp.dot(a_ref[:, 0:kdim], w_ref[...], preferred_element_type=F32)
    xn = x_ref[...] + gate_ref[...] * y
    if final:
        outs[0][...] = _rmsnorm(xn, g_ref[...])
    else:
        outs[0][...] = xn
        outs[1][...] = _norm_mod(xn, g_ref[...], sh_ref[...], sc_ref[...]).astype(BF16)


def _mm_res(a, w, widx, tm, x, mod, gate_layer, gate_chunk, norm_g, norm_idx, mod_layer, mod_chunk, final, name):
    m, ka = a.shape
    kdim = w.shape[1]
    row = pl.BlockSpec((tm, D_MODEL), lambda i: (i, 0))
    if final:
        gvec = pl.BlockSpec((1, D_MODEL), lambda i: (0, 0))
        g_arr = norm_g.reshape(1, D_MODEL)
        out_shape = jax.ShapeDtypeStruct((m, D_MODEL), F32)
        out_specs = row
    else:
        gvec = pl.BlockSpec((None, 1, D_MODEL), lambda i: (norm_idx, 0, 0))
        g_arr = norm_g.reshape(DEPTH, 1, D_MODEL)
        out_shape = (jax.ShapeDtypeStruct((m, D_MODEL), F32), jax.ShapeDtypeStruct((m, D_MODEL), BF16))
        out_specs = (row, row)
    return pl.pallas_call(
        functools.partial(_mm_res_kernel, final=final),
        out_shape=out_shape,
        grid=(m // tm,),
        in_specs=[pl.BlockSpec((tm, ka), lambda i: (i, 0)),
                  pl.BlockSpec((None, kdim, D_MODEL), lambda i: (widx, 0, 0), pipeline_mode=pl.Buffered(1)),
                  row,
                  _mod_spec(gate_layer, gate_chunk, tm),
                  gvec,
                  _mod_spec(mod_layer, mod_chunk, tm),
                  _mod_spec(mod_layer, mod_chunk + 1, tm)],
        out_specs=out_specs,
        compiler_params=_cparams(("parallel",)),
        name=name,
    )(a, w, x, mod, g_arr, mod, mod)


FFN_TM = 1024
FFN_TF = 512
FFN_HALO = BF16_ROWS
FFN_NT = TOKENS // FFN_TM
FFN_NF = D_FF_PAD // FFN_TF
FFN_STEPS = FFN_NT * FFN_NF
FFN_VQ = FFN_TF // LANES
FFN_ROWS = 64


def _ffn_up_kernel(h_ref, halo_ref, wg_ref, *rest):
    wv_refs = rest[0:FFN_VQ]
    cwg_ref = rest[FFN_VQ]
    cwv_refs = rest[FFN_VQ + 1:2 * FFN_VQ + 1]
    cbg_ref = rest[2 * FFN_VQ + 1]
    cbv_refs = rest[2 * FFN_VQ + 2:3 * FFN_VQ + 2]
    o_ref, hcat_ref, wv_ref, ug_a, uv_a, ug_b, uv_b = rest[3 * FFN_VQ + 2:]
    s = pl.program_id(0)

    @pl.when(s == 0)
    def _():
        ug_b[...] = jnp.zeros_like(ug_b)
        uv_b[...] = jnp.zeros_like(uv_b)

    @pl.when((s % FFN_NF == 0) & (s < FFN_STEPS))
    def _():
        hcat_ref[0:FFN_HALO, :] = halo_ref[...]
        hcat_ref[FFN_HALO:, :] = h_ref[...]

    jp = jnp.maximum(s - 1, 0) % FFN_NF

    def step(new_g, new_v, old_g, old_v):
        pad = SUBLANES

        def conv(u_ref, rows0, cols, w, b):
            x = u_ref[pl.ds(rows0 - pad, FFN_ROWS + pad), cols]
            out = b + w[FFN_CONV - 1:FFN_CONV, :] * x[pad:, :]
            for i in range(1, FFN_CONV):
                out = out + w[FFN_CONV - 1 - i:FFN_CONV - i, :] * pltpu.roll(x, i, 0)[pad:, :]
            return out

        for q in range(FFN_VQ):
            cols = slice(q * LANES, (q + 1) * LANES)
            valid = jp * FFN_TF + q * LANES < D_FF
            wg = cwg_ref[:, cols]
            wv = cwv_refs[q][...]
            bg = cbg_ref[:, cols]
            bv = cbv_refs[q][...]
            for r in range(FFN_TM // FFN_ROWS):
                rows0 = FFN_HALO + r * FFN_ROWS
                g = conv(old_g, rows0, cols, wg, bg)
                v = conv(old_v, rows0, cols, wv, bv)
                act = jnp.where(valid, jax.nn.gelu(g) * v, 0.0)
                o_ref[r * FFN_ROWS:(r + 1) * FFN_ROWS, cols] = act.astype(BF16)

        lhs = hcat_ref[...]
        for q in range(FFN_VQ):
            wv_ref[:, q * LANES:(q + 1) * LANES] = wv_refs[q][...]
        new_g[...] = jnp.dot(lhs, wg_ref[...], preferred_element_type=F32)
        new_v[...] = jnp.dot(lhs, wv_ref[...], preferred_element_type=F32)

    @pl.when(s % 2 == 0)
    def _():
        step(ug_a, uv_a, ug_b, uv_b)

    @pl.when(s % 2 == 1)
    def _():
        step(ug_b, uv_b, ug_a, uv_a)


def _ffn_up(h, w_in, conv_w, conv_b, layer):
    tiles = h.reshape(FFN_NT, FFN_TM, D_MODEL)[:, FFN_TM - FFN_HALO:, :]
    prev = jnp.concatenate([jnp.zeros_like(tiles[:1]), tiles[:-1]], axis=0)
    starts_seq = (jnp.arange(FFN_NT) * FFN_TM) % SEQ == 0
    halo = jnp.where(starts_seq[:, None, None], jnp.zeros_like(prev), prev)

    def cur(s):
        return jnp.minimum(s, FFN_STEPS - 1)

    def prv(s):
        return jnp.maximum(s - 1, 0)

    vblocks = D_FF // LANES
    last = 2 * D_FF // LANES - 1

    def vcol(t, q):
        return jnp.minimum(vblocks + FFN_VQ * (t % FFN_NF) + q, last)

    in_specs = [pl.BlockSpec((FFN_TM, D_MODEL), lambda s: (cur(s) // FFN_NF, 0)),
                pl.BlockSpec((None, FFN_HALO, D_MODEL), lambda s: (cur(s) // FFN_NF, 0, 0)),
                pl.BlockSpec((None, D_MODEL, FFN_TF), lambda s: (layer, 0, cur(s) % FFN_NF))]
    in_specs += [pl.BlockSpec((None, D_MODEL, LANES), lambda s, q=q: (layer, 0, vcol(cur(s), q)))
                 for q in range(FFN_VQ)]
    for rows, arr in ((FFN_CONV, conv_w), (1, conv_b)):
        in_specs.append(pl.BlockSpec((None, rows, FFN_TF), lambda s: (layer, 0, prv(s) % FFN_NF)))
        in_specs += [pl.BlockSpec((None, rows, LANES), lambda s, q=q: (layer, 0, vcol(prv(s), q)))
                     for q in range(FFN_VQ)]
    u_buf = pltpu.VMEM((FFN_HALO + FFN_TM, FFN_TF), F32)
    args = [h, halo, w_in] + [w_in] * FFN_VQ + [conv_w] * (FFN_VQ + 1) + [conv_b] * (FFN_VQ + 1)
    return pl.pallas_call(
        _ffn_up_kernel,
        out_shape=jax.ShapeDtypeStruct((TOKENS, D_FF_PAD), BF16),
        grid=(FFN_STEPS + 1,),
        in_specs=in_specs,
        out_specs=pl.BlockSpec((FFN_TM, FFN_TF), lambda s: (prv(s) // FFN_NF, prv(s) % FFN_NF)),
        scratch_shapes=[pltpu.VMEM((FFN_HALO + FFN_TM, D_MODEL), BF16),
                        pltpu.VMEM((D_MODEL, FFN_TF), BF16),
                        u_buf, u_buf, u_buf, u_buf],
        compiler_params=_cparams(("arbitrary",)),
        name="ffn_up",
    )(*args)


ATT_T = 256
ATT_NBIAS = 4


def _dilated_bias_tiles():
    tiles = np.zeros((ATT_NBIAS, ATT_T, ATT_T), np.float32)
    qi = np.arange(ATT_T)[:, None]
    kj = np.arange(ATT_T)[None, :]
    for off in range(ATT_NBIAS):
        delta = off * ATT_T + qi - kj
        count = np.zeros_like(delta)
        for window, dil in A_PATTERNS:
            count += ((delta >= 0) & (delta <= window) & (delta % dil == 0)).astype(delta.dtype)
        tiles[off] = np.where(count > 0, np.log(np.maximum(count, 1)), NEG)
    return tiles


def _attn_a_kernel(q_ref, k_ref, v_ref, cos_ref, sin_ref, bias_ref, o_ref, qs_ref, ks_ref, vs_ref):
    cos = cos_ref[...]
    sin = sin_ref[...]
    half = A_HEAD_DIM // 2
    q = q_ref[...]
    k = k_ref[...]
    scale = A_HEAD_DIM ** -0.5
    qs_ref[...] = ((q * cos + pltpu.roll(q, half, 1) * sin) * scale).astype(BF16)
    ks_ref[...] = (k * cos + pltpu.roll(k, half, 1) * sin).astype(BF16)
    vs_ref[...] = v_ref[...].astype(BF16)

    for i in range(SEQ // ATT_T):
        n = (i + 1) * ATT_T
        q_blk = qs_ref[i * ATT_T:n, :]
        s = lax.dot_general(q_blk, ks_ref[0:n, :], (((1,), (1,)), ((), ())), preferred_element_type=F32)
        s = s + jnp.concatenate([bias_ref[min(i - j, ATT_NBIAS - 1)] for j in range(i + 1)], axis=1)
        m = jnp.max(s, axis=-1, keepdims=True)
        p = jnp.exp(s - m)
        l = jnp.sum(p, axis=-1, keepdims=True)
        acc = jnp.dot(p.astype(BF16), vs_ref[0:n, :], preferred_element_type=F32)
        o_ref[i * ATT_T:n, :] = (acc / l).astype(o_ref.dtype)


def _attn_a(proj, cos_a, sin_a):
    bias = jnp.asarray(_dilated_bias_tiles())
    tab = pl.BlockSpec((SEQ, LANES), lambda b, h: (b, 0))
    return pl.pallas_call(
        _attn_a_kernel,
        out_shape=jax.ShapeDtypeStruct((TOKENS, A_WIDTH), BF16),
        grid=(BATCH, A_HEADS),
        in_specs=[pl.BlockSpec((SEQ, A_HEAD_DIM), lambda b, h: (b, h)),
                  pl.BlockSpec((SEQ, A_HEAD_DIM), lambda b, h: (b, A_HEADS + h)),
                  pl.BlockSpec((SEQ, A_HEAD_DIM), lambda b, h: (b, 2 * A_HEADS + h)),
                  tab, tab,
                  pl.BlockSpec((ATT_NBIAS, ATT_T, ATT_T), lambda b, h: (0, 0, 0))],
        out_specs=pl.BlockSpec((SEQ, A_HEAD_DIM), lambda b, h: (b, h)),
        scratch_shapes=[pltpu.VMEM((SEQ, A_HEAD_DIM), BF16)] * 3,
        compiler_params=_cparams(("parallel", "parallel")),
        name="attn_dilated",
    )(proj, proj, proj, cos_a, sin_a, bias)


LRU_TS = 512
LRU_HALO = SUBLANES


def _lru_kernel(xb_ref, yb_ref, cw_ref, cb_ref, ga_ref, gab_ref, gx_ref, gxb_ref, lam_ref, o_ref,
                ext_ref, a_ref, b_ref, carry_ref):
    t = pl.program_id(1)

    @pl.when(t == 0)
    def _():
        ext_ref[0:LRU_HALO, :] = jnp.zeros((LRU_HALO, B_WIDTH), F32)
        carry_ref[...] = jnp.zeros_like(carry_ref)

    ext_ref[LRU_HALO:, :] = xb_ref[...]
    ext = ext_ref[...]
    base = LRU_HALO - (B_CONV - 1)
    xc = cb_ref[...]
    for i in range(B_CONV):
        xc = xc + cw_ref[i:i + 1, :] * ext[base + i:base + i + LRU_TS, :]
    ext_ref[0:LRU_HALO, :] = ext[LRU_TS:LRU_TS + LRU_HALO, :]

    lam = lam_ref[...]
    neg_sp = -LRU_C * (jnp.maximum(-lam, 0.0) + jnp.log1p(jnp.exp(-jnp.abs(lam))))
    width = B_WIDTH // B_BLOCKS
    for blk in range(B_BLOCKS):
        sl = slice(blk * width, (blk + 1) * width)
        xh = xc[:, sl]
        xh16 = xh.astype(BF16)
        r = jax.nn.sigmoid(jnp.dot(xh16, ga_ref[blk], preferred_element_type=F32) + gab_ref[:, sl])
        gi = jax.nn.sigmoid(jnp.dot(xh16, gx_ref[blk], preferred_element_type=F32) + gxb_ref[:, sl])
        log_a = r * neg_sp[:, sl]
        a_ref[:, sl] = jnp.exp(log_a)
        th = jnp.tanh(log_a)
        b_ref[:, sl] = jnp.sqrt(-2.0 * th / (1.0 - th)) * (gi * xh)

    row = lax.broadcasted_iota(jnp.int32, (SUBLANES, B_WIDTH), 0)

    def scan_body(g, h_prev):
        rows = pl.ds(pl.multiple_of(g * SUBLANES, SUBLANES), SUBLANES)
        a = a_ref[rows, :]
        b = b_ref[rows, :]
        for s in (1, 2, 4):
            keep = row >= s
            a_sh = jnp.where(keep, pltpu.roll(a, s, 0), 1.0)
            b_sh = jnp.where(keep, pltpu.roll(b, s, 0), 0.0)
            b = a * b_sh + b
            a = a * a_sh
        h = a * h_prev + b
        b_ref[rows, :] = h
        return jnp.broadcast_to(h[SUBLANES - 1:SUBLANES, :], (SUBLANES, B_WIDTH))

    carry_ref[...] = lax.fori_loop(0, LRU_TS // SUBLANES, scan_body, carry_ref[...])
    o_ref[...] = (b_ref[...] * jax.nn.gelu(yb_ref[...])).astype(o_ref.dtype)


def _lru(proj, conv_w, conv_b, ga_w, ga_b, gx_w, gx_b, lam, e):
    nts = SEQ // LRU_TS
    vec = pl.BlockSpec((None, 1, B_WIDTH), lambda b, t: (e, 0, 0))
    gate = pl.BlockSpec((None, B_BLOCKS, B_WIDTH // B_BLOCKS, B_WIDTH // B_BLOCKS), lambda b, t: (e, 0, 0, 0))
    r3 = lambda a: a.reshape(a.shape[0], 1, B_WIDTH)
    return pl.pallas_call(
        _lru_kernel,
        out_shape=jax.ShapeDtypeStruct((TOKENS, B_WIDTH), BF16),
        grid=(BATCH, nts),
        in_specs=[pl.BlockSpec((LRU_TS, B_WIDTH), lambda b, t: (b * nts + t, 3)),
                  pl.BlockSpec((LRU_TS, B_WIDTH), lambda b, t: (b * nts + t, 4)),
                  pl.BlockSpec((None, B_CONV, B_WIDTH), lambda b, t: (e, 0, 0)),
                  vec, gate, vec, gate, vec, vec],
        out_specs=pl.BlockSpec((LRU_TS, B_WIDTH), lambda b, t: (b * nts + t, 0)),
        scratch_shapes=[pltpu.VMEM((LRU_HALO + LRU_TS, B_WIDTH), F32),
                        pltpu.VMEM((LRU_TS, B_WIDTH), F32),
                        pltpu.VMEM((LRU_TS, B_WIDTH), F32),
                        pltpu.VMEM((SUBLANES, B_WIDTH), F32)],
        compiler_params=_cparams(("parallel", "arbitrary")),
        name="rg_lru",
    )(proj, proj, conv_w, r3(conv_b), ga_w, r3(ga_b), gx_w, r3(gx_b), r3(lam))


SWA_T = 128
SWA_PAIRS = C_GROUP // 2


def _swa_bias_tiles():
    qi = np.tile(np.arange(SWA_T), SWA_PAIRS)[:, None]
    kj = np.arange(2 * SWA_T)[None, :]
    delta = qi + SWA_T - kj
    band = (delta >= 0) & (delta <= C_WINDOW - 1)
    tiles = np.zeros((2, SWA_PAIRS * SWA_T, 2 * SWA_T), np.float32)
    tiles[0] = np.where(band & (kj >= SWA_T), 0.0, NEG)
    tiles[1] = np.where(band, 0.0, NEG)
    return tiles


def _swa_kernel(sink_ref, q_ref, kv_ref, cos_ref, sp_ref, sm_ref, bias_ref, o_ref,
                qs_ref, ka_ref, kb_ref, va_ref, vb_ref):
    kvh = pl.program_id(1)
    cos = cos_ref[...]
    s_plus = sp_ref[...]
    s_minus = sm_ref[...]
    quarter = C_HEAD_DIM // 2

    def rope(x):
        return x * cos + pltpu.roll(x, quarter, 1) * s_plus + pltpu.roll(x, LANES - quarter, 1) * s_minus

    scale = C_HEAD_DIM ** -0.5
    for j in range(SWA_PAIRS):
        sl = slice(j * LANES, (j + 1) * LANES)
        qs_ref[:, sl] = (rope(q_ref[:, sl]) * scale).astype(BF16)

    lane = lax.broadcasted_iota(jnp.int32, (SEQ, LANES), 1)
    low = lane < C_HEAD_DIM
    kk = rope(kv_ref[:, 0:LANES])
    vv = kv_ref[:, LANES:2 * LANES]
    kk = jnp.where(kvh == 0, kk, pltpu.roll(kk, C_HEAD_DIM, 1))
    vv = jnp.where(kvh == 0, vv, pltpu.roll(vv, C_HEAD_DIM, 1))
    k_lo = jnp.where(low, kk, 0.0)
    v_lo = jnp.where(low, vv, 0.0)
    zeros = jnp.zeros((SWA_T, LANES), BF16)
    for ref, val in ((ka_ref, k_lo), (kb_ref, pltpu.roll(k_lo, C_HEAD_DIM, 1)),
                     (va_ref, v_lo), (vb_ref, pltpu.roll(v_lo, C_HEAD_DIM, 1))):
        ref[0:SWA_T, :] = zeros
        ref[SWA_T:, :] = val.astype(BF16)

    rows_st = SWA_PAIRS * SWA_T
    pair = lax.broadcasted_iota(jnp.int32, (rows_st, 1), 0) // SWA_T
    sink_a = jnp.zeros((rows_st, 1), F32)
    sink_b = jnp.zeros((rows_st, 1), F32)
    for j in range(SWA_PAIRS):
        sink_a = jnp.where(pair == j, sink_ref[kvh * C_GROUP + 2 * j], sink_a)
        sink_b = jnp.where(pair == j, sink_ref[kvh * C_GROUP + 2 * j + 1], sink_b)

    def q_body(i, carry):
        r0 = pl.multiple_of(i * SWA_T, SWA_T)
        q_st = jnp.concatenate([qs_ref[pl.ds(r0, SWA_T), j * LANES:(j + 1) * LANES]
                                for j in range(SWA_PAIRS)], axis=0)
        bias = bias_ref[jnp.minimum(i, 1)]
        win = pl.ds(r0, 2 * SWA_T)
        out = jnp.zeros((rows_st, LANES), F32)
        for k_ref, v_ref, sink in ((ka_ref, va_ref, sink_a), (kb_ref, vb_ref, sink_b)):
            s = lax.dot_general(q_st, k_ref[win, :], (((1,), (1,)), ((), ())),
                                preferred_element_type=F32) + bias
            m = jnp.maximum(jnp.max(s, axis=-1, keepdims=True), sink)
            p = jnp.exp(s - m)
            den = jnp.sum(p, axis=-1, keepdims=True) + jnp.exp(sink - m)
            out = out + jnp.dot(p.astype(BF16), v_ref[win, :], preferred_element_type=F32) / den
        for j in range(SWA_PAIRS):
            o_ref[pl.ds(r0, SWA_T), j * LANES:(j + 1) * LANES] = out[j * SWA_T:(j + 1) * SWA_T, :].astype(o_ref.dtype)
        return carry

    lax.fori_loop(0, SEQ // SWA_T, q_body, 0)


def _swa(proj, sinks, cos_c, sin_cp, sin_cm, o):
    bias = jnp.asarray(_swa_bias_tiles())
    qw = C_WIDTH // C_KV_HEADS
    tab = pl.BlockSpec((SEQ, LANES), lambda b, g, *_: (b, 0))
    kv_block = C_WIDTH // (2 * C_KV_WIDTH)
    grid_spec = pltpu.PrefetchScalarGridSpec(
        num_scalar_prefetch=1,
        grid=(BATCH, C_KV_HEADS),
        in_specs=[pl.BlockSpec((SEQ, qw), lambda b, g, *_: (b, g)),
                  pl.BlockSpec((SEQ, 2 * C_KV_WIDTH), lambda b, g, *_: (b, kv_block)),
                  tab, tab, tab,
                  pl.BlockSpec((2, SWA_PAIRS * SWA_T, 2 * SWA_T), lambda b, g, *_: (0, 0, 0))],
        out_specs=pl.BlockSpec((SEQ, qw), lambda b, g, *_: (b, g)),
        scratch_shapes=[pltpu.VMEM((SEQ, qw), BF16)] + [pltpu.VMEM((SWA_T + SEQ, LANES), BF16)] * 4,
    )
    return pl.pallas_call(
        _swa_kernel,
        out_shape=jax.ShapeDtypeStruct((TOKENS, C_WIDTH), BF16),
        grid_spec=grid_spec,
        compiler_params=_cparams(("parallel", "parallel")),
        name="attn_swa",
    )(sinks[o].astype(F32), proj, proj, cos_c, sin_cp, sin_cm, bias)


S5_TS = 512
S5_GPB = LANES // D_GROUP_DIM
S5_NB = D_WIDTH // LANES
S5_SW = S5_GPB * D_STATE


def _s5_kernel(u_ref, wre_ref, wim_ref, cre_ref, cim_ref, dec_ref, d_ref, o_ref, sre_ref, sim_ref, carry_ref):
    t = pl.program_id(2)

    @pl.when(t == 0)
    def _():
        carry_ref[...] = jnp.zeros_like(carry_ref)

    u = u_ref[...]
    u16 = u.astype(BF16)
    ng = S5_TS // SUBLANES
    x_re = jnp.dot(u16, wre_ref[...], preferred_element_type=F32).reshape(ng, SUBLANES, S5_SW)
    x_im = jnp.dot(u16, wim_ref[...], preferred_element_type=F32).reshape(ng, SUBLANES, S5_SW)
    for idx, s in enumerate((1, 2, 4)):
        m_re = dec_ref[2 * idx]
        m_im = dec_ref[2 * idx + 1]
        r_re = pltpu.roll(x_re, s, 1)
        r_im = pltpu.roll(x_im, s, 1)
        x_re, x_im = x_re + (m_re * r_re - m_im * r_im), x_im + (m_re * r_im + m_im * r_re)
    sre_ref[...] = x_re.reshape(S5_TS, S5_SW)
    sim_ref[...] = x_im.reshape(S5_TS, S5_SW)
    p_re = dec_ref[6]
    p_im = dec_ref[7]

    def carry_body(g, c):
        c_re, c_im = c
        rows = pl.ds(pl.multiple_of(g * SUBLANES, SUBLANES), SUBLANES)
        s_re = sre_ref[rows, :] + (p_re * c_re - p_im * c_im)
        s_im = sim_ref[rows, :] + (p_re * c_im + p_im * c_re)
        sre_ref[rows, :] = s_re
        sim_ref[rows, :] = s_im
        last = slice(SUBLANES - 1, SUBLANES)
        return (jnp.broadcast_to(s_re[last, :], (SUBLANES, S5_SW)),
                jnp.broadcast_to(s_im[last, :], (SUBLANES, S5_SW)))

    c_re, c_im = lax.fori_loop(0, ng, carry_body, (carry_ref[0], carry_ref[1]))
    carry_ref[0] = c_re
    carry_ref[1] = c_im
    y = (jnp.dot(sre_ref[...].astype(BF16), cre_ref[...], preferred_element_type=F32)
         - jnp.dot(sim_ref[...].astype(BF16), cim_ref[...], preferred_element_type=F32)
         + d_ref[...] * u)
    o_ref[...] = jax.nn.gelu(y).astype(o_ref.dtype)


def _s5_tables(a_re, a_im, b_re, b_im, c_re, c_im, log_dt):
    dt = jnp.exp(log_dt.astype(F32))[:, None]
    lr, li = a_re.astype(F32), a_im.astype(F32)
    zr, zi = lr * dt, li * dt

    def a_pow(k):
        mag = jnp.exp(k * zr)
        return mag * jnp.cos(k * zi), mag * jnp.sin(k * zi)

    ar, ai = a_pow(1.0)
    den = lr * lr + li * li
    cr = ((ar - 1.0) * lr + ai * li) / den
    ci = (ai * lr - (ar - 1.0) * li) / den
    bb_re = cr[..., None] * b_re - ci[..., None] * b_im
    bb_im = cr[..., None] * b_im + ci[..., None] * b_re
    eye = jnp.eye(S5_GPB, dtype=F32)

    def in_blocks(bb):
        x = bb.reshape(S5_NB, S5_GPB, D_STATE, D_GROUP_DIM).transpose(0, 1, 3, 2)
        return jnp.einsum('ngcp,gh->ngchp', x, eye).reshape(S5_NB, LANES, S5_SW).astype(BF16)

    def out_blocks(cc):
        x = cc.astype(F32).reshape(S5_NB, S5_GPB, D_GROUP_DIM, D_STATE).transpose(0, 1, 3, 2)
        return jnp.einsum('ngpc,gh->ngphc', x, eye).reshape(S5_NB, S5_SW, LANES).astype(BF16)

    rows = jnp.arange(SUBLANES)[:, None]
    tabs = []
    for s in (1, 2, 4):
        pr, pi = a_pow(float(s))
        for p in (pr, pi):
            tabs.append(jnp.where(rows >= s, p.reshape(S5_NB, 1, S5_SW), 0.0))
    zr_b, zi_b = zr.reshape(S5_NB, 1, S5_SW), zi.reshape(S5_NB, 1, S5_SW)
    kk = (rows + 1).astype(F32)
    mag = jnp.exp(kk * zr_b)
    tabs.append(mag * jnp.cos(kk * zi_b))
    tabs.append(mag * jnp.sin(kk * zi_b))
    dec = jnp.stack(tabs, axis=1)
    return in_blocks(bb_re), in_blocks(bb_im), out_blocks(c_re), out_blocks(c_im), dec


def _s5(proj, tables, d_skip, o):
    wre, wim, cre, cim, dec = tables
    nts = SEQ // S5_TS
    ublock = (C_WIDTH + 2 * C_KV_WIDTH) // LANES
    win = pl.BlockSpec((None, LANES, S5_SW), lambda b, n, t: (n, 0, 0))
    wout = pl.BlockSpec((None, S5_SW, LANES), lambda b, n, t: (n, 0, 0))
    return pl.pallas_call(
        _s5_kernel,
        out_shape=jax.ShapeDtypeStruct((TOKENS, D_WIDTH), BF16),
        grid=(BATCH, S5_NB, nts),
        in_specs=[pl.BlockSpec((S5_TS, LANES), lambda b, n, t: (b * nts + t, ublock + n)),
                  win, win, wout, wout,
                  pl.BlockSpec((None, 8, SUBLANES, S5_SW), lambda b, n, t: (n, 0, 0, 0)),
                  pl.BlockSpec((None, 1, LANES), lambda b, n, t: (o, 0, n))],
        out_specs=pl.BlockSpec((S5_TS, LANES), lambda b, n, t: (b * nts + t, n)),
        scratch_shapes=[pltpu.VMEM((S5_TS, S5_SW), F32), pltpu.VMEM((S5_TS, S5_SW), F32),
                        pltpu.VMEM((2, SUBLANES, S5_SW), F32)],
        compiler_params=_cparams(("parallel", "parallel", "arbitrary")),
        name="s5_ssm",
    )(proj, wre, wim, cre, cim, dec, d_skip.reshape(-1, 1, D_WIDTH))


GLU_TN = 512


def _glu_kernel(z_ref, w_ref, b_ref, zc_ref, o_ref):
    gate = jax.nn.sigmoid(jnp.dot(z_ref[...], w_ref[...], preferred_element_type=F32) + b_ref[...])
    o_ref[...] = (zc_ref[...].astype(F32) * gate).astype(o_ref.dtype)


def _glu(z, w, b, o):
    return pl.pallas_call(
        _glu_kernel,
        out_shape=jax.ShapeDtypeStruct((TOKENS, D_WIDTH), BF16),
        grid=(TOKENS // MM_TM, D_WIDTH // GLU_TN),
        in_specs=[pl.BlockSpec((MM_TM, D_WIDTH), lambda i, j: (i, 0)),
                  pl.BlockSpec((None, D_WIDTH, GLU_TN), lambda i, j: (o, 0, j)),
                  pl.BlockSpec((None, 1, GLU_TN), lambda i, j: (o, 0, j)),
                  pl.BlockSpec((MM_TM, GLU_TN), lambda i, j: (i, j))],
        out_specs=pl.BlockSpec((MM_TM, GLU_TN), lambda i, j: (i, j)),
        compiler_params=_cparams(("parallel", "parallel")),
        name="s5_glu",
    )(z, w, b.reshape(-1, 1, D_WIDTH), z)


OUT_TM = 512
DOWN_TM = 256


def kernel(x, c, positions, ada_w, ada_b, norm_mix, norm_ffn, norm_final, ev_w_in, ev_conv_w, ev_conv_b, ev_gate_a_w, ev_gate_a_b, ev_gate_x_w, ev_gate_x_b, ev_lambda, ev_w_out, od_w_in, od_sinks, od_a_re, od_a_im, od_b_re, od_b_im, od_c_re, od_c_im, od_d, od_log_dt, od_glu_w, od_glu_b, od_w_out, ffn_w_in, ffn_conv_w, ffn_conv_b, ffn_w_out):
    ev_w_in16 = ev_w_in.astype(BF16)
    ev_w_out16 = ev_w_out.astype(BF16)
    od_w_in16 = od_w_in.astype(BF16)
    od_w_out16 = od_w_out.astype(BF16)
    od_glu16 = od_glu_w.astype(BF16)
    ga16 = ev_gate_a_w.astype(BF16)
    gx16 = ev_gate_x_w.astype(BF16)
    ffn_in16 = ffn_w_in.astype(BF16)
    ffn_out16 = ffn_w_out.astype(BF16)
    ffn_cw = ffn_conv_w.astype(F32)
    ffn_cb = ffn_conv_b.astype(F32).reshape(DEPTH, 1, 2 * D_FF)

    mod = _ada_mod(c, ada_w, ada_b)
    mod = mod.reshape(DEPTH, SUBLANES, 6, 1, D_MODEL).transpose(0, 2, 1, 3, 4)
    cos_a, sin_a, cos_c, sin_cp, sin_cm = _rope_tables(positions)

    xt = x.reshape(TOKENS, D_MODEL).astype(F32)
    h = _prenorm(xt, norm_mix, mod, 0)
    for layer in range(DEPTH):
        idx = layer // 2
        if layer % 2 == 0:
            proj = _matmul(h, ev_w_in16, idx, F32, "even_in_proj")
            attn = _attn_a(proj, cos_a, sin_a)
            lru = _lru(proj, ev_conv_w, ev_conv_b, ga16, ev_gate_a_b, gx16, ev_gate_x_b, ev_lambda, idx)
            mix = jnp.concatenate([attn, lru], axis=1)
            w_out = ev_w_out16
        else:
            proj = _matmul(h, od_w_in16, idx, F32, "odd_in_proj")
            attn = _swa(proj, od_sinks, cos_c, sin_cp, sin_cm, idx)
            tables = _s5_tables(od_a_re[idx], od_a_im[idx], od_b_re[idx], od_b_im[idx],
                                od_c_re[idx], od_c_im[idx], od_log_dt[idx])
            z = _s5(proj, tables, od_d, idx)
            ssm = _glu(z, od_glu16, od_glu_b, idx)
            mix = jnp.concatenate([attn, ssm], axis=1)
            w_out = od_w_out16
        xt, h2 = _mm_res(mix, w_out, idx, OUT_TM, xt, mod, layer, 2, norm_ffn, layer, layer, 3, False,
                         "mix_out_proj")
        act = _ffn_up(h2, ffn_in16, ffn_cw, ffn_cb, layer)
        if layer + 1 < DEPTH:
            xt, h = _mm_res(act, ffn_out16, layer, DOWN_TM, xt, mod, layer, 5, norm_mix, layer + 1, layer + 1, 0,
                            False, "ffn_down_proj")
        else:
            out = _mm_res(act, ffn_out16, layer, DOWN_TM, xt, mod, layer, 5, norm_final, 0, layer, 0, True,
                          "ffn_down_final")
    return out.reshape(BATCH, SEQ, D_MODEL).astype(x.dtype)
```

```python
import functools
import math

import jax
import jax.numpy as jnp
import numpy as np
from jax import lax
from jax.experimental import pallas as pl
from jax.experimental.pallas import tpu as pltpu

F32 = jnp.float32
BF16 = jnp.bfloat16

D_MODEL = 2048
BATCH = 4
SEQ = 2048
TOKENS = BATCH * SEQ
DEPTH = 4
ROPE_THETA = 10000.0
NORM_EPS = 1e-6
LANES = 128
SUBLANES = 8
BF16_ROWS = 16

A_HEAD_DIM = 128
A_HEADS = 8
A_WIDTH = 1024
A_PATTERNS = ((128, 1), (512, 4), (2048, 16))
B_WIDTH = 1024
B_BLOCKS = 8
B_CONV = 4
LRU_C = 8.0
EVEN_IN = 3 * A_WIDTH + 2 * B_WIDTH

C_HEAD_DIM = 64
C_HEADS = 16
C_KV_HEADS = 2
C_GROUP = 8
C_WIDTH = 1024
C_KV_WIDTH = 128
C_WINDOW = 128
D_WIDTH = 1024
D_GROUP_DIM = 16
D_GROUPS = 64
D_STATE = 64
ODD_IN = C_WIDTH + 2 * C_KV_WIDTH + D_WIDTH

D_FF = 5504
D_FF_PAD = 5632
FFN_CONV = 3

NEG = -1e30

VMEM_LIMIT = 56 * 1024 * 1024


def _cparams(sem, vmem=VMEM_LIMIT, flags=None):
    return pltpu.CompilerParams(dimension_semantics=sem, vmem_limit_bytes=vmem, flags=flags)


ADA_TN = 1024


def _ada_kernel(c_ref, w_ref, b_ref, o_ref):
    c = c_ref[...]
    cond = (c * jax.nn.sigmoid(c)).astype(BF16)
    o_ref[...] = jnp.dot(cond, w_ref[...].astype(BF16), preferred_element_type=F32) + b_ref[...]


def _ada_mod(c, ada_w, ada_b):
    c8 = jnp.zeros((SUBLANES, D_MODEL), F32).at[:BATCH].set(c.astype(F32))
    n = 6 * D_MODEL
    return pl.pallas_call(
        _ada_kernel,
        out_shape=jax.ShapeDtypeStruct((DEPTH, SUBLANES, n), F32),
        grid=(DEPTH, n // ADA_TN),
        in_specs=[
            pl.BlockSpec((SUBLANES, D_MODEL), lambda l, j: (0, 0)),
            pl.BlockSpec((None, D_MODEL, ADA_TN), lambda l, j: (l, 0, j)),
            pl.BlockSpec((None, 1, ADA_TN), lambda l, j: (l, 0, j)),
        ],
        out_specs=pl.BlockSpec((None, SUBLANES, ADA_TN), lambda l, j: (l, 0, j)),
        compiler_params=_cparams(("parallel", "parallel")),
        name="ada_mod",
    )(c8, ada_w, ada_b.reshape(DEPTH, 1, n))


ROPE_TM = 1024


def _rope_kernel(pos_ref, inva_ref, invc_ref, ca_ref, sa_ref, cc_ref, scp_ref, scm_ref):
    pos = pos_ref[...].astype(F32)
    lane = lax.broadcasted_iota(jnp.int32, (ROPE_TM, LANES), 1)
    ang = pos * inva_ref[...]
    s = jnp.sin(ang)
    ca_ref[...] = jnp.cos(ang)
    sa_ref[...] = jnp.where(lane < A_HEAD_DIM // 2, -s, s)
    ang = pos * invc_ref[...]
    s = jnp.sin(ang)
    cc_ref[...] = jnp.cos(ang)
    second = (lane % C_HEAD_DIM) >= C_HEAD_DIM // 2
    scp_ref[...] = jnp.where(second, s, 0.0)
    scm_ref[...] = jnp.where(second, 0.0, -s)


def _rope_tables(positions):
    half_a, half_c = A_HEAD_DIM // 2, C_HEAD_DIM // 2
    inv_a = ROPE_THETA ** (-jnp.arange(half_a, dtype=F32) / half_a)
    inv_c = ROPE_THETA ** (-jnp.arange(half_c, dtype=F32) / half_c)
    inv_a = jnp.tile(inv_a, LANES // half_a).reshape(1, LANES)
    inv_c = jnp.tile(inv_c, LANES // half_c).reshape(1, LANES)
    tab = jax.ShapeDtypeStruct((TOKENS, LANES), F32)
    row = pl.BlockSpec((ROPE_TM, LANES), lambda i: (i, 0))
    vec = pl.BlockSpec((1, LANES), lambda i: (0, 0))
    return pl.pallas_call(
        _rope_kernel,
        out_shape=(tab,) * 5,
        grid=(TOKENS // ROPE_TM,),
        in_specs=[pl.BlockSpec((ROPE_TM, 1), lambda i: (i, 0)), vec, vec],
        out_specs=(row,) * 5,
        compiler_params=_cparams(("parallel",)),
        name="rope_tables",
    )(positions.reshape(TOKENS, 1), inv_a, inv_c)


def _norm_mod(x, g, sh, sc):
    ms = jnp.mean(x * x, axis=-1, keepdims=True)
    y = x * lax.rsqrt(ms + NORM_EPS) * g
    return y * (1.0 + sc) + sh


def _rmsnorm(x, g):
    ms = jnp.mean(x * x, axis=-1, keepdims=True)
    return x * lax.rsqrt(ms + NORM_EPS) * g


NORM_TM = 512


def _prenorm_kernel(x_ref, g_ref, sh_ref, sc_ref, h_ref):
    h_ref[...] = _norm_mod(x_ref[...], g_ref[...], sh_ref[...], sc_ref[...]).astype(BF16)


def _mod_spec(layer, chunk, tm):
    return pl.BlockSpec((None, None, None, 1, D_MODEL),
                        lambda i, *_: (layer, chunk, (i * tm) // SEQ, 0, 0))


def _prenorm(x, norm_g, mod, layer):
    vec = pl.BlockSpec((None, 1, D_MODEL), lambda i: (layer, 0, 0))
    return pl.pallas_call(
        _prenorm_kernel,
        out_shape=jax.ShapeDtypeStruct((TOKENS, D_MODEL), BF16),
        grid=(TOKENS // NORM_TM,),
        in_specs=[pl.BlockSpec((NORM_TM, D_MODEL), lambda i: (i, 0)), vec,
                  _mod_spec(layer, 0, NORM_TM), _mod_spec(layer, 1, NORM_TM)],
        out_specs=pl.BlockSpec((NORM_TM, D_MODEL), lambda i: (i, 0)),
        compiler_params=_cparams(("parallel",)),
        name="prenorm",
    )(x, norm_g.reshape(DEPTH, 1, D_MODEL), mod, mod)


MM_TM = 1024
MM_TN = 512


def _mm_kernel(a_ref, w_ref, o_ref):
    o_ref[...] = jnp.dot(a_ref[...], w_ref[...], preferred_element_type=F32).astype(o_ref.dtype)


def _matmul(a, w, idx, out_dtype, name):
    m, k = a.shape
    n = w.shape[-1]
    tn = MM_TN if n % MM_TN == 0 else 256
    return pl.pallas_call(
        _mm_kernel,
        out_shape=jax.ShapeDtypeStruct((m, n), out_dtype),
        grid=(m // MM_TM, n // tn),
        in_specs=[pl.BlockSpec((MM_TM, k), lambda i, j: (i, 0)),
                  pl.BlockSpec((None, k, tn), lambda i, j: (idx, 0, j))],
        out_specs=pl.BlockSpec((MM_TM, tn), lambda i, j: (i, j)),
        compiler_params=_cparams(("parallel", "parallel")),
        name=name,
    )(a, w)


def _mm_res_kernel(a_ref, w_ref, x_ref, gate_ref, g_ref, sh_ref, sc_ref, *outs, final):
    kdim = w_ref.shape[0]
    y = jnp.dot(a_ref[:, 0:kdim], w_ref[...], preferred_element_type=F32)
    xn = x_ref[...] + gate_ref[...] * y
    if final:
        outs[0][...] = _rmsnorm(xn, g_ref[...])
    else:
        outs[0][...] = xn
        outs[1][...] = _norm_mod(xn, g_ref[...], sh_ref[...], sc_ref[...]).astype(BF16)


def _mm_res(a, w, widx, tm, x, mod, gate_layer, gate_chunk, norm_g, norm_idx, mod_layer, mod_chunk, final, name):
    m, ka = a.shape
    kdim = w.shape[1]
    row = pl.BlockSpec((tm, D_MODEL), lambda i: (i, 0))
    if final:
        gvec = pl.BlockSpec((1, D_MODEL), lambda i: (0, 0))
        g_arr = norm_g.reshape(1, D_MODEL)
        out_shape = jax.ShapeDtypeStruct((m, D_MODEL), F32)
        out_specs = row
    else:
        gvec = pl.BlockSpec((None, 1, D_MODEL), lambda i: (norm_idx, 0, 0))
        g_arr = norm_g.reshape(DEPTH, 1, D_MODEL)
        out_shape = (jax.ShapeDtypeStruct((m, D_MODEL), F32), jax.ShapeDtypeStruct((m, D_MODEL), BF16))
        out_specs = (row, row)
    return pl.pallas_call(
        functools.partial(_mm_res_kernel, final=final),
        out_shape=out_shape,
        grid=(m // tm,),
        in_specs=[pl.BlockSpec((tm, ka), lambda i: (i, 0)),
                  pl.BlockSpec((None, kdim, D_MODEL), lambda i: (widx, 0, 0), pipeline_mode=pl.Buffered(1)),
                  row,
                  _mod_spec(gate_layer, gate_chunk, tm),
                  gvec,
                  _mod_spec(mod_layer, mod_chunk, tm),
                  _mod_spec(mod_layer, mod_chunk + 1, tm)],
        out_specs=out_specs,
        compiler_params=_cparams(("parallel",)),
        name=name,
    )(a, w, x, mod, g_arr, mod, mod)


FFN_TM = 1024
FFN_TF = 512
FFN_HALO = BF16_ROWS
FFN_NT = TOKENS // FFN_TM
FFN_NF = D_FF_PAD // FFN_TF
FFN_STEPS = FFN_NT * FFN_NF
FFN_VQ = FFN_TF // LANES
FFN_EDGE = D_FF_PAD - D_FF
FFN_ROWS = 64


def _ffn_up_kernel(h_ref, halo_ref, wg_ref, wv_win_ref, cwg_ref, cwv_win_ref, cbg_ref, cbv_win_ref,
                   o_ref, hcat_ref, wv_ref, ug_a, uv_a, ug_b, uv_b):
    s = pl.program_id(0)

    @pl.when(s == 0)
    def _():
        ug_b[...] = jnp.zeros_like(ug_b)
        uv_b[...] = jnp.zeros_like(uv_b)

    @pl.when((s % FFN_NF == 0) & (s < FFN_STEPS))
    def _():
        hcat_ref[0:FFN_HALO, :] = halo_ref[...]
        hcat_ref[FFN_HALO:, :] = h_ref[...]

    cur_last = jnp.minimum(s, FFN_STEPS - 1) % FFN_NF == FFN_NF - 1

    @pl.when(cur_last)
    def _():
        wv_ref[:, 0:FFN_TF - FFN_EDGE] = wv_win_ref[:, FFN_EDGE:FFN_TF]
        wv_ref[:, FFN_TF - FFN_EDGE:FFN_TF] = wv_win_ref[:, 0:FFN_EDGE]

    @pl.when(jnp.logical_not(cur_last))
    def _():
        wv_ref[...] = wv_win_ref[...]

    jp = jnp.maximum(s - 1, 0) % FFN_NF
    prev_last = jp == FFN_NF - 1

    def step(new_g, new_v, old_g, old_v):
        pad = SUBLANES

        def conv(u_ref, rows0, cols, w, b):
            x = u_ref[pl.ds(rows0 - pad, FFN_ROWS + pad), cols]
            out = b + w[FFN_CONV - 1:FFN_CONV, :] * x[pad:, :]
            for i in range(1, FFN_CONV):
                out = out + w[FFN_CONV - 1 - i:FFN_CONV - i, :] * pltpu.roll(x, i, 0)[pad:, :]
            return out

        for q in range(FFN_VQ):
            cols = slice(q * LANES, (q + 1) * LANES)
            edge = slice((q * LANES + FFN_EDGE) % FFN_TF, (q * LANES + FFN_EDGE) % FFN_TF + LANES)
            valid = jp * FFN_TF + q * LANES < D_FF
            wg = cwg_ref[:, cols]
            bg = cbg_ref[:, cols]
            wv = jnp.where(prev_last, cwv_win_ref[:, edge], cwv_win_ref[:, cols])
            bv = jnp.where(prev_last, cbv_win_ref[:, edge], cbv_win_ref[:, cols])
            for r in range(FFN_TM // FFN_ROWS):
                rows0 = FFN_HALO + r * FFN_ROWS
                g = conv(old_g, rows0, cols, wg, bg)
                v = conv(old_v, rows0, cols, wv, bv)
                act = jnp.where(valid, jax.nn.gelu(g) * v, 0.0)
                o_ref[r * FFN_ROWS:(r + 1) * FFN_ROWS, cols] = act.astype(BF16)

        lhs = hcat_ref[...]
        new_g[...] = jnp.dot(lhs, wg_ref[...], preferred_element_type=F32)
        new_v[...] = jnp.dot(lhs, wv_ref[...], preferred_element_type=F32)

    @pl.when(s % 2 == 0)
    def _():
        step(ug_a, uv_a, ug_b, uv_b)

    @pl.when(s % 2 == 1)
    def _():
        step(ug_b, uv_b, ug_a, uv_a)


def _ffn_up(h, w_in, conv_w, conv_b, layer):
    tiles = h.reshape(FFN_NT, FFN_TM, D_MODEL)[:, FFN_TM - FFN_HALO:, :]
    prev = jnp.concatenate([jnp.zeros_like(tiles[:1]), tiles[:-1]], axis=0)
    starts_seq = (jnp.arange(FFN_NT) * FFN_TM) % SEQ == 0
    halo = jnp.where(starts_seq[:, None, None], jnp.zeros_like(prev), prev)

    def cur(s):
        return jnp.minimum(s, FFN_STEPS - 1)

    def prv(s):
        return jnp.maximum(s - 1, 0)

    def voff(t):
        return LANES * jnp.minimum(D_FF // LANES + FFN_VQ * (t % FFN_NF), (2 * D_FF - FFN_TF) // LANES)

    def win(rows, step_of):
        return pl.BlockSpec((None, pl.Element(rows), pl.Element(FFN_TF)),
                            lambda s: (layer, 0, voff(step_of(s))))

    in_specs = [pl.BlockSpec((FFN_TM, D_MODEL), lambda s: (cur(s) // FFN_NF, 0)),
                pl.BlockSpec((None, FFN_HALO, D_MODEL), lambda s: (cur(s) // FFN_NF, 0, 0)),
                pl.BlockSpec((None, D_MODEL, FFN_TF), lambda s: (layer, 0, cur(s) % FFN_NF)),
                win(D_MODEL, cur),
                pl.BlockSpec((None, FFN_CONV, FFN_TF), lambda s: (layer, 0, prv(s) % FFN_NF)),
                win(FFN_CONV, prv),
                pl.BlockSpec((None, 1, FFN_TF), lambda s: (layer, 0, prv(s) % FFN_NF)),
                win(1, prv)]
    u_buf = pltpu.VMEM((FFN_HALO + FFN_TM, FFN_TF), F32)
    args = [h, halo, w_in, w_in, conv_w, conv_w, conv_b, conv_b]
    return pl.pallas_call(
        _ffn_up_kernel,
        out_shape=jax.ShapeDtypeStruct((TOKENS, D_FF_PAD), BF16),
        grid=(FFN_STEPS + 1,),
        in_specs=in_specs,
        out_specs=pl.BlockSpec((FFN_TM, FFN_TF), lambda s: (prv(s) // FFN_NF, prv(s) % FFN_NF)),
        scratch_shapes=[pltpu.VMEM((FFN_HALO + FFN_TM, D_MODEL), BF16),
                        pltpu.VMEM((D_MODEL, FFN_TF), BF16),
                        u_buf, u_buf, u_buf, u_buf],
        compiler_params=_cparams(("arbitrary",)),
        name="ffn_up",
    )(*args)


ATT_T = 256
ATT_NBIAS = 4


def _dilated_bias_tiles():
    tiles = np.zeros((ATT_NBIAS, ATT_T, ATT_T), np.float32)
    qi = np.arange(ATT_T)[:, None]
    kj = np.arange(ATT_T)[None, :]
    for off in range(ATT_NBIAS):
        delta = off * ATT_T + qi - kj
        count = np.zeros_like(delta)
        for window, dil in A_PATTERNS:
            count += ((delta >= 0) & (delta <= window) & (delta % dil == 0)).astype(delta.dtype)
        tiles[off] = np.where(count > 0, np.log(np.maximum(count, 1)), NEG)
    return tiles


def _attn_a_kernel(q_ref, k_ref, v_ref, cos_ref, sin_ref, bias_ref, o_ref, qs_ref, ks_ref, vs_ref):
    cos = cos_ref[...]
    sin = sin_ref[...]
    half = A_HEAD_DIM // 2
    q = q_ref[...]
    k = k_ref[...]
    scale = A_HEAD_DIM ** -0.5
    qs_ref[...] = ((q * cos + pltpu.roll(q, half, 1) * sin) * scale).astype(BF16)
    ks_ref[...] = (k * cos + pltpu.roll(k, half, 1) * sin).astype(BF16)
    vs_ref[...] = v_ref[...].astype(BF16)

    for i in range(SEQ // ATT_T):
        n = (i + 1) * ATT_T
        q_blk = qs_ref[i * ATT_T:n, :]
        s = lax.dot_general(q_blk, ks_ref[0:n, :], (((1,), (1,)), ((), ())), preferred_element_type=F32)
        s = s + jnp.concatenate([bias_ref[min(i - j, ATT_NBIAS - 1)] for j in range(i + 1)], axis=1)
        m = jnp.max(s, axis=-1, keepdims=True)
        p = jnp.exp(s - m)
        l = jnp.sum(p, axis=-1, keepdims=True)
        acc = jnp.dot(p.astype(BF16), vs_ref[0:n, :], preferred_element_type=F32)
        o_ref[i * ATT_T:n, :] = (acc / l).astype(o_ref.dtype)


def _attn_a(proj, cos_a, sin_a):
    bias = jnp.asarray(_dilated_bias_tiles())
    tab = pl.BlockSpec((SEQ, LANES), lambda b, h: (b, 0))
    return pl.pallas_call(
        _attn_a_kernel,
        out_shape=jax.ShapeDtypeStruct((TOKENS, A_WIDTH), BF16),
        grid=(BATCH, A_HEADS),
        in_specs=[pl.BlockSpec((SEQ, A_HEAD_DIM), lambda b, h: (b, h)),
                  pl.BlockSpec((SEQ, A_HEAD_DIM), lambda b, h: (b, A_HEADS + h)),
                  pl.BlockSpec((SEQ, A_HEAD_DIM), lambda b, h: (b, 2 * A_HEADS + h)),
                  tab, tab,
                  pl.BlockSpec((ATT_NBIAS, ATT_T, ATT_T), lambda b, h: (0, 0, 0))],
        out_specs=pl.BlockSpec((SEQ, A_HEAD_DIM), lambda b, h: (b, h)),
        scratch_shapes=[pltpu.VMEM((SEQ, A_HEAD_DIM), BF16)] * 3,
        compiler_params=_cparams(("parallel", "parallel")),
        name="attn_dilated",
    )(proj, proj, proj, cos_a, sin_a, bias)


LRU_TS = 512
LRU_HALO = SUBLANES


def _lru_kernel(xb_ref, yb_ref, cw_ref, cb_ref, ga_ref, gab_ref, gx_ref, gxb_ref, lam_ref, o_ref,
                ext_ref, a_ref, b_ref, carry_ref):
    t = pl.program_id(1)

    @pl.when(t == 0)
    def _():
        ext_ref[0:LRU_HALO, :] = jnp.zeros((LRU_HALO, B_WIDTH), F32)
        carry_ref[...] = jnp.zeros_like(carry_ref)

    ext_ref[LRU_HALO:, :] = xb_ref[...]
    ext = ext_ref[...]
    base = LRU_HALO - (B_CONV - 1)
    xc = cb_ref[...]
    for i in range(B_CONV):
        xc = xc + cw_ref[i:i + 1, :] * ext[base + i:base + i + LRU_TS, :]
    ext_ref[0:LRU_HALO, :] = ext[LRU_TS:LRU_TS + LRU_HALO, :]

    lam = lam_ref[...]
    neg_sp = -LRU_C * (jnp.maximum(-lam, 0.0) + jnp.log1p(jnp.exp(-jnp.abs(lam))))
    width = B_WIDTH // B_BLOCKS
    for blk in range(B_BLOCKS):
        sl = slice(blk * width, (blk + 1) * width)
        xh = xc[:, sl]
        xh16 = xh.astype(BF16)
        r = jax.nn.sigmoid(jnp.dot(xh16, ga_ref[blk], preferred_element_type=F32) + gab_ref[:, sl])
        gi = jax.nn.sigmoid(jnp.dot(xh16, gx_ref[blk], preferred_element_type=F32) + gxb_ref[:, sl])
        log_a = r * neg_sp[:, sl]
        a_ref[:, sl] = jnp.exp(log_a)
        th = jnp.tanh(log_a)
        b_ref[:, sl] = jnp.sqrt(-2.0 * th / (1.0 - th)) * (gi * xh)

    row = lax.broadcasted_iota(jnp.int32, (SUBLANES, B_WIDTH), 0)

    def scan_body(g, h_prev):
        rows = pl.ds(pl.multiple_of(g * SUBLANES, SUBLANES), SUBLANES)
        a = a_ref[rows, :]
        b = b_ref[rows, :]
        for s in (1, 2, 4):
            keep = row >= s
            a_sh = jnp.where(keep, pltpu.roll(a, s, 0), 1.0)
            b_sh = jnp.where(keep, pltpu.roll(b, s, 0), 0.0)
            b = a * b_sh + b
            a = a * a_sh
        h = a * h_prev + b
        b_ref[rows, :] = h
        return jnp.broadcast_to(h[SUBLANES - 1:SUBLANES, :], (SUBLANES, B_WIDTH))

    carry_ref[...] = lax.fori_loop(0, LRU_TS // SUBLANES, scan_body, carry_ref[...])
    o_ref[...] = (b_ref[...] * jax.nn.gelu(yb_ref[...])).astype(o_ref.dtype)


def _lru(proj, conv_w, conv_b, ga_w, ga_b, gx_w, gx_b, lam, e):
    nts = SEQ // LRU_TS
    vec = pl.BlockSpec((None, 1, B_WIDTH), lambda b, t: (e, 0, 0))
    gate = pl.BlockSpec((None, B_BLOCKS, B_WIDTH // B_BLOCKS, B_WIDTH // B_BLOCKS), lambda b, t: (e, 0, 0, 0))
    r3 = lambda a: a.reshape(a.shape[0], 1, B_WIDTH)
    return pl.pallas_call(
        _lru_kernel,
        out_shape=jax.ShapeDtypeStruct((TOKENS, B_WIDTH), BF16),
        grid=(BATCH, nts),
        in_specs=[pl.BlockSpec((LRU_TS, B_WIDTH), lambda b, t: (b * nts + t, 3)),
                  pl.BlockSpec((LRU_TS, B_WIDTH), lambda b, t: (b * nts + t, 4)),
                  pl.BlockSpec((None, B_CONV, B_WIDTH), lambda b, t: (e, 0, 0)),
                  vec, gate, vec, gate, vec, vec],
        out_specs=pl.BlockSpec((LRU_TS, B_WIDTH), lambda b, t: (b * nts + t, 0)),
        scratch_shapes=[pltpu.VMEM((LRU_HALO + LRU_TS, B_WIDTH), F32),
                        pltpu.VMEM((LRU_TS, B_WIDTH), F32),
                        pltpu.VMEM((LRU_TS, B_WIDTH), F32),
                        pltpu.VMEM((SUBLANES, B_WIDTH), F32)],
        compiler_params=_cparams(("parallel", "arbitrary")),
        name="rg_lru",
    )(proj, proj, conv_w, r3(conv_b), ga_w, r3(ga_b), gx_w, r3(gx_b), r3(lam))


SWA_T = 128
SWA_PAIRS = C_GROUP // 2


def _swa_bias_tiles():
    qi = np.tile(np.arange(SWA_T), SWA_PAIRS)[:, None]
    kj = np.arange(2 * SWA_T)[None, :]
    delta = qi + SWA_T - kj
    band = (delta >= 0) & (delta <= C_WINDOW - 1)
    tiles = np.zeros((2, SWA_PAIRS * SWA_T, 2 * SWA_T), np.float32)
    tiles[0] = np.where(band & (kj >= SWA_T), 0.0, NEG)
    tiles[1] = np.where(band, 0.0, NEG)
    return tiles


def _swa_kernel(sink_ref, q_ref, kv_ref, cos_ref, sp_ref, sm_ref, bias_ref, o_ref,
                qs_ref, ka_ref, kb_ref, va_ref, vb_ref):
    kvh = pl.program_id(1)
    cos = cos_ref[...]
    s_plus = sp_ref[...]
    s_minus = sm_ref[...]
    quarter = C_HEAD_DIM // 2

    def rope(x):
        return x * cos + pltpu.roll(x, quarter, 1) * s_plus + pltpu.roll(x, LANES - quarter, 1) * s_minus

    scale = C_HEAD_DIM ** -0.5
    for j in range(SWA_PAIRS):
        sl = slice(j * LANES, (j + 1) * LANES)
        qs_ref[:, sl] = (rope(q_ref[:, sl]) * scale).astype(BF16)

    lane = lax.broadcasted_iota(jnp.int32, (SEQ, LANES), 1)
    low = lane < C_HEAD_DIM
    kk = rope(kv_ref[:, 0:LANES])
    vv = kv_ref[:, LANES:2 * LANES]
    kk = jnp.where(kvh == 0, kk, pltpu.roll(kk, C_HEAD_DIM, 1))
    vv = jnp.where(kvh == 0, vv, pltpu.roll(vv, C_HEAD_DIM, 1))
    k_lo = jnp.where(low, kk, 0.0)
    v_lo = jnp.where(low, vv, 0.0)
    zeros = jnp.zeros((SWA_T, LANES), BF16)
    for ref, val in ((ka_ref, k_lo), (kb_ref, pltpu.roll(k_lo, C_HEAD_DIM, 1)),
                     (va_ref, v_lo), (vb_ref, pltpu.roll(v_lo, C_HEAD_DIM, 1))):
        ref[0:SWA_T, :] = zeros
        ref[SWA_T:, :] = val.astype(BF16)

    rows_st = SWA_PAIRS * SWA_T
    pair = lax.broadcasted_iota(jnp.int32, (rows_st, 1), 0) // SWA_T
    sink_a = jnp.zeros((rows_st, 1), F32)
    sink_b = jnp.zeros((rows_st, 1), F32)
    for j in range(SWA_PAIRS):
        sink_a = jnp.where(pair == j, sink_ref[kvh * C_GROUP + 2 * j], sink_a)
        sink_b = jnp.where(pair == j, sink_ref[kvh * C_GROUP + 2 * j + 1], sink_b)

    def q_body(i, carry):
        r0 = pl.multiple_of(i * SWA_T, SWA_T)
        q_st = jnp.concatenate([qs_ref[pl.ds(r0, SWA_T), j * LANES:(j + 1) * LANES]
                                for j in range(SWA_PAIRS)], axis=0)
        bias = bias_ref[jnp.minimum(i, 1)]
        win = pl.ds(r0, 2 * SWA_T)
        out = jnp.zeros((rows_st, LANES), F32)
        for k_ref, v_ref, sink in ((ka_ref, va_ref, sink_a), (kb_ref, vb_ref, sink_b)):
            s = lax.dot_general(q_st, k_ref[win, :], (((1,), (1,)), ((), ())),
                                preferred_element_type=F32) + bias
            m = jnp.maximum(jnp.max(s, axis=-1, keepdims=True), sink)
            p = jnp.exp(s - m)
            den = jnp.sum(p, axis=-1, keepdims=True) + jnp.exp(sink - m)
            out = out + jnp.dot(p.astype(BF16), v_ref[win, :], preferred_element_type=F32) / den
        for j in range(SWA_PAIRS):
            o_ref[pl.ds(r0, SWA_T), j * LANES:(j + 1) * LANES] = out[j * SWA_T:(j + 1) * SWA_T, :].astype(o_ref.dtype)
        return carry

    lax.fori_loop(0, SEQ // SWA_T, q_body, 0)


def _swa(proj, sinks, cos_c, sin_cp, sin_cm, o):
    bias = jnp.asarray(_swa_bias_tiles())
    qw = C_WIDTH // C_KV_HEADS
    tab = pl.BlockSpec((SEQ, LANES), lambda b, g, *_: (b, 0))
    kv_block = C_WIDTH // (2 * C_KV_WIDTH)
    grid_spec = pltpu.PrefetchScalarGridSpec(
        num_scalar_prefetch=1,
        grid=(BATCH, C_KV_HEADS),
        in_specs=[pl.BlockSpec((SEQ, qw), lambda b, g, *_: (b, g)),
                  pl.BlockSpec((SEQ, 2 * C_KV_WIDTH), lambda b, g, *_: (b, kv_block)),
                  tab, tab, tab,
                  pl.BlockSpec((2, SWA_PAIRS * SWA_T, 2 * SWA_T), lambda b, g, *_: (0, 0, 0))],
        out_specs=pl.BlockSpec((SEQ, qw), lambda b, g, *_: (b, g)),
        scratch_shapes=[pltpu.VMEM((SEQ, qw), BF16)] + [pltpu.VMEM((SWA_T + SEQ, LANES), BF16)] * 4,
    )
    return pl.pallas_call(
        _swa_kernel,
        out_shape=jax.ShapeDtypeStruct((TOKENS, C_WIDTH), BF16),
        grid_spec=grid_spec,
        compiler_params=_cparams(("parallel", "parallel")),
        name="attn_swa",
    )(sinks[o].astype(F32), proj, proj, cos_c, sin_cp, sin_cm, bias)


S5_TS = 512
S5_GPB = LANES // D_GROUP_DIM
S5_NB = D_WIDTH // LANES
S5_SW = S5_GPB * D_STATE


def _s5_kernel(u_ref, wre_ref, wim_ref, cre_ref, cim_ref, dec_ref, d_ref, o_ref, sre_ref, sim_ref, carry_ref):
    t = pl.program_id(2)

    @pl.when(t == 0)
    def _():
        carry_ref[...] = jnp.zeros_like(carry_ref)

    u = u_ref[...]
    u16 = u.astype(BF16)
    ng = S5_TS // SUBLANES
    x_re = jnp.dot(u16, wre_ref[...], preferred_element_type=F32).reshape(ng, SUBLANES, S5_SW)
    x_im = jnp.dot(u16, wim_ref[...], preferred_element_type=F32).reshape(ng, SUBLANES, S5_SW)
    for idx, s in enumerate((1, 2, 4)):
        m_re = dec_ref[2 * idx]
        m_im = dec_ref[2 * idx + 1]
        r_re = pltpu.roll(x_re, s, 1)
        r_im = pltpu.roll(x_im, s, 1)
        x_re, x_im = x_re + (m_re * r_re - m_im * r_im), x_im + (m_re * r_im + m_im * r_re)
    sre_ref[...] = x_re.reshape(S5_TS, S5_SW)
    sim_ref[...] = x_im.reshape(S5_TS, S5_SW)
    p_re = dec_ref[6]
    p_im = dec_ref[7]

    def carry_body(g, c):
        c_re, c_im = c
        rows = pl.ds(pl.multiple_of(g * SUBLANES, SUBLANES), SUBLANES)
        s_re = sre_ref[rows, :] + (p_re * c_re - p_im * c_im)
        s_im = sim_ref[rows, :] + (p_re * c_im + p_im * c_re)
        sre_ref[rows, :] = s_re
        sim_ref[rows, :] = s_im
        last = slice(SUBLANES - 1, SUBLANES)
        return (jnp.broadcast_to(s_re[last, :], (SUBLANES, S5_SW)),
                jnp.broadcast_to(s_im[last, :], (SUBLANES, S5_SW)))

    c_re, c_im = lax.fori_loop(0, ng, carry_body, (carry_ref[0], carry_ref[1]))
    carry_ref[0] = c_re
    carry_ref[1] = c_im
    y = (jnp.dot(sre_ref[...].astype(BF16), cre_ref[...], preferred_element_type=F32)
         - jnp.dot(sim_ref[...].astype(BF16), cim_ref[...], preferred_element_type=F32)
         + d_ref[...] * u)
    o_ref[...] = jax.nn.gelu(y).astype(o_ref.dtype)


def _s5_tables(a_re, a_im, b_re, b_im, c_re, c_im, log_dt):
    dt = jnp.exp(log_dt.astype(F32))[:, None]
    lr, li = a_re.astype(F32), a_im.astype(F32)
    zr, zi = lr * dt, li * dt

    def a_pow(k):
        mag = jnp.exp(k * zr)
        return mag * jnp.cos(k * zi), mag * jnp.sin(k * zi)

    ar, ai = a_pow(1.0)
    den = lr * lr + li * li
    cr = ((ar - 1.0) * lr + ai * li) / den
    ci = (ai * lr - (ar - 1.0) * li) / den
    bb_re = cr[..., None] * b_re - ci[..., None] * b_im
    bb_im = cr[..., None] * b_im + ci[..., None] * b_re
    eye = jnp.eye(S5_GPB, dtype=F32)

    def in_blocks(bb):
        x = bb.reshape(S5_NB, S5_GPB, D_STATE, D_GROUP_DIM).transpose(0, 1, 3, 2)
        return jnp.einsum('ngcp,gh->ngchp', x, eye).reshape(S5_NB, LANES, S5_SW).astype(BF16)

    def out_blocks(cc):
        x = cc.astype(F32).reshape(S5_NB, S5_GPB, D_GROUP_DIM, D_STATE).transpose(0, 1, 3, 2)
        return jnp.einsum('ngpc,gh->ngphc', x, eye).reshape(S5_NB, S5_SW, LANES).astype(BF16)

    rows = jnp.arange(SUBLANES)[:, None]
    tabs = []
    for s in (1, 2, 4):
        pr, pi = a_pow(float(s))
        for p in (pr, pi):
            tabs.append(jnp.where(rows >= s, p.reshape(S5_NB, 1, S5_SW), 0.0))
    zr_b, zi_b = zr.reshape(S5_NB, 1, S5_SW), zi.reshape(S5_NB, 1, S5_SW)
    kk = (rows + 1).astype(F32)
    mag = jnp.exp(kk * zr_b)
    tabs.append(mag * jnp.cos(kk * zi_b))
    tabs.append(mag * jnp.sin(kk * zi_b))
    dec = jnp.stack(tabs, axis=1)
    return in_blocks(bb_re), in_blocks(bb_im), out_blocks(c_re), out_blocks(c_im), dec


def _s5(proj, tables, d_skip, o):
    wre, wim, cre, cim, dec = tables
    nts = SEQ // S5_TS
    ublock = (C_WIDTH + 2 * C_KV_WIDTH) // LANES
    win = pl.BlockSpec((None, LANES, S5_SW), lambda b, n, t: (n, 0, 0))
    wout = pl.BlockSpec((None, S5_SW, LANES), lambda b, n, t: (n, 0, 0))
    return pl.pallas_call(
        _s5_kernel,
        out_shape=jax.ShapeDtypeStruct((TOKENS, D_WIDTH), BF16),
        grid=(BATCH, S5_NB, nts),
        in_specs=[pl.BlockSpec((S5_TS, LANES), lambda b, n, t: (b * nts + t, ublock + n)),
                  win, win, wout, wout,
                  pl.BlockSpec((None, 8, SUBLANES, S5_SW), lambda b, n, t: (n, 0, 0, 0)),
                  pl.BlockSpec((None, 1, LANES), lambda b, n, t: (o, 0, n))],
        out_specs=pl.BlockSpec((S5_TS, LANES), lambda b, n, t: (b * nts + t, n)),
        scratch_shapes=[pltpu.VMEM((S5_TS, S5_SW), F32), pltpu.VMEM((S5_TS, S5_SW), F32),
                        pltpu.VMEM((2, SUBLANES, S5_SW), F32)],
        compiler_params=_cparams(("parallel", "parallel", "arbitrary")),
        name="s5_ssm",
    )(proj, wre, wim, cre, cim, dec, d_skip.reshape(-1, 1, D_WIDTH))


GLU_TN = 512


def _glu_kernel(z_ref, w_ref, b_ref, zc_ref, o_ref):
    gate = jax.nn.sigmoid(jnp.dot(z_ref[...], w_ref[...], preferred_element_type=F32) + b_ref[...])
    o_ref[...] = (zc_ref[...].astype(F32) * gate).astype(o_ref.dtype)


def _glu(z, w, b, o):
    return pl.pallas_call(
        _glu_kernel,
        out_shape=jax.ShapeDtypeStruct((TOKENS, D_WIDTH), BF16),
        grid=(TOKENS // MM_TM, D_WIDTH // GLU_TN),
        in_specs=[pl.BlockSpec((MM_TM, D_WIDTH), lambda i, j: (i, 0)),
                  pl.BlockSpec((None, D_WIDTH, GLU_TN), lambda i, j: (o, 0, j)),
                  pl.BlockSpec((None, 1, GLU_TN), lambda i, j: (o, 0, j)),
                  pl.BlockSpec((MM_TM, GLU_TN), lambda i, j: (i, j))],
        out_specs=pl.BlockSpec((MM_TM, GLU_TN), lambda i, j: (i, j)),
        compiler_params=_cparams(("parallel", "parallel")),
        name="s5_glu",
    )(z, w, b.reshape(-1, 1, D_WIDTH), z)


OUT_TM = 512
DOWN_TM = 256


def kernel(x, c, positions, ada_w, ada_b, norm_mix, norm_ffn, norm_final, ev_w_in, ev_conv_w, ev_conv_b, ev_gate_a_w, ev_gate_a_b, ev_gate_x_w, ev_gate_x_b, ev_lambda, ev_w_out, od_w_in, od_sinks, od_a_re, od_a_im, od_b_re, od_b_im, od_c_re, od_c_im, od_d, od_log_dt, od_glu_w, od_glu_b, od_w_out, ffn_w_in, ffn_conv_w, ffn_conv_b, ffn_w_out):
    ev_w_in16 = ev_w_in.astype(BF16)
    ev_w_out16 = ev_w_out.astype(BF16)
    od_w_in16 = od_w_in.astype(BF16)
    od_w_out16 = od_w_out.astype(BF16)
    od_glu16 = od_glu_w.astype(BF16)
    ga16 = ev_gate_a_w.astype(BF16)
    gx16 = ev_gate_x_w.astype(BF16)
    ffn_in16 = ffn_w_in.astype(BF16)
    ffn_out16 = ffn_w_out.astype(BF16)
    ffn_cw = ffn_conv_w.astype(F32)
    ffn_cb = ffn_conv_b.astype(F32).reshape(DEPTH, 1, 2 * D_FF)

    mod = _ada_mod(c, ada_w, ada_b)
    mod = mod.reshape(DEPTH, SUBLANES, 6, 1, D_MODEL).transpose(0, 2, 1, 3, 4)
    cos_a, sin_a, cos_c, sin_cp, sin_cm = _rope_tables(positions)

    xt = x.reshape(TOKENS, D_MODEL).astype(F32)
    h = _prenorm(xt, norm_mix, mod, 0)
    for layer in range(DEPTH):
        idx = layer // 2
        if layer % 2 == 0:
            proj = _matmul(h, ev_w_in16, idx, F32, "even_in_proj")
            attn = _attn_a(proj, cos_a, sin_a)
            lru = _lru(proj, ev_conv_w, ev_conv_b, ga16, ev_gate_a_b, gx16, ev_gate_x_b, ev_lambda, idx)
            mix = jnp.concatenate([attn, lru], axis=1)
            w_out = ev_w_out16
        else:
            proj = _matmul(h, od_w_in16, idx, F32, "odd_in_proj")
            attn = _swa(proj, od_sinks, cos_c, sin_cp, sin_cm, idx)
            tables = _s5_tables(od_a_re[idx], od_a_im[idx], od_b_re[idx], od_b_im[idx],
                                od_c_re[idx], od_c_im[idx], od_log_dt[idx])
            z = _s5(proj, tables, od_d, idx)
            ssm = _glu(z, od_glu16, od_glu_b, idx)
            mix = jnp.concatenate([attn, ssm], axis=1)
            w_out = od_w_out16
        xt, h2 = _mm_res(mix, w_out, idx, OUT_TM, xt, mod, layer, 2, norm_ffn, layer, layer, 3, False,
                         "mix_out_proj")
        act = _ffn_up(h2, ffn_in16, ffn_cw, ffn_cb, layer)
        if layer + 1 < DEPTH:
            xt, h = _mm_res(act, ffn_out16, layer, DOWN_TM, xt, mod, layer, 5, norm_mix, layer + 1, layer + 1, 0,
                            False, "ffn_down_proj")
        else:
            out = _mm_res(act, ffn_out16, layer, DOWN_TM, xt, mod, layer, 5, norm_final, 0, layer, 0, True,
                          "ffn_down_final")
    return out.reshape(BATCH, SEQ, D_MODEL).astype(x.dtype)
```

```python
import functools
import math

import jax
import jax.numpy as jnp
import numpy as np
from jax import lax
from jax.experimental import pallas as pl
from jax.experimental.pallas import tpu as pltpu

F32 = jnp.float32
BF16 = jnp.bfloat16

D_MODEL = 2048
BATCH = 4
SEQ = 2048
TOKENS = BATCH * SEQ
DEPTH = 4
ROPE_THETA = 10000.0
NORM_EPS = 1e-6
LANES = 128
SUBLANES = 8
BF16_ROWS = 16

A_HEAD_DIM = 128
A_HEADS = 8
A_WIDTH = 1024
A_PATTERNS = ((128, 1), (512, 4), (2048, 16))
B_WIDTH = 1024
B_BLOCKS = 8
B_CONV = 4
LRU_C = 8.0
EVEN_IN = 3 * A_WIDTH + 2 * B_WIDTH

C_HEAD_DIM = 64
C_HEADS = 16
C_KV_HEADS = 2
C_GROUP = 8
C_WIDTH = 1024
C_KV_WIDTH = 128
C_WINDOW = 128
D_WIDTH = 1024
D_GROUP_DIM = 16
D_GROUPS = 64
D_STATE = 64
ODD_IN = C_WIDTH + 2 * C_KV_WIDTH + D_WIDTH

D_FF = 5504
D_FF_PAD = 5632
FFN_CONV = 3

NEG = -1e30

VMEM_LIMIT = 56 * 1024 * 1024


def _cparams(sem, vmem=VMEM_LIMIT):
    return pltpu.CompilerParams(dimension_semantics=sem, vmem_limit_bytes=vmem)


GELU_C1 = 2.0 * math.sqrt(2.0 / math.pi)
GELU_C2 = 0.044715 * GELU_C1


def _gelu_tanh(x):
    z = x * (GELU_C1 + GELU_C2 * (x * x))
    return x / (1.0 + jnp.exp(-z))


ADA_TN = 1024


def _ada_kernel(c_ref, w_ref, b_ref, o_ref):
    c = c_ref[...]
    cond = (c * jax.nn.sigmoid(c)).astype(BF16)
    o_ref[...] = jnp.dot(cond, w_ref[...].astype(BF16), preferred_element_type=F32) + b_ref[...]


def _ada_mod(c, ada_w, ada_b):
    c8 = jnp.zeros((SUBLANES, D_MODEL), F32).at[:BATCH].set(c.astype(F32))
    n = 6 * D_MODEL
    return pl.pallas_call(
        _ada_kernel,
        out_shape=jax.ShapeDtypeStruct((DEPTH, SUBLANES, n), F32),
        grid=(DEPTH, n // ADA_TN),
        in_specs=[
            pl.BlockSpec((SUBLANES, D_MODEL), lambda l, j: (0, 0)),
            pl.BlockSpec((None, D_MODEL, ADA_TN), lambda l, j: (l, 0, j)),
            pl.BlockSpec((None, 1, ADA_TN), lambda l, j: (l, 0, j)),
        ],
        out_specs=pl.BlockSpec((None, SUBLANES, ADA_TN), lambda l, j: (l, 0, j)),
        compiler_params=_cparams(("parallel", "parallel")),
        name="ada_mod",
    )(c8, ada_w, ada_b.reshape(DEPTH, 1, n))


ROPE_TM = 1024


def _rope_kernel(pos_ref, inva_ref, invc_ref, ca_ref, sa_ref, cc_ref, scp_ref, scm_ref):
    pos = pos_ref[...].astype(F32)
    lane = lax.broadcasted_iota(jnp.int32, (ROPE_TM, LANES), 1)
    ang = pos * inva_ref[...]
    s = jnp.sin(ang)
    ca_ref[...] = jnp.cos(ang)
    sa_ref[...] = jnp.where(lane < A_HEAD_DIM // 2, -s, s)
    ang = pos * invc_ref[...]
    s = jnp.sin(ang)
    cc_ref[...] = jnp.cos(ang)
    second = (lane % C_HEAD_DIM) >= C_HEAD_DIM // 2
    scp_ref[...] = jnp.where(second, s, 0.0)
    scm_ref[...] = jnp.where(second, 0.0, -s)


def _rope_tables(positions):
    half_a, half_c = A_HEAD_DIM // 2, C_HEAD_DIM // 2
    inv_a = ROPE_THETA ** (-jnp.arange(half_a, dtype=F32) / half_a)
    inv_c = ROPE_THETA ** (-jnp.arange(half_c, dtype=F32) / half_c)
    inv_a = jnp.tile(inv_a, LANES // half_a).reshape(1, LANES)
    inv_c = jnp.tile(inv_c, LANES // half_c).reshape(1, LANES)
    tab = jax.ShapeDtypeStruct((TOKENS, LANES), F32)
    row = pl.BlockSpec((ROPE_TM, LANES), lambda i: (i, 0))
    vec = pl.BlockSpec((1, LANES), lambda i: (0, 0))
    return pl.pallas_call(
        _rope_kernel,
        out_shape=(tab,) * 5,
        grid=(TOKENS // ROPE_TM,),
        in_specs=[pl.BlockSpec((ROPE_TM, 1), lambda i: (i, 0)), vec, vec],
        out_specs=(row,) * 5,
        compiler_params=_cparams(("parallel",)),
        name="rope_tables",
    )(positions.reshape(TOKENS, 1), inv_a, inv_c)


def _norm_mod(x, g, sh, sc):
    ms = jnp.mean(x * x, axis=-1, keepdims=True)
    y = x * lax.rsqrt(ms + NORM_EPS) * g
    return y * (1.0 + sc) + sh


def _rmsnorm(x, g):
    ms = jnp.mean(x * x, axis=-1, keepdims=True)
    return x * lax.rsqrt(ms + NORM_EPS) * g


NORM_TM = 512


def _prenorm_kernel(x_ref, g_ref, sh_ref, sc_ref, h_ref):
    h_ref[...] = _norm_mod(x_ref[...], g_ref[...], sh_ref[...], sc_ref[...]).astype(BF16)


def _mod_spec(layer, chunk, tm):
    return pl.BlockSpec((None, None, None, 1, D_MODEL),
                        lambda i, *_: (layer, chunk, (i * tm) // SEQ, 0, 0))


def _prenorm(x, norm_g, mod, layer):
    vec = pl.BlockSpec((None, 1, D_MODEL), lambda i: (layer, 0, 0))
    return pl.pallas_call(
        _prenorm_kernel,
        out_shape=jax.ShapeDtypeStruct((TOKENS, D_MODEL), BF16),
        grid=(TOKENS // NORM_TM,),
        in_specs=[pl.BlockSpec((NORM_TM, D_MODEL), lambda i: (i, 0)), vec,
                  _mod_spec(layer, 0, NORM_TM), _mod_spec(layer, 1, NORM_TM)],
        out_specs=pl.BlockSpec((NORM_TM, D_MODEL), lambda i: (i, 0)),
        compiler_params=_cparams(("parallel",)),
        name="prenorm",
    )(x, norm_g.reshape(DEPTH, 1, D_MODEL), mod, mod)


MM_TM = 1024
MM_TN_CHOICES = (1024, 768, 512, 256)


def _mm_kernel(a_ref, w_ref, o_ref):
    o_ref[...] = jnp.dot(a_ref[...], w_ref[...], preferred_element_type=F32).astype(o_ref.dtype)


def _matmul(a, w, idx, out_dtype, name):
    m, k = a.shape
    n = w.shape[-1]
    tn = next(t for t in MM_TN_CHOICES if n % t == 0)
    return pl.pallas_call(
        _mm_kernel,
        out_shape=jax.ShapeDtypeStruct((m, n), out_dtype),
        grid=(m // MM_TM, n // tn),
        in_specs=[pl.BlockSpec((MM_TM, k), lambda i, j: (i, 0)),
                  pl.BlockSpec((None, k, tn), lambda i, j: (idx, 0, j))],
        out_specs=pl.BlockSpec((MM_TM, tn), lambda i, j: (i, j)),
        compiler_params=_cparams(("parallel", "parallel")),
        name=name,
    )(a, w)


FFN_HALO = BF16_ROWS


def _mm_res_kernel(a_ref, w_ref, x_ref, gate_ref, g_ref, sh_ref, sc_ref, *outs, final):
    kdim = w_ref.shape[0]
    y = jnp.dot(a_ref[:, 0:kdim], w_ref[...], preferred_element_type=F32)
    xn = x_ref[...] + gate_ref[...] * y
    if final:
        outs[0][...] = _rmsnorm(xn, g_ref[...])
    else:
        h = _norm_mod(xn, g_ref[...], sh_ref[...], sc_ref[...]).astype(BF16)
        outs[0][...] = xn
        outs[1][...] = h
        outs[2][...] = h[h.shape[0] - FFN_HALO:, :]


def _mm_res(a, w, widx, tm, x, mod, gate_layer, gate_chunk, norm_g, norm_idx, mod_layer, mod_chunk, final, name):
    m, ka = a.shape
    kdim = w.shape[1]
    row = pl.BlockSpec((tm, D_MODEL), lambda i: (i, 0))
    if final:
        gvec = pl.BlockSpec((1, D_MODEL), lambda i: (0, 0))
        g_arr = norm_g.reshape(1, D_MODEL)
        out_shape = jax.ShapeDtypeStruct((m, D_MODEL), F32)
        out_specs = row
    else:
        gvec = pl.BlockSpec((None, 1, D_MODEL), lambda i: (norm_idx, 0, 0))
        g_arr = norm_g.reshape(DEPTH, 1, D_MODEL)
        out_shape = (jax.ShapeDtypeStruct((m, D_MODEL), F32), jax.ShapeDtypeStruct((m, D_MODEL), BF16),
                     jax.ShapeDtypeStruct((m // tm, FFN_HALO, D_MODEL), BF16))
        out_specs = (row, row, pl.BlockSpec((None, FFN_HALO, D_MODEL), lambda i: (i, 0, 0)))
    return pl.pallas_call(
        functools.partial(_mm_res_kernel, final=final),
        out_shape=out_shape,
        grid=(m // tm,),
        in_specs=[pl.BlockSpec((tm, ka), lambda i: (i, 0)),
                  pl.BlockSpec((None, kdim, D_MODEL), lambda i: (widx, 0, 0), pipeline_mode=pl.Buffered(1)),
                  row,
                  _mod_spec(gate_layer, gate_chunk, tm),
                  gvec,
                  _mod_spec(mod_layer, mod_chunk, tm),
                  _mod_spec(mod_layer, mod_chunk + 1, tm)],
        out_specs=out_specs,
        compiler_params=_cparams(("parallel",)),
        name=name,
    )(a, w, x, mod, g_arr, mod, mod)


FFN_TM = 1024
FFN_TF = 512
FFN_SUB = 256
FFN_NT = TOKENS // FFN_TM
FFN_NF = D_FF_PAD // FFN_TF
FFN_EDGE = D_FF_PAD - D_FF
OUT_TM = 512


def _ffn_up_kernel(h_ref, halo_ref, wg_ref, wv_win_ref, cwg_ref, cwv_win_ref, cbg_ref, cbv_win_ref,
                   o_ref, hcat_ref, wv_ref, cv_ref):
    i = pl.program_id(0)
    j = pl.program_id(1)

    @pl.when(j == 0)
    def _():
        starts_seq = (i * FFN_TM) % SEQ == 0
        hcat_ref[0:FFN_HALO, :] = jnp.where(starts_seq, jnp.zeros(halo_ref.shape, BF16), halo_ref[...])
        hcat_ref[FFN_HALO:, :] = h_ref[...]

    keep = FFN_TF - FFN_EDGE

    @pl.when(j == FFN_NF - 1)
    def _():
        wv_ref[:, 0:keep] = wv_win_ref[:, FFN_EDGE:FFN_TF]
        wv_ref[:, keep:FFN_TF] = wv_win_ref[:, 0:FFN_EDGE]
        cv_ref[0:FFN_CONV, 0:keep] = cwv_win_ref[:, FFN_EDGE:FFN_TF]
        cv_ref[FFN_CONV:FFN_CONV + 1, 0:keep] = cbv_win_ref[:, FFN_EDGE:FFN_TF]
        cv_ref[:, keep:FFN_TF] = jnp.zeros((SUBLANES, FFN_EDGE), F32)

    @pl.when(j != FFN_NF - 1)
    def _():
        wv_ref[...] = wv_win_ref[...]
        cv_ref[0:FFN_CONV, :] = cwv_win_ref[...]
        cv_ref[FFN_CONV:FFN_CONV + 1, :] = cbv_win_ref[...]

    lhs = hcat_ref[...]

    def conv(u, w, b):
        out = b + w[FFN_CONV - 1:FFN_CONV, :] * u[FFN_HALO:, :]
        for k in range(1, FFN_CONV):
            out = out + w[FFN_CONV - 1 - k:FFN_CONV - k, :] * pltpu.roll(u, k, 0)[FFN_HALO:, :]
        return out

    for c in range(FFN_TF // FFN_SUB):
        sl = slice(c * FFN_SUB, (c + 1) * FFN_SUB)
        ug = jnp.dot(lhs, wg_ref[:, sl], preferred_element_type=F32)
        uv = jnp.dot(lhs, wv_ref[:, sl], preferred_element_type=F32)
        g = conv(ug, cwg_ref[:, sl], cbg_ref[:, sl])
        v = conv(uv, cv_ref[0:FFN_CONV, sl], cv_ref[FFN_CONV:FFN_CONV + 1, sl])
        o_ref[:, sl] = (_gelu_tanh(g) * v).astype(BF16)


def _ffn_up(h, tails, w_in, conv_w, conv_b, layer):
    per = FFN_TM // OUT_TM

    def voff(j):
        return LANES * jnp.minimum(D_FF // LANES + (FFN_TF // LANES) * j, (2 * D_FF - FFN_TF) // LANES)

    def win(rows):
        return pl.BlockSpec((None, pl.Element(rows), pl.Element(FFN_TF)), lambda i, j: (layer, 0, voff(j)))

    return pl.pallas_call(
        _ffn_up_kernel,
        out_shape=jax.ShapeDtypeStruct((TOKENS, D_FF_PAD), BF16),
        grid=(FFN_NT, FFN_NF),
        in_specs=[pl.BlockSpec((FFN_TM, D_MODEL), lambda i, j: (i, 0)),
                  pl.BlockSpec((None, FFN_HALO, D_MODEL), lambda i, j: (jnp.maximum(per * i - 1, 0), 0, 0)),
                  pl.BlockSpec((None, D_MODEL, FFN_TF), lambda i, j: (layer, 0, j)),
                  win(D_MODEL),
                  pl.BlockSpec((None, FFN_CONV, FFN_TF), lambda i, j: (layer, 0, j)),
                  win(FFN_CONV),
                  pl.BlockSpec((None, 1, FFN_TF), lambda i, j: (layer, 0, j)),
                  win(1)],
        out_specs=pl.BlockSpec((FFN_TM, FFN_TF), lambda i, j: (i, j)),
        scratch_shapes=[pltpu.VMEM((FFN_HALO + FFN_TM, D_MODEL), BF16),
                        pltpu.VMEM((D_MODEL, FFN_TF), BF16),
                        pltpu.VMEM((SUBLANES, FFN_TF), F32)],
        compiler_params=_cparams(("parallel", "arbitrary")),
        name="ffn_up",
    )(h, tails, w_in, w_in, conv_w, conv_w, conv_b, conv_b)


ATT_T = 256
ATT_NBIAS = 4


def _dilated_bias_tiles():
    tiles = np.zeros((ATT_NBIAS, ATT_T, ATT_T), np.float32)
    qi = np.arange(ATT_T)[:, None]
    kj = np.arange(ATT_T)[None, :]
    for off in range(ATT_NBIAS):
        delta = off * ATT_T + qi - kj
        count = np.zeros_like(delta)
        for window, dil in A_PATTERNS:
            count += ((delta >= 0) & (delta <= window) & (delta % dil == 0)).astype(delta.dtype)
        tiles[off] = np.where(count > 0, np.log(np.maximum(count, 1)), NEG)
    return tiles


def _attn_a_kernel(q_ref, k_ref, v_ref, cos_ref, sin_ref, bias_ref, o_ref, qs_ref, ks_ref, vs_ref):
    cos = cos_ref[...]
    sin = sin_ref[...]
    half = A_HEAD_DIM // 2
    q = q_ref[...]
    k = k_ref[...]
    scale = A_HEAD_DIM ** -0.5
    qs_ref[...] = ((q * cos + pltpu.roll(q, half, 1) * sin) * scale).astype(BF16)
    ks_ref[...] = (k * cos + pltpu.roll(k, half, 1) * sin).astype(BF16)
    vs_ref[...] = v_ref[...].astype(BF16)

    for i in range(SEQ // ATT_T):
        n = (i + 1) * ATT_T
        q_blk = qs_ref[i * ATT_T:n, :]
        s = lax.dot_general(q_blk, ks_ref[0:n, :], (((1,), (1,)), ((), ())), preferred_element_type=F32)
        s = s + jnp.concatenate([bias_ref[min(i - j, ATT_NBIAS - 1)] for j in range(i + 1)], axis=1)
        m = jnp.max(s, axis=-1, keepdims=True)
        p = jnp.exp(s - m)
        l = jnp.sum(p, axis=-1, keepdims=True)
        acc = jnp.dot(p.astype(BF16), vs_ref[0:n, :], preferred_element_type=F32)
        o_ref[i * ATT_T:n, :] = (acc / l).astype(o_ref.dtype)


def _attn_a(proj, cos_a, sin_a):
    bias = jnp.asarray(_dilated_bias_tiles())
    tab = pl.BlockSpec((SEQ, LANES), lambda b, h: (b, 0))
    return pl.pallas_call(
        _attn_a_kernel,
        out_shape=jax.ShapeDtypeStruct((TOKENS, A_WIDTH), BF16),
        grid=(BATCH, A_HEADS),
        in_specs=[pl.BlockSpec((SEQ, A_HEAD_DIM), lambda b, h: (b, h)),
                  pl.BlockSpec((SEQ, A_HEAD_DIM), lambda b, h: (b, A_HEADS + h)),
                  pl.BlockSpec((SEQ, A_HEAD_DIM), lambda b, h: (b, 2 * A_HEADS + h)),
                  tab, tab,
                  pl.BlockSpec((ATT_NBIAS, ATT_T, ATT_T), lambda b, h: (0, 0, 0))],
        out_specs=pl.BlockSpec((SEQ, A_HEAD_DIM), lambda b, h: (b, h)),
        scratch_shapes=[pltpu.VMEM((SEQ, A_HEAD_DIM), BF16)] * 3,
        compiler_params=_cparams(("parallel", "parallel")),
        name="attn_dilated",
    )(proj, proj, proj, cos_a, sin_a, bias)


LRU_TS = 512
LRU_HALO = SUBLANES


def _lru_kernel(xb_ref, yb_ref, cw_ref, cb_ref, ga_ref, gab_ref, gx_ref, gxb_ref, lam_ref, o_ref,
                ext_ref, a_ref, b_ref, carry_ref):
    t = pl.program_id(1)

    @pl.when(t == 0)
    def _():
        ext_ref[0:LRU_HALO, :] = jnp.zeros((LRU_HALO, B_WIDTH), F32)
        carry_ref[...] = jnp.zeros_like(carry_ref)

    ext_ref[LRU_HALO:, :] = xb_ref[...]
    ext = ext_ref[...]
    base = LRU_HALO - (B_CONV - 1)
    xc = cb_ref[...]
    for i in range(B_CONV):
        xc = xc + cw_ref[i:i + 1, :] * ext[base + i:base + i + LRU_TS, :]
    ext_ref[0:LRU_HALO, :] = ext[LRU_TS:LRU_TS + LRU_HALO, :]

    lam = lam_ref[...]
    neg_sp = -LRU_C * (jnp.maximum(-lam, 0.0) + jnp.log1p(jnp.exp(-jnp.abs(lam))))
    width = B_WIDTH // B_BLOCKS
    for blk in range(B_BLOCKS):
        sl = slice(blk * width, (blk + 1) * width)
        xh = xc[:, sl]
        xh16 = xh.astype(BF16)
        r = jax.nn.sigmoid(jnp.dot(xh16, ga_ref[blk], preferred_element_type=F32) + gab_ref[:, sl])
        gi = jax.nn.sigmoid(jnp.dot(xh16, gx_ref[blk], preferred_element_type=F32) + gxb_ref[:, sl])
        log_a = r * neg_sp[:, sl]
        a_ref[:, sl] = jnp.exp(log_a)
        th = jnp.tanh(log_a)
        b_ref[:, sl] = jnp.sqrt(-2.0 * th / (1.0 - th)) * (gi * xh)

    row = lax.broadcasted_iota(jnp.int32, (SUBLANES, B_WIDTH), 0)

    def scan_body(g, h_prev):
        rows = pl.ds(pl.multiple_of(g * SUBLANES, SUBLANES), SUBLANES)
        a = a_ref[rows, :]
        b = b_ref[rows, :]
        for s in (1, 2, 4):
            keep = row >= s
            a_sh = jnp.where(keep, pltpu.roll(a, s, 0), 1.0)
            b_sh = jnp.where(keep, pltpu.roll(b, s, 0), 0.0)
            b = a * b_sh + b
            a = a * a_sh
        h = a * h_prev + b
        b_ref[rows, :] = h
        return jnp.broadcast_to(h[SUBLANES - 1:SUBLANES, :], (SUBLANES, B_WIDTH))

    carry_ref[...] = lax.fori_loop(0, LRU_TS // SUBLANES, scan_body, carry_ref[...])
    o_ref[...] = (b_ref[...] * _gelu_tanh(yb_ref[...])).astype(o_ref.dtype)


def _lru(proj, conv_w, conv_b, ga_w, ga_b, gx_w, gx_b, lam, e):
    nts = SEQ // LRU_TS
    vec = pl.BlockSpec((None, 1, B_WIDTH), lambda b, t: (e, 0, 0))
    gate = pl.BlockSpec((None, B_BLOCKS, B_WIDTH // B_BLOCKS, B_WIDTH // B_BLOCKS), lambda b, t: (e, 0, 0, 0))
    r3 = lambda a: a.reshape(a.shape[0], 1, B_WIDTH)
    return pl.pallas_call(
        _lru_kernel,
        out_shape=jax.ShapeDtypeStruct((TOKENS, B_WIDTH), BF16),
        grid=(BATCH, nts),
        in_specs=[pl.BlockSpec((LRU_TS, B_WIDTH), lambda b, t: (b * nts + t, 3)),
                  pl.BlockSpec((LRU_TS, B_WIDTH), lambda b, t: (b * nts + t, 4)),
                  pl.BlockSpec((None, B_CONV, B_WIDTH), lambda b, t: (e, 0, 0)),
                  vec, gate, vec, gate, vec, vec],
        out_specs=pl.BlockSpec((LRU_TS, B_WIDTH), lambda b, t: (b * nts + t, 0)),
        scratch_shapes=[pltpu.VMEM((LRU_HALO + LRU_TS, B_WIDTH), F32),
                        pltpu.VMEM((LRU_TS, B_WIDTH), F32),
                        pltpu.VMEM((LRU_TS, B_WIDTH), F32),
                        pltpu.VMEM((SUBLANES, B_WIDTH), F32)],
        compiler_params=_cparams(("parallel", "arbitrary")),
        name="rg_lru",
    )(proj, proj, conv_w, r3(conv_b), ga_w, r3(ga_b), gx_w, r3(gx_b), r3(lam))


SWA_T = 128
SWA_PAIRS = C_GROUP // 2


def _swa_bias_tiles():
    qi = np.tile(np.arange(SWA_T), SWA_PAIRS)[:, None]
    kj = np.arange(2 * SWA_T)[None, :]
    delta = qi + SWA_T - kj
    band = (delta >= 0) & (delta <= C_WINDOW - 1)
    tiles = np.zeros((2, SWA_PAIRS * SWA_T, 2 * SWA_T), np.float32)
    tiles[0] = np.where(band & (kj >= SWA_T), 0.0, NEG)
    tiles[1] = np.where(band, 0.0, NEG)
    return tiles


def _swa_kernel(sink_ref, q_ref, kv_ref, cos_ref, sp_ref, sm_ref, bias_ref, o_ref,
                qs_ref, ka_ref, kb_ref, va_ref, vb_ref):
    kvh = pl.program_id(1)
    cos = cos_ref[...]
    s_plus = sp_ref[...]
    s_minus = sm_ref[...]
    quarter = C_HEAD_DIM // 2

    def rope(x):
        return x * cos + pltpu.roll(x, quarter, 1) * s_plus + pltpu.roll(x, LANES - quarter, 1) * s_minus

    scale = C_HEAD_DIM ** -0.5
    for j in range(SWA_PAIRS):
        sl = slice(j * LANES, (j + 1) * LANES)
        qs_ref[:, sl] = (rope(q_ref[:, sl]) * scale).astype(BF16)

    lane = lax.broadcasted_iota(jnp.int32, (SEQ, LANES), 1)
    low = lane < C_HEAD_DIM
    kk = rope(kv_ref[:, 0:LANES])
    vv = kv_ref[:, LANES:2 * LANES]
    kk = jnp.where(kvh == 0, kk, pltpu.roll(kk, C_HEAD_DIM, 1))
    vv = jnp.where(kvh == 0, vv, pltpu.roll(vv, C_HEAD_DIM, 1))
    k_lo = jnp.where(low, kk, 0.0)
    v_lo = jnp.where(low, vv, 0.0)
    zeros = jnp.zeros((SWA_T, LANES), BF16)
    for ref, val in ((ka_ref, k_lo), (kb_ref, pltpu.roll(k_lo, C_HEAD_DIM, 1)),
                     (va_ref, v_lo), (vb_ref, pltpu.roll(v_lo, C_HEAD_DIM, 1))):
        ref[0:SWA_T, :] = zeros
        ref[SWA_T:, :] = val.astype(BF16)

    rows_st = SWA_PAIRS * SWA_T
    pair = lax.broadcasted_iota(jnp.int32, (rows_st, 1), 0) // SWA_T
    sink_a = jnp.zeros((rows_st, 1), F32)
    sink_b = jnp.zeros((rows_st, 1), F32)
    for j in range(SWA_PAIRS):
        sink_a = jnp.where(pair == j, sink_ref[kvh * C_GROUP + 2 * j], sink_a)
        sink_b = jnp.where(pair == j, sink_ref[kvh * C_GROUP + 2 * j + 1], sink_b)

    def q_body(i, carry):
        r0 = pl.multiple_of(i * SWA_T, SWA_T)
        q_st = jnp.concatenate([qs_ref[pl.ds(r0, SWA_T), j * LANES:(j + 1) * LANES]
                                for j in range(SWA_PAIRS)], axis=0)
        bias = bias_ref[jnp.minimum(i, 1)]
        win = pl.ds(r0, 2 * SWA_T)
        out = jnp.zeros((rows_st, LANES), F32)
        for k_ref, v_ref, sink in ((ka_ref, va_ref, sink_a), (kb_ref, vb_ref, sink_b)):
            s = lax.dot_general(q_st, k_ref[win, :], (((1,), (1,)), ((), ())),
                                preferred_element_type=F32) + bias
            m = jnp.maximum(jnp.max(s, axis=-1, keepdims=True), sink)
            p = jnp.exp(s - m)
            den = jnp.sum(p, axis=-1, keepdims=True) + jnp.exp(sink - m)
            out = out + jnp.dot(p.astype(BF16), v_ref[win, :], preferred_element_type=F32) / den
        for j in range(SWA_PAIRS):
            o_ref[pl.ds(r0, SWA_T), j * LANES:(j + 1) * LANES] = out[j * SWA_T:(j + 1) * SWA_T, :].astype(o_ref.dtype)
        return carry

    lax.fori_loop(0, SEQ // SWA_T, q_body, 0)


def _swa(proj, sinks, cos_c, sin_cp, sin_cm, o):
    bias = jnp.asarray(_swa_bias_tiles())
    qw = C_WIDTH // C_KV_HEADS
    tab = pl.BlockSpec((SEQ, LANES), lambda b, g, *_: (b, 0))
    kv_block = C_WIDTH // (2 * C_KV_WIDTH)
    grid_spec = pltpu.PrefetchScalarGridSpec(
        num_scalar_prefetch=1,
        grid=(BATCH, C_KV_HEADS),
        in_specs=[pl.BlockSpec((SEQ, qw), lambda b, g, *_: (b, g)),
                  pl.BlockSpec((SEQ, 2 * C_KV_WIDTH), lambda b, g, *_: (b, kv_block)),
                  tab, tab, tab,
                  pl.BlockSpec((2, SWA_PAIRS * SWA_T, 2 * SWA_T), lambda b, g, *_: (0, 0, 0))],
        out_specs=pl.BlockSpec((SEQ, qw), lambda b, g, *_: (b, g)),
        scratch_shapes=[pltpu.VMEM((SEQ, qw), BF16)] + [pltpu.VMEM((SWA_T + SEQ, LANES), BF16)] * 4,
    )
    return pl.pallas_call(
        _swa_kernel,
        out_shape=jax.ShapeDtypeStruct((TOKENS, C_WIDTH), BF16),
        grid_spec=grid_spec,
        compiler_params=_cparams(("parallel", "parallel")),
        name="attn_swa",
    )(sinks[o].astype(F32), proj, proj, cos_c, sin_cp, sin_cm, bias)


S5_TS = 512
S5_GPB = LANES // D_GROUP_DIM
S5_NB = D_WIDTH // LANES
S5_SW = S5_GPB * D_STATE


def _s5_kernel(u_ref, wre_ref, wim_ref, cre_ref, cim_ref, dec_ref, d_ref, o_ref, sre_ref, sim_ref, carry_ref):
    t = pl.program_id(2)

    @pl.when(t == 0)
    def _():
        carry_ref[...] = jnp.zeros_like(carry_ref)

    u = u_ref[...]
    u16 = u.astype(BF16)
    ng = S5_TS // SUBLANES
    x_re = jnp.dot(u16, wre_ref[...], preferred_element_type=F32).reshape(ng, SUBLANES, S5_SW)
    x_im = jnp.dot(u16, wim_ref[...], preferred_element_type=F32).reshape(ng, SUBLANES, S5_SW)
    for idx, s in enumerate((1, 2, 4)):
        m_re = dec_ref[2 * idx]
        m_im = dec_ref[2 * idx + 1]
        r_re = pltpu.roll(x_re, s, 1)
        r_im = pltpu.roll(x_im, s, 1)
        x_re, x_im = x_re + (m_re * r_re - m_im * r_im), x_im + (m_re * r_im + m_im * r_re)
    sre_ref[...] = x_re.reshape(S5_TS, S5_SW)
    sim_ref[...] = x_im.reshape(S5_TS, S5_SW)
    p_re = dec_ref[6]
    p_im = dec_ref[7]

    def carry_body(g, c):
        c_re, c_im = c
        rows = pl.ds(pl.multiple_of(g * SUBLANES, SUBLANES), SUBLANES)
        s_re = sre_ref[rows, :] + (p_re * c_re - p_im * c_im)
        s_im = sim_ref[rows, :] + (p_re * c_im + p_im * c_re)
        sre_ref[rows, :] = s_re
        sim_ref[rows, :] = s_im
        last = slice(SUBLANES - 1, SUBLANES)
        return (jnp.broadcast_to(s_re[last, :], (SUBLANES, S5_SW)),
                jnp.broadcast_to(s_im[last, :], (SUBLANES, S5_SW)))

    c_re, c_im = lax.fori_loop(0, ng, carry_body, (carry_ref[0], carry_ref[1]))
    carry_ref[0] = c_re
    carry_ref[1] = c_im
    y = (jnp.dot(sre_ref[...].astype(BF16), cre_ref[...], preferred_element_type=F32)
         - jnp.dot(sim_ref[...].astype(BF16), cim_ref[...], preferred_element_type=F32)
         + d_ref[...] * u)
    o_ref[...] = _gelu_tanh(y).astype(o_ref.dtype)


def _s5_tables(a_re, a_im, b_re, b_im, c_re, c_im, log_dt):
    dt = jnp.exp(log_dt.astype(F32))[:, None]
    lr, li = a_re.astype(F32), a_im.astype(F32)
    zr, zi = lr * dt, li * dt

    def a_pow(k):
        mag = jnp.exp(k * zr)
        return mag * jnp.cos(k * zi), mag * jnp.sin(k * zi)

    ar, ai = a_pow(1.0)
    den = lr * lr + li * li
    cr = ((ar - 1.0) * lr + ai * li) / den
    ci = (ai * lr - (ar - 1.0) * li) / den
    bb_re = cr[..., None] * b_re - ci[..., None] * b_im
    bb_im = cr[..., None] * b_im + ci[..., None] * b_re
    eye = jnp.eye(S5_GPB, dtype=F32)

    def in_blocks(bb):
        x = bb.reshape(S5_NB, S5_GPB, D_STATE, D_GROUP_DIM).transpose(0, 1, 3, 2)
        return jnp.einsum('ngcp,gh->ngchp', x, eye).reshape(S5_NB, LANES, S5_SW).astype(BF16)

    def out_blocks(cc):
        x = cc.astype(F32).reshape(S5_NB, S5_GPB, D_GROUP_DIM, D_STATE).transpose(0, 1, 3, 2)
        return jnp.einsum('ngpc,gh->ngphc', x, eye).reshape(S5_NB, S5_SW, LANES).astype(BF16)

    rows = jnp.arange(SUBLANES)[:, None]
    tabs = []
    for s in (1, 2, 4):
        pr, pi = a_pow(float(s))
        for p in (pr, pi):
            tabs.append(jnp.where(rows >= s, p.reshape(S5_NB, 1, S5_SW), 0.0))
    zr_b, zi_b = zr.reshape(S5_NB, 1, S5_SW), zi.reshape(S5_NB, 1, S5_SW)
    kk = (rows + 1).astype(F32)
    mag = jnp.exp(kk * zr_b)
    tabs.append(mag * jnp.cos(kk * zi_b))
    tabs.append(mag * jnp.sin(kk * zi_b))
    dec = jnp.stack(tabs, axis=1)
    return in_blocks(bb_re), in_blocks(bb_im), out_blocks(c_re), out_blocks(c_im), dec


def _s5(proj, tables, d_skip, o):
    wre, wim, cre, cim, dec = tables
    nts = SEQ // S5_TS
    ublock = (C_WIDTH + 2 * C_KV_WIDTH) // LANES
    win = pl.BlockSpec((None, LANES, S5_SW), lambda b, n, t: (n, 0, 0))
    wout = pl.BlockSpec((None, S5_SW, LANES), lambda b, n, t: (n, 0, 0))
    return pl.pallas_call(
        _s5_kernel,
        out_shape=jax.ShapeDtypeStruct((TOKENS, D_WIDTH), BF16),
        grid=(BATCH, S5_NB, nts),
        in_specs=[pl.BlockSpec((S5_TS, LANES), lambda b, n, t: (b * nts + t, ublock + n)),
                  win, win, wout, wout,
                  pl.BlockSpec((None, 8, SUBLANES, S5_SW), lambda b, n, t: (n, 0, 0, 0)),
                  pl.BlockSpec((None, 1, LANES), lambda b, n, t: (o, 0, n))],
        out_specs=pl.BlockSpec((S5_TS, LANES), lambda b, n, t: (b * nts + t, n)),
        scratch_shapes=[pltpu.VMEM((S5_TS, S5_SW), F32), pltpu.VMEM((S5_TS, S5_SW), F32),
                        pltpu.VMEM((2, SUBLANES, S5_SW), F32)],
        compiler_params=_cparams(("parallel", "parallel", "arbitrary")),
        name="s5_ssm",
    )(proj, wre, wim, cre, cim, dec, d_skip.reshape(-1, 1, D_WIDTH))


GLU_TN = 512


def _glu_kernel(z_ref, w_ref, b_ref, zc_ref, o_ref):
    gate = jax.nn.sigmoid(jnp.dot(z_ref[...], w_ref[...], preferred_element_type=F32) + b_ref[...])
    o_ref[...] = (zc_ref[...].astype(F32) * gate).astype(o_ref.dtype)


def _glu(z, w, b, o):
    return pl.pallas_call(
        _glu_kernel,
        out_shape=jax.ShapeDtypeStruct((TOKENS, D_WIDTH), BF16),
        grid=(TOKENS // MM_TM, D_WIDTH // GLU_TN),
        in_specs=[pl.BlockSpec((MM_TM, D_WIDTH), lambda i, j: (i, 0)),
                  pl.BlockSpec((None, D_WIDTH, GLU_TN), lambda i, j: (o, 0, j)),
                  pl.BlockSpec((None, 1, GLU_TN), lambda i, j: (o, 0, j)),
                  pl.BlockSpec((MM_TM, GLU_TN), lambda i, j: (i, j))],
        out_specs=pl.BlockSpec((MM_TM, GLU_TN), lambda i, j: (i, j)),
        compiler_params=_cparams(("parallel", "parallel")),
        name="s5_glu",
    )(z, w, b.reshape(-1, 1, D_WIDTH), z)


DOWN_TM = 256


def kernel(x, c, positions, ada_w, ada_b, norm_mix, norm_ffn, norm_final, ev_w_in, ev_conv_w, ev_conv_b, ev_gate_a_w, ev_gate_a_b, ev_gate_x_w, ev_gate_x_b, ev_lambda, ev_w_out, od_w_in, od_sinks, od_a_re, od_a_im, od_b_re, od_b_im, od_c_re, od_c_im, od_d, od_log_dt, od_glu_w, od_glu_b, od_w_out, ffn_w_in, ffn_conv_w, ffn_conv_b, ffn_w_out):
    ev_w_in16 = ev_w_in.astype(BF16)
    ev_w_out16 = ev_w_out.astype(BF16)
    od_w_in16 = od_w_in.astype(BF16)
    od_w_out16 = od_w_out.astype(BF16)
    od_glu16 = od_glu_w.astype(BF16)
    ga16 = ev_gate_a_w.astype(BF16)
    gx16 = ev_gate_x_w.astype(BF16)
    ffn_in16 = ffn_w_in.astype(BF16)
    ffn_out16 = ffn_w_out.astype(BF16)
    ffn_cw = ffn_conv_w.astype(F32)
    ffn_cb = ffn_conv_b.astype(F32).reshape(DEPTH, 1, 2 * D_FF)

    mod = _ada_mod(c, ada_w, ada_b)
    mod = mod.reshape(DEPTH, SUBLANES, 6, 1, D_MODEL).transpose(0, 2, 1, 3, 4)
    cos_a, sin_a, cos_c, sin_cp, sin_cm = _rope_tables(positions)

    xt = x.reshape(TOKENS, D_MODEL).astype(F32)
    h = _prenorm(xt, norm_mix, mod, 0)
    for layer in range(DEPTH):
        idx = layer // 2
        if layer % 2 == 0:
            proj = _matmul(h, ev_w_in16, idx, F32, "even_in_proj")
            attn = _attn_a(proj, cos_a, sin_a)
            lru = _lru(proj, ev_conv_w, ev_conv_b, ga16, ev_gate_a_b, gx16, ev_gate_x_b, ev_lambda, idx)
            mix = jnp.concatenate([attn, lru], axis=1)
            w_out = ev_w_out16
        else:
            proj = _matmul(h, od_w_in16, idx, F32, "odd_in_proj")
            attn = _swa(proj, od_sinks, cos_c, sin_cp, sin_cm, idx)
            tables = _s5_tables(od_a_re[idx], od_a_im[idx], od_b_re[idx], od_b_im[idx],
                                od_c_re[idx], od_c_im[idx], od_log_dt[idx])
            z = _s5(proj, tables, od_d, idx)
            ssm = _glu(z, od_glu16, od_glu_b, idx)
            mix = jnp.concatenate([attn, ssm], axis=1)
            w_out = od_w_out16
        xt, h2, tails = _mm_res(mix, w_out, idx, OUT_TM, xt, mod, layer, 2, norm_ffn, layer, layer, 3, False,
                                "mix_out_proj")
        act = _ffn_up(h2, tails, ffn_in16, ffn_cw, ffn_cb, layer)
        if layer + 1 < DEPTH:
            xt, h, _ = _mm_res(act, ffn_out16, layer, DOWN_TM, xt, mod, layer, 5, norm_mix, layer + 1, layer + 1, 0,
                               False, "ffn_down_proj")
        else:
            out = _mm_res(act, ffn_out16, layer, DOWN_TM, xt, mod, layer, 5, norm_final, 0, layer, 0, True,
                          "ffn_down_final")
    return out.reshape(BATCH, SEQ, D_MODEL).astype(x.dtype)
```

```python
import functools
import math

import jax
import jax.numpy as jnp
import numpy as np
from jax import lax
from jax.experimental import pallas as pl
from jax.experimental.pallas import tpu as pltpu

F32 = jnp.float32
BF16 = jnp.bfloat16

D_MODEL = 2048
BATCH = 4
SEQ = 2048
TOKENS = BATCH * SEQ
DEPTH = 4
ROPE_THETA = 10000.0
NORM_EPS = 1e-6
LANES = 128
SUBLANES = 8
BF16_ROWS = 16

A_HEAD_DIM = 128
A_HEADS = 8
A_WIDTH = 1024
A_PATTERNS = ((128, 1), (512, 4), (2048, 16))
B_WIDTH = 1024
B_BLOCKS = 8
B_CONV = 4
LRU_C = 8.0
EVEN_IN = 3 * A_WIDTH + 2 * B_WIDTH

C_HEAD_DIM = 64
C_HEADS = 16
C_KV_HEADS = 2
C_GROUP = 8
C_WIDTH = 1024
C_KV_WIDTH = 128
C_WINDOW = 128
D_WIDTH = 1024
D_GROUP_DIM = 16
D_GROUPS = 64
D_STATE = 64
ODD_IN = C_WIDTH + 2 * C_KV_WIDTH + D_WIDTH

D_FF = 5504
D_FF_PAD = 5632
FFN_CONV = 3

NEG = -1e30

VMEM_LIMIT = 56 * 1024 * 1024


def _cparams(sem, vmem=VMEM_LIMIT):
    return pltpu.CompilerParams(dimension_semantics=sem, vmem_limit_bytes=vmem)


GELU_C1 = 2.0 * math.sqrt(2.0 / math.pi)
GELU_C2 = 0.044715 * GELU_C1


def _gelu_tanh(x):
    z = x * (GELU_C1 + GELU_C2 * (x * x))
    return x / (1.0 + jnp.exp(-z))


ADA_TN = 1024


def _ada_kernel(c_ref, w_ref, b_ref, o_ref):
    c = c_ref[...]
    cond = (c * jax.nn.sigmoid(c)).astype(BF16)
    o_ref[...] = jnp.dot(cond, w_ref[...].astype(BF16), preferred_element_type=F32) + b_ref[...]


def _ada_mod(c, ada_w, ada_b):
    c8 = jnp.zeros((SUBLANES, D_MODEL), F32).at[:BATCH].set(c.astype(F32))
    n = 6 * D_MODEL
    return pl.pallas_call(
        _ada_kernel,
        out_shape=jax.ShapeDtypeStruct((DEPTH, SUBLANES, n), F32),
        grid=(DEPTH, n // ADA_TN),
        in_specs=[
            pl.BlockSpec((SUBLANES, D_MODEL), lambda l, j: (0, 0)),
            pl.BlockSpec((None, D_MODEL, ADA_TN), lambda l, j: (l, 0, j)),
            pl.BlockSpec((None, 1, ADA_TN), lambda l, j: (l, 0, j)),
        ],
        out_specs=pl.BlockSpec((None, SUBLANES, ADA_TN), lambda l, j: (l, 0, j)),
        compiler_params=_cparams(("parallel", "parallel")),
        name="ada_mod",
    )(c8, ada_w, ada_b.reshape(DEPTH, 1, n))


ROPE_TM = 1024


def _rope_kernel(pos_ref, inva_ref, invc_ref, ca_ref, sa_ref, cc_ref, scp_ref, scm_ref):
    pos = pos_ref[...].astype(F32)
    lane = lax.broadcasted_iota(jnp.int32, (ROPE_TM, LANES), 1)
    ang = pos * inva_ref[...]
    s = jnp.sin(ang)
    ca_ref[...] = jnp.cos(ang)
    sa_ref[...] = jnp.where(lane < A_HEAD_DIM // 2, -s, s)
    ang = pos * invc_ref[...]
    s = jnp.sin(ang)
    cc_ref[...] = jnp.cos(ang)
    second = (lane % C_HEAD_DIM) >= C_HEAD_DIM // 2
    scp_ref[...] = jnp.where(second, s, 0.0)
    scm_ref[...] = jnp.where(second, 0.0, -s)


def _rope_tables(positions):
    half_a, half_c = A_HEAD_DIM // 2, C_HEAD_DIM // 2
    inv_a = ROPE_THETA ** (-jnp.arange(half_a, dtype=F32) / half_a)
    inv_c = ROPE_THETA ** (-jnp.arange(half_c, dtype=F32) / half_c)
    inv_a = jnp.tile(inv_a, LANES // half_a).reshape(1, LANES)
    inv_c = jnp.tile(inv_c, LANES // half_c).reshape(1, LANES)
    tab = jax.ShapeDtypeStruct((TOKENS, LANES), F32)
    row = pl.BlockSpec((ROPE_TM, LANES), lambda i: (i, 0))
    vec = pl.BlockSpec((1, LANES), lambda i: (0, 0))
    return pl.pallas_call(
        _rope_kernel,
        out_shape=(tab,) * 5,
        grid=(TOKENS // ROPE_TM,),
        in_specs=[pl.BlockSpec((ROPE_TM, 1), lambda i: (i, 0)), vec, vec],
        out_specs=(row,) * 5,
        compiler_params=_cparams(("parallel",)),
        name="rope_tables",
    )(positions.reshape(TOKENS, 1), inv_a, inv_c)


def _norm_mod(x, g, sh, sc):
    ms = jnp.mean(x * x, axis=-1, keepdims=True)
    y = x * lax.rsqrt(ms + NORM_EPS) * g
    return y * (1.0 + sc) + sh


def _rmsnorm(x, g):
    ms = jnp.mean(x * x, axis=-1, keepdims=True)
    return x * lax.rsqrt(ms + NORM_EPS) * g


NORM_TM = 512


def _prenorm_kernel(x_ref, g_ref, sh_ref, sc_ref, h_ref):
    h_ref[...] = _norm_mod(x_ref[...], g_ref[...], sh_ref[...], sc_ref[...]).astype(BF16)


def _mod_spec(layer, chunk, tm):
    return pl.BlockSpec((None, None, None, 1, D_MODEL),
                        lambda i, *_: (layer, chunk, (i * tm) // SEQ, 0, 0))


def _prenorm(x, norm_g, mod, layer):
    vec = pl.BlockSpec((None, 1, D_MODEL), lambda i: (layer, 0, 0))
    return pl.pallas_call(
        _prenorm_kernel,
        out_shape=jax.ShapeDtypeStruct((TOKENS, D_MODEL), BF16),
        grid=(TOKENS // NORM_TM,),
        in_specs=[pl.BlockSpec((NORM_TM, D_MODEL), lambda i: (i, 0)), vec,
                  _mod_spec(layer, 0, NORM_TM), _mod_spec(layer, 1, NORM_TM)],
        out_specs=pl.BlockSpec((NORM_TM, D_MODEL), lambda i: (i, 0)),
        compiler_params=_cparams(("parallel",)),
        name="prenorm",
    )(x, norm_g.reshape(DEPTH, 1, D_MODEL), mod, mod)


MM_TM = 1024
MM_TN_CHOICES = (1024, 768, 512, 256)


def _mm_kernel(a_ref, w_ref, o_ref):
    o_ref[...] = jnp.dot(a_ref[...], w_ref[...].astype(BF16), preferred_element_type=F32).astype(o_ref.dtype)


def _matmul(a, w, idx, out_dtype, name):
    m, k = a.shape
    n = w.shape[-1]
    tn = next(t for t in MM_TN_CHOICES if n % t == 0)
    return pl.pallas_call(
        _mm_kernel,
        out_shape=jax.ShapeDtypeStruct((m, n), out_dtype),
        grid=(m // MM_TM, n // tn),
        in_specs=[pl.BlockSpec((MM_TM, k), lambda i, j: (i, 0)),
                  pl.BlockSpec((None, k, tn), lambda i, j: (idx, 0, j))],
        out_specs=pl.BlockSpec((MM_TM, tn), lambda i, j: (i, j)),
        compiler_params=_cparams(("parallel", "parallel")),
        name=name,
    )(a, w)


FFN_HALO = BF16_ROWS


def _mm_res_kernel(*refs, n_lhs, final, cast_w):
    a_refs = refs[:n_lhs]
    w_ref, x_ref, gate_ref, g_ref, sh_ref, sc_ref = refs[n_lhs:n_lhs + 6]
    outs = refs[n_lhs + 6:]
    if cast_w:
        outs, w16_ref = outs[:-1], outs[-1]

        @pl.when(pl.program_id(0) == 0)
        def _():
            w16_ref[...] = w_ref[...].astype(BF16)

        w_ref = w16_ref
    y = None
    row0 = 0
    for a_ref in a_refs:
        kk = min(a_ref.shape[1], w_ref.shape[0] - row0)
        part = jnp.dot(a_ref[:, 0:kk], w_ref[row0:row0 + kk, :], preferred_element_type=F32)
        y = part if y is None else y + part
        row0 += kk
    xn = x_ref[...] + gate_ref[...] * y
    if final:
        outs[0][...] = _rmsnorm(xn, g_ref[...])
    else:
        h = _norm_mod(xn, g_ref[...], sh_ref[...], sc_ref[...]).astype(BF16)
        outs[0][...] = xn
        outs[1][...] = h
        outs[2][...] = h[h.shape[0] - FFN_HALO:, :]


def _mm_res(lhs, w, widx, tm, x, mod, gate_layer, gate_chunk, norm_g, norm_idx, mod_layer, mod_chunk, final, name):
    m = lhs[0].shape[0]
    kdim = w.shape[1]
    cast_w = w.dtype != BF16
    row = pl.BlockSpec((tm, D_MODEL), lambda i: (i, 0))
    if final:
        gvec = pl.BlockSpec((1, D_MODEL), lambda i: (0, 0))
        g_arr = norm_g.reshape(1, D_MODEL)
        out_shape = jax.ShapeDtypeStruct((m, D_MODEL), F32)
        out_specs = row
    else:
        gvec = pl.BlockSpec((None, 1, D_MODEL), lambda i: (norm_idx, 0, 0))
        g_arr = norm_g.reshape(DEPTH, 1, D_MODEL)
        out_shape = (jax.ShapeDtypeStruct((m, D_MODEL), F32), jax.ShapeDtypeStruct((m, D_MODEL), BF16),
                     jax.ShapeDtypeStruct((m // tm, FFN_HALO, D_MODEL), BF16))
        out_specs = (row, row, pl.BlockSpec((None, FFN_HALO, D_MODEL), lambda i: (i, 0, 0)))
    return pl.pallas_call(
        functools.partial(_mm_res_kernel, n_lhs=len(lhs), final=final, cast_w=cast_w),
        out_shape=out_shape,
        grid=(m // tm,),
        in_specs=[pl.BlockSpec((tm, a.shape[1]), lambda i: (i, 0)) for a in lhs] + [
                  pl.BlockSpec((None, kdim, D_MODEL), lambda i: (widx, 0, 0), pipeline_mode=pl.Buffered(1)),
                  row,
                  _mod_spec(gate_layer, gate_chunk, tm),
                  gvec,
                  _mod_spec(mod_layer, mod_chunk, tm),
                  _mod_spec(mod_layer, mod_chunk + 1, tm)],
        out_specs=out_specs,
        scratch_shapes=[pltpu.VMEM((kdim, D_MODEL), BF16)] if cast_w else [],
        compiler_params=_cparams(("arbitrary",)),
        name=name,
    )(*lhs, w, x, mod, g_arr, mod, mod)


FFN_TM = 1024
FFN_TF = 512
FFN_SUB = 256
FFN_NT = TOKENS // FFN_TM
FFN_NF = D_FF_PAD // FFN_TF
FFN_EDGE = D_FF_PAD - D_FF
OUT_TM = 512


def _ffn_up_kernel(h_ref, halo_ref, wg_ref, wv_win_ref, cwg_ref, cwv_win_ref, cbg_ref, cbv_win_ref,
                   o_ref, hcat_ref, wg16_ref, wv_ref, cv_ref):
    i = pl.program_id(0)
    j = pl.program_id(1)
    wg16_ref[...] = wg_ref[...].astype(BF16)

    @pl.when(j == 0)
    def _():
        starts_seq = (i * FFN_TM) % SEQ == 0
        hcat_ref[0:FFN_HALO, :] = jnp.where(starts_seq, jnp.zeros(halo_ref.shape, BF16), halo_ref[...])
        hcat_ref[FFN_HALO:, :] = h_ref[...]

    keep = FFN_TF - FFN_EDGE

    @pl.when(j == FFN_NF - 1)
    def _():
        wv_ref[:, 0:keep] = wv_win_ref[:, FFN_EDGE:FFN_TF].astype(BF16)
        wv_ref[:, keep:FFN_TF] = wv_win_ref[:, 0:FFN_EDGE].astype(BF16)
        cv_ref[0:FFN_CONV, 0:keep] = cwv_win_ref[:, FFN_EDGE:FFN_TF]
        cv_ref[FFN_CONV:FFN_CONV + 1, 0:keep] = cbv_win_ref[:, FFN_EDGE:FFN_TF]
        cv_ref[:, keep:FFN_TF] = jnp.zeros((SUBLANES, FFN_EDGE), F32)

    @pl.when(j != FFN_NF - 1)
    def _():
        wv_ref[...] = wv_win_ref[...].astype(BF16)
        cv_ref[0:FFN_CONV, :] = cwv_win_ref[...]
        cv_ref[FFN_CONV:FFN_CONV + 1, :] = cbv_win_ref[...]

    lhs = hcat_ref[...]

    def conv(u, w, b):
        out = b + w[FFN_CONV - 1:FFN_CONV, :] * u[FFN_HALO:, :]
        for k in range(1, FFN_CONV):
            out = out + w[FFN_CONV - 1 - k:FFN_CONV - k, :] * pltpu.roll(u, k, 0)[FFN_HALO:, :]
        return out

    for c in range(FFN_TF // FFN_SUB):
        sl = slice(c * FFN_SUB, (c + 1) * FFN_SUB)
        ug = jnp.dot(lhs, wg16_ref[:, sl], preferred_element_type=F32)
        uv = jnp.dot(lhs, wv_ref[:, sl], preferred_element_type=F32)
        g = conv(ug, cwg_ref[:, sl], cbg_ref[:, sl])
        v = conv(uv, cv_ref[0:FFN_CONV, sl], cv_ref[FFN_CONV:FFN_CONV + 1, sl])
        o_ref[:, sl] = (_gelu_tanh(g) * v).astype(BF16)


def _ffn_up(h, tails, w_in, conv_w, conv_b, layer):
    per = FFN_TM // OUT_TM

    def voff(j):
        return LANES * jnp.minimum(D_FF // LANES + (FFN_TF // LANES) * j, (2 * D_FF - FFN_TF) // LANES)

    def win(rows):
        return pl.BlockSpec((None, pl.Element(rows), pl.Element(FFN_TF)), lambda i, j: (layer, 0, voff(j)))

    return pl.pallas_call(
        _ffn_up_kernel,
        out_shape=jax.ShapeDtypeStruct((TOKENS, D_FF_PAD), BF16),
        grid=(FFN_NT, FFN_NF),
        in_specs=[pl.BlockSpec((FFN_TM, D_MODEL), lambda i, j: (i, 0)),
                  pl.BlockSpec((None, FFN_HALO, D_MODEL), lambda i, j: (jnp.maximum(per * i - 1, 0), 0, 0)),
                  pl.BlockSpec((None, D_MODEL, FFN_TF), lambda i, j: (layer, 0, j)),
                  win(D_MODEL),
                  pl.BlockSpec((None, FFN_CONV, FFN_TF), lambda i, j: (layer, 0, j)),
                  win(FFN_CONV),
                  pl.BlockSpec((None, 1, FFN_TF), lambda i, j: (layer, 0, j)),
                  win(1)],
        out_specs=pl.BlockSpec((FFN_TM, FFN_TF), lambda i, j: (i, j)),
        scratch_shapes=[pltpu.VMEM((FFN_HALO + FFN_TM, D_MODEL), BF16),
                        pltpu.VMEM((D_MODEL, FFN_TF), BF16),
                        pltpu.VMEM((D_MODEL, FFN_TF), BF16),
                        pltpu.VMEM((SUBLANES, FFN_TF), F32)],
        compiler_params=_cparams(("parallel", "arbitrary")),
        name="ffn_up",
    )(h, tails, w_in, w_in, conv_w, conv_w, conv_b, conv_b)


ATT_T = 256
ATT_NBIAS = 4


def _dilated_bias_tiles():
    tiles = np.zeros((ATT_NBIAS, ATT_T, ATT_T), np.float32)
    qi = np.arange(ATT_T)[:, None]
    kj = np.arange(ATT_T)[None, :]
    for off in range(ATT_NBIAS):
        delta = off * ATT_T + qi - kj
        count = np.zeros_like(delta)
        for window, dil in A_PATTERNS:
            count += ((delta >= 0) & (delta <= window) & (delta % dil == 0)).astype(delta.dtype)
        tiles[off] = np.where(count > 0, np.log(np.maximum(count, 1)), NEG)
    return tiles


def _attn_a_kernel(q_ref, k_ref, v_ref, cos_ref, sin_ref, bias_ref, o_ref, qs_ref, ks_ref, vs_ref):
    cos = cos_ref[...]
    sin = sin_ref[...]
    half = A_HEAD_DIM // 2
    q = q_ref[...].astype(F32)
    k = k_ref[...].astype(F32)
    scale = A_HEAD_DIM ** -0.5
    qs_ref[...] = ((q * cos + pltpu.roll(q, half, 1) * sin) * scale).astype(BF16)
    ks_ref[...] = (k * cos + pltpu.roll(k, half, 1) * sin).astype(BF16)
    vs_ref[...] = v_ref[...].astype(BF16)

    for i in range(SEQ // ATT_T):
        n = (i + 1) * ATT_T
        q_blk = qs_ref[i * ATT_T:n, :]
        s = lax.dot_general(q_blk, ks_ref[0:n, :], (((1,), (1,)), ((), ())), preferred_element_type=F32)
        s = s + jnp.concatenate([bias_ref[min(i - j, ATT_NBIAS - 1)] for j in range(i + 1)], axis=1)
        m = jnp.max(s, axis=-1, keepdims=True)
        p = jnp.exp(s - m)
        l = jnp.sum(p, axis=-1, keepdims=True)
        acc = jnp.dot(p.astype(BF16), vs_ref[0:n, :], preferred_element_type=F32)
        o_ref[i * ATT_T:n, :] = (acc / l).astype(o_ref.dtype)


def _attn_a(proj, cos_a, sin_a):
    bias = jnp.asarray(_dilated_bias_tiles())
    tab = pl.BlockSpec((SEQ, LANES), lambda b, h: (b, 0))
    return pl.pallas_call(
        _attn_a_kernel,
        out_shape=jax.ShapeDtypeStruct((TOKENS, A_WIDTH), BF16),
        grid=(BATCH, A_HEADS),
        in_specs=[pl.BlockSpec((SEQ, A_HEAD_DIM), lambda b, h: (b, h)),
                  pl.BlockSpec((SEQ, A_HEAD_DIM), lambda b, h: (b, A_HEADS + h)),
                  pl.BlockSpec((SEQ, A_HEAD_DIM), lambda b, h: (b, 2 * A_HEADS + h)),
                  tab, tab,
                  pl.BlockSpec((ATT_NBIAS, ATT_T, ATT_T), lambda b, h: (0, 0, 0))],
        out_specs=pl.BlockSpec((SEQ, A_HEAD_DIM), lambda b, h: (b, h)),
        scratch_shapes=[pltpu.VMEM((SEQ, A_HEAD_DIM), BF16)] * 3,
        compiler_params=_cparams(("parallel", "parallel")),
        name="attn_dilated",
    )(proj, proj, proj, cos_a, sin_a, bias)


LRU_TS = 512
LRU_HALO = SUBLANES


def _lru_kernel(xb_ref, yb_ref, cw_ref, cb_ref, ga_ref, gab_ref, gx_ref, gxb_ref, lam_ref, o_ref,
                ext_ref, a_ref, b_ref, carry_ref):
    t = pl.program_id(1)

    @pl.when(t == 0)
    def _():
        ext_ref[0:LRU_HALO, :] = jnp.zeros((LRU_HALO, B_WIDTH), F32)
        carry_ref[...] = jnp.zeros_like(carry_ref)

    ext_ref[LRU_HALO:, :] = xb_ref[...].astype(F32)
    ext = ext_ref[...]
    base = LRU_HALO - (B_CONV - 1)
    xc = cb_ref[...]
    for i in range(B_CONV):
        xc = xc + cw_ref[i:i + 1, :] * ext[base + i:base + i + LRU_TS, :]
    ext_ref[0:LRU_HALO, :] = ext[LRU_TS:LRU_TS + LRU_HALO, :]

    lam = lam_ref[...]
    neg_sp = -LRU_C * (jnp.maximum(-lam, 0.0) + jnp.log1p(jnp.exp(-jnp.abs(lam))))
    width = B_WIDTH // B_BLOCKS
    for blk in range(B_BLOCKS):
        sl = slice(blk * width, (blk + 1) * width)
        xh = xc[:, sl]
        xh16 = xh.astype(BF16)
        r = jax.nn.sigmoid(jnp.dot(xh16, ga_ref[blk].astype(BF16), preferred_element_type=F32) + gab_ref[:, sl])
        gi = jax.nn.sigmoid(jnp.dot(xh16, gx_ref[blk].astype(BF16), preferred_element_type=F32) + gxb_ref[:, sl])
        log_a = r * neg_sp[:, sl]
        a_ref[:, sl] = jnp.exp(log_a)
        th = jnp.tanh(log_a)
        b_ref[:, sl] = jnp.sqrt(-2.0 * th / (1.0 - th)) * (gi * xh)

    row = lax.broadcasted_iota(jnp.int32, (SUBLANES, B_WIDTH), 0)

    def scan_body(g, h_prev):
        rows = pl.ds(pl.multiple_of(g * SUBLANES, SUBLANES), SUBLANES)
        a = a_ref[rows, :]
        b = b_ref[rows, :]
        for s in (1, 2, 4):
            keep = row >= s
            a_sh = jnp.where(keep, pltpu.roll(a, s, 0), 1.0)
            b_sh = jnp.where(keep, pltpu.roll(b, s, 0), 0.0)
            b = a * b_sh + b
            a = a * a_sh
        h = a * h_prev + b
        b_ref[rows, :] = h
        return jnp.broadcast_to(h[SUBLANES - 1:SUBLANES, :], (SUBLANES, B_WIDTH))

    carry_ref[...] = lax.fori_loop(0, LRU_TS // SUBLANES, scan_body, carry_ref[...])
    o_ref[...] = (b_ref[...] * _gelu_tanh(yb_ref[...].astype(F32))).astype(o_ref.dtype)


def _lru(proj, conv_w, conv_b, ga_w, ga_b, gx_w, gx_b, lam, e):
    nts = SEQ // LRU_TS
    vec = pl.BlockSpec((None, 1, B_WIDTH), lambda b, t: (e, 0, 0))
    gate = pl.BlockSpec((None, B_BLOCKS, B_WIDTH // B_BLOCKS, B_WIDTH // B_BLOCKS), lambda b, t: (e, 0, 0, 0))
    r3 = lambda a: a.reshape(a.shape[0], 1, B_WIDTH)
    return pl.pallas_call(
        _lru_kernel,
        out_shape=jax.ShapeDtypeStruct((TOKENS, B_WIDTH), BF16),
        grid=(BATCH, nts),
        in_specs=[pl.BlockSpec((LRU_TS, B_WIDTH), lambda b, t: (b * nts + t, 3)),
                  pl.BlockSpec((LRU_TS, B_WIDTH), lambda b, t: (b * nts + t, 4)),
                  pl.BlockSpec((None, B_CONV, B_WIDTH), lambda b, t: (e, 0, 0)),
                  vec, gate, vec, gate, vec, vec],
        out_specs=pl.BlockSpec((LRU_TS, B_WIDTH), lambda b, t: (b * nts + t, 0)),
        scratch_shapes=[pltpu.VMEM((LRU_HALO + LRU_TS, B_WIDTH), F32),
                        pltpu.VMEM((LRU_TS, B_WIDTH), F32),
                        pltpu.VMEM((LRU_TS, B_WIDTH), F32),
                        pltpu.VMEM((SUBLANES, B_WIDTH), F32)],
        compiler_params=_cparams(("parallel", "arbitrary")),
        name="rg_lru",
    )(proj, proj, conv_w, r3(conv_b), ga_w, r3(ga_b), gx_w, r3(gx_b), r3(lam))


SWA_T = 128
SWA_PAIRS = C_GROUP // 2


def _swa_bias_tiles():
    qi = np.tile(np.arange(SWA_T), SWA_PAIRS)[:, None]
    kj = np.arange(2 * SWA_T)[None, :]
    delta = qi + SWA_T - kj
    band = (delta >= 0) & (delta <= C_WINDOW - 1)
    tiles = np.zeros((2, SWA_PAIRS * SWA_T, 2 * SWA_T), np.float32)
    tiles[0] = np.where(band & (kj >= SWA_T), 0.0, NEG)
    tiles[1] = np.where(band, 0.0, NEG)
    return tiles


def _swa_kernel(sink_ref, q_ref, kv_ref, cos_ref, sp_ref, sm_ref, bias_ref, o_ref,
                qs_ref, ka_ref, kb_ref, va_ref, vb_ref):
    kvh = pl.program_id(1)
    cos = cos_ref[...]
    s_plus = sp_ref[...]
    s_minus = sm_ref[...]
    quarter = C_HEAD_DIM // 2

    def rope(x):
        return x * cos + pltpu.roll(x, quarter, 1) * s_plus + pltpu.roll(x, LANES - quarter, 1) * s_minus

    scale = C_HEAD_DIM ** -0.5
    for j in range(SWA_PAIRS):
        sl = slice(j * LANES, (j + 1) * LANES)
        qs_ref[:, sl] = (rope(q_ref[:, sl].astype(F32)) * scale).astype(BF16)

    lane = lax.broadcasted_iota(jnp.int32, (SEQ, LANES), 1)
    low = lane < C_HEAD_DIM
    kk = rope(kv_ref[:, 0:LANES].astype(F32))
    vv = kv_ref[:, LANES:2 * LANES].astype(F32)
    kk = jnp.where(kvh == 0, kk, pltpu.roll(kk, C_HEAD_DIM, 1))
    vv = jnp.where(kvh == 0, vv, pltpu.roll(vv, C_HEAD_DIM, 1))
    k_lo = jnp.where(low, kk, 0.0)
    v_lo = jnp.where(low, vv, 0.0)
    zeros = jnp.zeros((SWA_T, LANES), BF16)
    for ref, val in ((ka_ref, k_lo), (kb_ref, pltpu.roll(k_lo, C_HEAD_DIM, 1)),
                     (va_ref, v_lo), (vb_ref, pltpu.roll(v_lo, C_HEAD_DIM, 1))):
        ref[0:SWA_T, :] = zeros
        ref[SWA_T:, :] = val.astype(BF16)

    rows_st = SWA_PAIRS * SWA_T
    pair = lax.broadcasted_iota(jnp.int32, (rows_st, 1), 0) // SWA_T
    sink_a = jnp.zeros((rows_st, 1), F32)
    sink_b = jnp.zeros((rows_st, 1), F32)
    for j in range(SWA_PAIRS):
        sink_a = jnp.where(pair == j, sink_ref[kvh * C_GROUP + 2 * j], sink_a)
        sink_b = jnp.where(pair == j, sink_ref[kvh * C_GROUP + 2 * j + 1], sink_b)

    def q_body(i, carry):
        r0 = pl.multiple_of(i * SWA_T, SWA_T)
        q_st = jnp.concatenate([qs_ref[pl.ds(r0, SWA_T), j * LANES:(j + 1) * LANES]
                                for j in range(SWA_PAIRS)], axis=0)
        bias = bias_ref[jnp.minimum(i, 1)]
        win = pl.ds(r0, 2 * SWA_T)
        out = jnp.zeros((rows_st, LANES), F32)
        for k_ref, v_ref, sink in ((ka_ref, va_ref, sink_a), (kb_ref, vb_ref, sink_b)):
            s = lax.dot_general(q_st, k_ref[win, :], (((1,), (1,)), ((), ())),
                                preferred_element_type=F32) + bias
            m = jnp.maximum(jnp.max(s, axis=-1, keepdims=True), sink)
            p = jnp.exp(s - m)
            den = jnp.sum(p, axis=-1, keepdims=True) + jnp.exp(sink - m)
            out = out + jnp.dot(p.astype(BF16), v_ref[win, :], preferred_element_type=F32) / den
        for j in range(SWA_PAIRS):
            o_ref[pl.ds(r0, SWA_T), j * LANES:(j + 1) * LANES] = out[j * SWA_T:(j + 1) * SWA_T, :].astype(o_ref.dtype)
        return carry

    lax.fori_loop(0, SEQ // SWA_T, q_body, 0)


def _swa(proj, sinks, cos_c, sin_cp, sin_cm, o):
    bias = jnp.asarray(_swa_bias_tiles())
    qw = C_WIDTH // C_KV_HEADS
    tab = pl.BlockSpec((SEQ, LANES), lambda b, g, *_: (b, 0))
    kv_block = C_WIDTH // (2 * C_KV_WIDTH)
    grid_spec = pltpu.PrefetchScalarGridSpec(
        num_scalar_prefetch=1,
        grid=(BATCH, C_KV_HEADS),
        in_specs=[pl.BlockSpec((SEQ, qw), lambda b, g, *_: (b, g)),
                  pl.BlockSpec((SEQ, 2 * C_KV_WIDTH), lambda b, g, *_: (b, kv_block)),
                  tab, tab, tab,
                  pl.BlockSpec((2, SWA_PAIRS * SWA_T, 2 * SWA_T), lambda b, g, *_: (0, 0, 0))],
        out_specs=pl.BlockSpec((SEQ, qw), lambda b, g, *_: (b, g)),
        scratch_shapes=[pltpu.VMEM((SEQ, qw), BF16)] + [pltpu.VMEM((SWA_T + SEQ, LANES), BF16)] * 4,
    )
    return pl.pallas_call(
        _swa_kernel,
        out_shape=jax.ShapeDtypeStruct((TOKENS, C_WIDTH), BF16),
        grid_spec=grid_spec,
        compiler_params=_cparams(("parallel", "parallel")),
        name="attn_swa",
    )(sinks[o].astype(F32), proj, proj, cos_c, sin_cp, sin_cm, bias)


S5_TS = 512
S5_GPB = LANES // D_GROUP_DIM
S5_NB = D_WIDTH // LANES
S5_SW = S5_GPB * D_STATE


def _s5_kernel(u_ref, wre_ref, wim_ref, cre_ref, cim_ref, dec_ref, d_ref, o_ref, sre_ref, sim_ref, carry_ref):
    t = pl.program_id(2)

    @pl.when(t == 0)
    def _():
        carry_ref[...] = jnp.zeros_like(carry_ref)

    u16 = u_ref[...]
    u = u16.astype(F32)
    ng = S5_TS // SUBLANES
    x_re = jnp.dot(u16, wre_ref[...], preferred_element_type=F32).reshape(ng, SUBLANES, S5_SW)
    x_im = jnp.dot(u16, wim_ref[...], preferred_element_type=F32).reshape(ng, SUBLANES, S5_SW)
    for idx, s in enumerate((1, 2, 4)):
        m_re = dec_ref[2 * idx]
        m_im = dec_ref[2 * idx + 1]
        r_re = pltpu.roll(x_re, s, 1)
        r_im = pltpu.roll(x_im, s, 1)
        x_re, x_im = x_re + (m_re * r_re - m_im * r_im), x_im + (m_re * r_im + m_im * r_re)
    sre_ref[...] = x_re.reshape(S5_TS, S5_SW)
    sim_ref[...] = x_im.reshape(S5_TS, S5_SW)
    p_re = dec_ref[6]
    p_im = dec_ref[7]

    def carry_body(g, c):
        c_re, c_im = c
        rows = pl.ds(pl.multiple_of(g * SUBLANES, SUBLANES), SUBLANES)
        s_re = sre_ref[rows, :] + (p_re * c_re - p_im * c_im)
        s_im = sim_ref[rows, :] + (p_re * c_im + p_im * c_re)
        sre_ref[rows, :] = s_re
        sim_ref[rows, :] = s_im
        last = slice(SUBLANES - 1, SUBLANES)
        return (jnp.broadcast_to(s_re[last, :], (SUBLANES, S5_SW)),
                jnp.broadcast_to(s_im[last, :], (SUBLANES, S5_SW)))

    c_re, c_im = lax.fori_loop(0, ng, carry_body, (carry_ref[0], carry_ref[1]))
    carry_ref[0] = c_re
    carry_ref[1] = c_im
    y = (jnp.dot(sre_ref[...].astype(BF16), cre_ref[...], preferred_element_type=F32)
         - jnp.dot(sim_ref[...].astype(BF16), cim_ref[...], preferred_element_type=F32)
         + d_ref[...] * u)
    o_ref[...] = _gelu_tanh(y).astype(o_ref.dtype)


def _s5_tables(a_re, a_im, b_re, b_im, c_re, c_im, log_dt):
    dt = jnp.exp(log_dt.astype(F32))[:, None]
    lr, li = a_re.astype(F32), a_im.astype(F32)
    zr, zi = lr * dt, li * dt

    def a_pow(k):
        mag = jnp.exp(k * zr)
        return mag * jnp.cos(k * zi), mag * jnp.sin(k * zi)

    ar, ai = a_pow(1.0)
    den = lr * lr + li * li
    cr = ((ar - 1.0) * lr + ai * li) / den
    ci = (ai * lr - (ar - 1.0) * li) / den
    bb_re = cr[..., None] * b_re - ci[..., None] * b_im
    bb_im = cr[..., None] * b_im + ci[..., None] * b_re
    eye = jnp.eye(S5_GPB, dtype=F32)

    def in_blocks(bb):
        x = bb.reshape(S5_NB, S5_GPB, D_STATE, D_GROUP_DIM).transpose(0, 1, 3, 2)
        return jnp.einsum('ngcp,gh->ngchp', x, eye).reshape(S5_NB, LANES, S5_SW).astype(BF16)

    def out_blocks(cc):
        x = cc.astype(F32).reshape(S5_NB, S5_GPB, D_GROUP_DIM, D_STATE).transpose(0, 1, 3, 2)
        return jnp.einsum('ngpc,gh->ngphc', x, eye).reshape(S5_NB, S5_SW, LANES).astype(BF16)

    rows = jnp.arange(SUBLANES)[:, None]
    tabs = []
    for s in (1, 2, 4):
        pr, pi = a_pow(float(s))
        for p in (pr, pi):
            tabs.append(jnp.where(rows >= s, p.reshape(S5_NB, 1, S5_SW), 0.0))
    zr_b, zi_b = zr.reshape(S5_NB, 1, S5_SW), zi.reshape(S5_NB, 1, S5_SW)
    kk = (rows + 1).astype(F32)
    mag = jnp.exp(kk * zr_b)
    tabs.append(mag * jnp.cos(kk * zi_b))
    tabs.append(mag * jnp.sin(kk * zi_b))
    dec = jnp.stack(tabs, axis=1)
    return in_blocks(bb_re), in_blocks(bb_im), out_blocks(c_re), out_blocks(c_im), dec


def _s5(proj, tables, d_skip, o):
    wre, wim, cre, cim, dec = tables
    nts = SEQ // S5_TS
    ublock = (C_WIDTH + 2 * C_KV_WIDTH) // LANES
    win = pl.BlockSpec((None, LANES, S5_SW), lambda b, n, t: (n, 0, 0))
    wout = pl.BlockSpec((None, S5_SW, LANES), lambda b, n, t: (n, 0, 0))
    return pl.pallas_call(
        _s5_kernel,
        out_shape=jax.ShapeDtypeStruct((TOKENS, D_WIDTH), BF16),
        grid=(BATCH, S5_NB, nts),
        in_specs=[pl.BlockSpec((S5_TS, LANES), lambda b, n, t: (b * nts + t, ublock + n)),
                  win, win, wout, wout,
                  pl.BlockSpec((None, 8, SUBLANES, S5_SW), lambda b, n, t: (n, 0, 0, 0)),
                  pl.BlockSpec((None, 1, LANES), lambda b, n, t: (o, 0, n))],
        out_specs=pl.BlockSpec((S5_TS, LANES), lambda b, n, t: (b * nts + t, n)),
        scratch_shapes=[pltpu.VMEM((S5_TS, S5_SW), F32), pltpu.VMEM((S5_TS, S5_SW), F32),
                        pltpu.VMEM((2, SUBLANES, S5_SW), F32)],
        compiler_params=_cparams(("parallel", "parallel", "arbitrary")),
        name="s5_ssm",
    )(proj, wre, wim, cre, cim, dec, d_skip.reshape(-1, 1, D_WIDTH))


GLU_TN = 512


def _glu_kernel(z_ref, w_ref, b_ref, zc_ref, o_ref):
    gate = jax.nn.sigmoid(jnp.dot(z_ref[...], w_ref[...].astype(BF16), preferred_element_type=F32) + b_ref[...])
    o_ref[...] = (zc_ref[...].astype(F32) * gate).astype(o_ref.dtype)


def _glu(z, w, b, o):
    return pl.pallas_call(
        _glu_kernel,
        out_shape=jax.ShapeDtypeStruct((TOKENS, D_WIDTH), BF16),
        grid=(TOKENS // MM_TM, D_WIDTH // GLU_TN),
        in_specs=[pl.BlockSpec((MM_TM, D_WIDTH), lambda i, j: (i, 0)),
                  pl.BlockSpec((None, D_WIDTH, GLU_TN), lambda i, j: (o, 0, j)),
                  pl.BlockSpec((None, 1, GLU_TN), lambda i, j: (o, 0, j)),
                  pl.BlockSpec((MM_TM, GLU_TN), lambda i, j: (i, j))],
        out_specs=pl.BlockSpec((MM_TM, GLU_TN), lambda i, j: (i, j)),
        compiler_params=_cparams(("parallel", "parallel")),
        name="s5_glu",
    )(z, w, b.reshape(-1, 1, D_WIDTH), z)


DOWN_TM = 256


def kernel(x, c, positions, ada_w, ada_b, norm_mix, norm_ffn, norm_final, ev_w_in, ev_conv_w, ev_conv_b, ev_gate_a_w, ev_gate_a_b, ev_gate_x_w, ev_gate_x_b, ev_lambda, ev_w_out, od_w_in, od_sinks, od_a_re, od_a_im, od_b_re, od_b_im, od_c_re, od_c_im, od_d, od_log_dt, od_glu_w, od_glu_b, od_w_out, ffn_w_in, ffn_conv_w, ffn_conv_b, ffn_w_out):
    ffn_out16 = ffn_w_out.astype(BF16)
    ffn_cw = ffn_conv_w.astype(F32)
    ffn_cb = ffn_conv_b.astype(F32).reshape(DEPTH, 1, 2 * D_FF)

    mod = _ada_mod(c, ada_w, ada_b)
    mod = mod.reshape(DEPTH, SUBLANES, 6, 1, D_MODEL).transpose(0, 2, 1, 3, 4)
    cos_a, sin_a, cos_c, sin_cp, sin_cm = _rope_tables(positions)

    xt = x.reshape(TOKENS, D_MODEL).astype(F32)
    h = _prenorm(xt, norm_mix, mod, 0)
    for layer in range(DEPTH):
        idx = layer // 2
        if layer % 2 == 0:
            proj = _matmul(h, ev_w_in, idx, BF16, "even_in_proj")
            attn = _attn_a(proj, cos_a, sin_a)
            other = _lru(proj, ev_conv_w, ev_conv_b, ev_gate_a_w, ev_gate_a_b, ev_gate_x_w, ev_gate_x_b,
                         ev_lambda, idx)
            w_out = ev_w_out
        else:
            proj = _matmul(h, od_w_in, idx, BF16, "odd_in_proj")
            attn = _swa(proj, od_sinks, cos_c, sin_cp, sin_cm, idx)
            tables = _s5_tables(od_a_re[idx], od_a_im[idx], od_b_re[idx], od_b_im[idx],
                                od_c_re[idx], od_c_im[idx], od_log_dt[idx])
            z = _s5(proj, tables, od_d, idx)
            other = _glu(z, od_glu_w, od_glu_b, idx)
            w_out = od_w_out
        xt, h2, tails = _mm_res([attn, other], w_out, idx, OUT_TM, xt, mod, layer, 2, norm_ffn, layer, layer, 3,
                                False, "mix_out_proj")
        act = _ffn_up(h2, tails, ffn_w_in, ffn_cw, ffn_cb, layer)
        if layer + 1 < DEPTH:
            xt, h, _ = _mm_res([act], ffn_out16, layer, DOWN_TM, xt, mod, layer, 5, norm_mix, layer + 1, layer + 1,
                               0, False, "ffn_down_proj")
        else:
            out = _mm_res([act], ffn_out16, layer, DOWN_TM, xt, mod, layer, 5, norm_final, 0, layer, 0, True,
                          "ffn_down_final")
    return out.reshape(BATCH, SEQ, D_MODEL).astype(x.dtype)
```

```python
import functools
import math

import jax
import jax.numpy as jnp
import numpy as np
from jax import lax
from jax.experimental import pallas as pl
from jax.experimental.pallas import tpu as pltpu

F32 = jnp.float32
BF16 = jnp.bfloat16

D_MODEL = 2048
BATCH = 4
SEQ = 2048
TOKENS = BATCH * SEQ
DEPTH = 4
ROPE_THETA = 10000.0
NORM_EPS = 1e-6
LANES = 128
SUBLANES = 8
BF16_ROWS = 16

A_HEAD_DIM = 128
A_HEADS = 8
A_WIDTH = 1024
A_PATTERNS = ((128, 1), (512, 4), (2048, 16))
B_WIDTH = 1024
B_BLOCKS = 8
B_CONV = 4
LRU_C = 8.0
EVEN_IN = 3 * A_WIDTH + 2 * B_WIDTH

C_HEAD_DIM = 64
C_HEADS = 16
C_KV_HEADS = 2
C_GROUP = 8
C_WIDTH = 1024
C_KV_WIDTH = 128
C_WINDOW = 128
D_WIDTH = 1024
D_GROUP_DIM = 16
D_GROUPS = 64
D_STATE = 64
ODD_IN = C_WIDTH + 2 * C_KV_WIDTH + D_WIDTH

D_FF = 5504
D_FF_PAD = 5632
FFN_CONV = 3

NEG = -1e30

VMEM_LIMIT = 56 * 1024 * 1024


def _cparams(sem, vmem=VMEM_LIMIT):
    return pltpu.CompilerParams(dimension_semantics=sem, vmem_limit_bytes=vmem)


GELU_C1 = 2.0 * math.sqrt(2.0 / math.pi)
GELU_C2 = 0.044715 * GELU_C1


def _gelu_tanh(x):
    z = x * (GELU_C1 + GELU_C2 * (x * x))
    return x / (1.0 + jnp.exp(-z))


ADA_TN = 1024


def _ada_kernel(c_ref, w_ref, b_ref, o_ref):
    c = c_ref[...]
    cond = (c * jax.nn.sigmoid(c)).astype(BF16)
    o_ref[...] = jnp.dot(cond, w_ref[...].astype(BF16), preferred_element_type=F32) + b_ref[...]


def _ada_mod(c, ada_w, ada_b):
    c8 = jnp.zeros((SUBLANES, D_MODEL), F32).at[:BATCH].set(c.astype(F32))
    n = 6 * D_MODEL
    return pl.pallas_call(
        _ada_kernel,
        out_shape=jax.ShapeDtypeStruct((DEPTH, SUBLANES, n), F32),
        grid=(DEPTH, n // ADA_TN),
        in_specs=[
            pl.BlockSpec((SUBLANES, D_MODEL), lambda l, j: (0, 0)),
            pl.BlockSpec((None, D_MODEL, ADA_TN), lambda l, j: (l, 0, j)),
            pl.BlockSpec((None, 1, ADA_TN), lambda l, j: (l, 0, j)),
        ],
        out_specs=pl.BlockSpec((None, SUBLANES, ADA_TN), lambda l, j: (l, 0, j)),
        compiler_params=_cparams(("parallel", "parallel")),
        name="ada_mod",
    )(c8, ada_w, ada_b.reshape(DEPTH, 1, n))


ROPE_TM = 1024


def _rope_kernel(pos_ref, inva_ref, invc_ref, ca_ref, sa_ref, cc_ref, scp_ref, scm_ref):
    pos = pos_ref[...].astype(F32)
    lane = lax.broadcasted_iota(jnp.int32, (ROPE_TM, LANES), 1)
    ang = pos * inva_ref[...]
    s = jnp.sin(ang)
    ca_ref[...] = jnp.cos(ang)
    sa_ref[...] = jnp.where(lane < A_HEAD_DIM // 2, -s, s)
    ang = pos * invc_ref[...]
    s = jnp.sin(ang)
    cc_ref[...] = jnp.cos(ang)
    second = (lane % C_HEAD_DIM) >= C_HEAD_DIM // 2
    scp_ref[...] = jnp.where(second, s, 0.0)
    scm_ref[...] = jnp.where(second, 0.0, -s)


def _rope_tables(positions):
    half_a, half_c = A_HEAD_DIM // 2, C_HEAD_DIM // 2
    inv_a = ROPE_THETA ** (-jnp.arange(half_a, dtype=F32) / half_a)
    inv_c = ROPE_THETA ** (-jnp.arange(half_c, dtype=F32) / half_c)
    inv_a = jnp.tile(inv_a, LANES // half_a).reshape(1, LANES)
    inv_c = jnp.tile(inv_c, LANES // half_c).reshape(1, LANES)
    tab = jax.ShapeDtypeStruct((TOKENS, LANES), F32)
    row = pl.BlockSpec((ROPE_TM, LANES), lambda i: (i, 0))
    vec = pl.BlockSpec((1, LANES), lambda i: (0, 0))
    return pl.pallas_call(
        _rope_kernel,
        out_shape=(tab,) * 5,
        grid=(TOKENS // ROPE_TM,),
        in_specs=[pl.BlockSpec((ROPE_TM, 1), lambda i: (i, 0)), vec, vec],
        out_specs=(row,) * 5,
        compiler_params=_cparams(("parallel",)),
        name="rope_tables",
    )(positions.reshape(TOKENS, 1), inv_a, inv_c)


def _norm_mod(x, g, sh, sc):
    ms = jnp.mean(x * x, axis=-1, keepdims=True)
    y = x * lax.rsqrt(ms + NORM_EPS) * g
    return y * (1.0 + sc) + sh


def _rmsnorm(x, g):
    ms = jnp.mean(x * x, axis=-1, keepdims=True)
    return x * lax.rsqrt(ms + NORM_EPS) * g


NORM_TM = 512


def _prenorm_kernel(x_ref, g_ref, sh_ref, sc_ref, h_ref):
    h_ref[...] = _norm_mod(x_ref[...], g_ref[...], sh_ref[...], sc_ref[...]).astype(BF16)


def _mod_spec(layer, chunk, tm):
    return pl.BlockSpec((None, None, None, 1, D_MODEL),
                        lambda i, *_: (layer, chunk, (i * tm) // SEQ, 0, 0))


def _prenorm(x, norm_g, mod, layer):
    vec = pl.BlockSpec((None, 1, D_MODEL), lambda i: (layer, 0, 0))
    return pl.pallas_call(
        _prenorm_kernel,
        out_shape=jax.ShapeDtypeStruct((TOKENS, D_MODEL), BF16),
        grid=(TOKENS // NORM_TM,),
        in_specs=[pl.BlockSpec((NORM_TM, D_MODEL), lambda i: (i, 0)), vec,
                  _mod_spec(layer, 0, NORM_TM), _mod_spec(layer, 1, NORM_TM)],
        out_specs=pl.BlockSpec((NORM_TM, D_MODEL), lambda i: (i, 0)),
        compiler_params=_cparams(("parallel",)),
        name="prenorm",
    )(x, norm_g.reshape(DEPTH, 1, D_MODEL), mod, mod)


MM_TM = 1024
MM_TN_CHOICES = (1024, 768, 512, 256)


def _mm_kernel(a_ref, w_ref, o_ref):
    o_ref[...] = jnp.dot(a_ref[...], w_ref[...].astype(BF16), preferred_element_type=F32).astype(o_ref.dtype)


def _matmul(a, w, idx, out_dtype, name):
    m, k = a.shape
    n = w.shape[-1]
    tn = next(t for t in MM_TN_CHOICES if n % t == 0)
    return pl.pallas_call(
        _mm_kernel,
        out_shape=jax.ShapeDtypeStruct((m, n), out_dtype),
        grid=(m // MM_TM, n // tn),
        in_specs=[pl.BlockSpec((MM_TM, k), lambda i, j: (i, 0)),
                  pl.BlockSpec((None, k, tn), lambda i, j: (idx, 0, j))],
        out_specs=pl.BlockSpec((MM_TM, tn), lambda i, j: (i, j)),
        compiler_params=_cparams(("parallel", "parallel")),
        name=name,
    )(a, w)


FFN_HALO = BF16_ROWS


def _mm_res_kernel(*refs, n_lhs, final, cast_w):
    a_refs = refs[:n_lhs]
    w_ref, x_ref, gate_ref, g_ref, sh_ref, sc_ref = refs[n_lhs:n_lhs + 6]
    outs = refs[n_lhs + 6:]
    if cast_w:
        outs, w16_ref = outs[:-1], outs[-1]

        @pl.when(pl.program_id(0) == 0)
        def _():
            w16_ref[...] = w_ref[...].astype(BF16)

        w_ref = w16_ref
    y = None
    row0 = 0
    for a_ref in a_refs:
        kk = min(a_ref.shape[1], w_ref.shape[0] - row0)
        part = jnp.dot(a_ref[:, 0:kk], w_ref[row0:row0 + kk, :], preferred_element_type=F32)
        y = part if y is None else y + part
        row0 += kk
    xn = x_ref[...] + gate_ref[...] * y
    if final:
        outs[0][...] = _rmsnorm(xn, g_ref[...])
    else:
        h = _norm_mod(xn, g_ref[...], sh_ref[...], sc_ref[...]).astype(BF16)
        outs[0][...] = xn
        outs[1][...] = h
        outs[2][...] = h[h.shape[0] - FFN_HALO:, :]


def _mm_res(lhs, w, widx, tm, x, mod, gate_layer, gate_chunk, norm_g, norm_idx, mod_layer, mod_chunk, final, name):
    m = lhs[0].shape[0]
    kdim = w.shape[1]
    cast_w = w.dtype != BF16
    row = pl.BlockSpec((tm, D_MODEL), lambda i: (i, 0))
    if final:
        gvec = pl.BlockSpec((1, D_MODEL), lambda i: (0, 0))
        g_arr = norm_g.reshape(1, D_MODEL)
        out_shape = jax.ShapeDtypeStruct((m, D_MODEL), F32)
        out_specs = row
    else:
        gvec = pl.BlockSpec((None, 1, D_MODEL), lambda i: (norm_idx, 0, 0))
        g_arr = norm_g.reshape(DEPTH, 1, D_MODEL)
        out_shape = (jax.ShapeDtypeStruct((m, D_MODEL), F32), jax.ShapeDtypeStruct((m, D_MODEL), BF16),
                     jax.ShapeDtypeStruct((m // tm, FFN_HALO, D_MODEL), BF16))
        out_specs = (row, row, pl.BlockSpec((None, FFN_HALO, D_MODEL), lambda i: (i, 0, 0)))
    return pl.pallas_call(
        functools.partial(_mm_res_kernel, n_lhs=len(lhs), final=final, cast_w=cast_w),
        out_shape=out_shape,
        grid=(m // tm,),
        in_specs=[pl.BlockSpec((tm, a.shape[1]), lambda i: (i, 0)) for a in lhs] + [
                  pl.BlockSpec((None, kdim, D_MODEL), lambda i: (widx, 0, 0), pipeline_mode=pl.Buffered(1)),
                  row,
                  _mod_spec(gate_layer, gate_chunk, tm),
                  gvec,
                  _mod_spec(mod_layer, mod_chunk, tm),
                  _mod_spec(mod_layer, mod_chunk + 1, tm)],
        out_specs=out_specs,
        scratch_shapes=[pltpu.VMEM((kdim, D_MODEL), BF16)] if cast_w else [],
        compiler_params=_cparams(("arbitrary",)),
        name=name,
    )(*lhs, w, x, mod, g_arr, mod, mod)


FFN_TM = 1024
FFN_TF = 512
FFN_SUB = 256
FFN_NT = TOKENS // FFN_TM
FFN_NF = D_FF_PAD // FFN_TF
FFN_EDGE = D_FF_PAD - D_FF
OUT_TM = 512


def _ffn_up_kernel(h_ref, halo_ref, wg_ref, wv_win_ref, cwg_ref, cwv_win_ref, cbg_ref, cbv_win_ref,
                   o_ref, hcat_ref, wg16_ref, wv_ref, cv_ref):
    i = pl.program_id(0)
    j = pl.program_id(1)
    wg16_ref[...] = wg_ref[...].astype(BF16)

    @pl.when(j == 0)
    def _():
        starts_seq = (i * FFN_TM) % SEQ == 0
        hcat_ref[0:FFN_HALO, :] = jnp.where(starts_seq, jnp.zeros(halo_ref.shape, BF16), halo_ref[...])
        hcat_ref[FFN_HALO:, :] = h_ref[...]

    keep = FFN_TF - FFN_EDGE

    @pl.when(j == FFN_NF - 1)
    def _():
        wv_ref[:, 0:keep] = wv_win_ref[:, FFN_EDGE:FFN_TF].astype(BF16)
        wv_ref[:, keep:FFN_TF] = wv_win_ref[:, 0:FFN_EDGE].astype(BF16)
        cv_ref[0:FFN_CONV, 0:keep] = cwv_win_ref[:, FFN_EDGE:FFN_TF]
        cv_ref[FFN_CONV:FFN_CONV + 1, 0:keep] = cbv_win_ref[:, FFN_EDGE:FFN_TF]
        cv_ref[:, keep:FFN_TF] = jnp.zeros((SUBLANES, FFN_EDGE), F32)

    @pl.when(j != FFN_NF - 1)
    def _():
        wv_ref[...] = wv_win_ref[...].astype(BF16)
        cv_ref[0:FFN_CONV, :] = cwv_win_ref[...]
        cv_ref[FFN_CONV:FFN_CONV + 1, :] = cbv_win_ref[...]

    lhs = hcat_ref[...]

    def conv(u, w, b):
        out = b + w[FFN_CONV - 1:FFN_CONV, :] * u[FFN_HALO:, :]
        for k in range(1, FFN_CONV):
            out = out + w[FFN_CONV - 1 - k:FFN_CONV - k, :] * pltpu.roll(u, k, 0)[FFN_HALO:, :]
        return out

    for c in range(FFN_TF // FFN_SUB):
        sl = slice(c * FFN_SUB, (c + 1) * FFN_SUB)
        ug = jnp.dot(lhs, wg16_ref[:, sl], preferred_element_type=F32)
        uv = jnp.dot(lhs, wv_ref[:, sl], preferred_element_type=F32)
        g = conv(ug, cwg_ref[:, sl], cbg_ref[:, sl])
        v = conv(uv, cv_ref[0:FFN_CONV, sl], cv_ref[FFN_CONV:FFN_CONV + 1, sl])
        o_ref[:, sl] = (_gelu_tanh(g) * v).astype(BF16)


def _ffn_up(h, tails, w_in, conv_w, conv_b, layer):
    per = FFN_TM // OUT_TM

    def voff(j):
        return LANES * jnp.minimum(D_FF // LANES + (FFN_TF // LANES) * j, (2 * D_FF - FFN_TF) // LANES)

    def win(rows):
        return pl.BlockSpec((None, pl.Element(rows), pl.Element(FFN_TF)), lambda i, j: (layer, 0, voff(j)))

    return pl.pallas_call(
        _ffn_up_kernel,
        out_shape=jax.ShapeDtypeStruct((TOKENS, D_FF_PAD), BF16),
        grid=(FFN_NT, FFN_NF),
        in_specs=[pl.BlockSpec((FFN_TM, D_MODEL), lambda i, j: (i, 0)),
                  pl.BlockSpec((None, FFN_HALO, D_MODEL), lambda i, j: (jnp.maximum(per * i - 1, 0), 0, 0)),
                  pl.BlockSpec((None, D_MODEL, FFN_TF), lambda i, j: (layer, 0, j)),
                  win(D_MODEL),
                  pl.BlockSpec((None, FFN_CONV, FFN_TF), lambda i, j: (layer, 0, j)),
                  win(FFN_CONV),
                  pl.BlockSpec((None, 1, FFN_TF), lambda i, j: (layer, 0, j)),
                  win(1)],
        out_specs=pl.BlockSpec((FFN_TM, FFN_TF), lambda i, j: (i, j)),
        scratch_shapes=[pltpu.VMEM((FFN_HALO + FFN_TM, D_MODEL), BF16),
                        pltpu.VMEM((D_MODEL, FFN_TF), BF16),
                        pltpu.VMEM((D_MODEL, FFN_TF), BF16),
                        pltpu.VMEM((SUBLANES, FFN_TF), F32)],
        compiler_params=_cparams(("parallel", "arbitrary")),
        name="ffn_up",
    )(h, tails, w_in, w_in, conv_w, conv_w, conv_b, conv_b)


ATT_T = 256
ATT_NBIAS = 4


def _dilated_bias_tiles():
    tiles = np.zeros((ATT_NBIAS, ATT_T, ATT_T), np.float32)
    qi = np.arange(ATT_T)[:, None]
    kj = np.arange(ATT_T)[None, :]
    for off in range(ATT_NBIAS):
        delta = off * ATT_T + qi - kj
        count = np.zeros_like(delta)
        for window, dil in A_PATTERNS:
            count += ((delta >= 0) & (delta <= window) & (delta % dil == 0)).astype(delta.dtype)
        tiles[off] = np.where(count > 0, np.log(np.maximum(count, 1)), NEG)
    return tiles


def _attn_a_kernel(q_ref, k_ref, v_ref, cos_ref, sin_ref, bias_ref, o_ref, qs_ref, ks_ref, vs_ref):
    cos = cos_ref[...]
    sin = sin_ref[...]
    half = A_HEAD_DIM // 2
    q = q_ref[...].astype(F32)
    k = k_ref[...].astype(F32)
    scale = A_HEAD_DIM ** -0.5
    qs_ref[...] = ((q * cos + pltpu.roll(q, half, 1) * sin) * scale).astype(BF16)
    ks_ref[...] = (k * cos + pltpu.roll(k, half, 1) * sin).astype(BF16)
    vs_ref[...] = v_ref[...].astype(BF16)

    for i in range(SEQ // ATT_T):
        n = (i + 1) * ATT_T
        q_blk = qs_ref[i * ATT_T:n, :]
        s = lax.dot_general(q_blk, ks_ref[0:n, :], (((1,), (1,)), ((), ())), preferred_element_type=F32)
        s = s + jnp.concatenate([bias_ref[min(i - j, ATT_NBIAS - 1)] for j in range(i + 1)], axis=1)
        m = jnp.max(s, axis=-1, keepdims=True)
        p = jnp.exp(s - m)
        l = jnp.sum(p, axis=-1, keepdims=True)
        acc = jnp.dot(p.astype(BF16), vs_ref[0:n, :], preferred_element_type=F32)
        o_ref[i * ATT_T:n, :] = (acc / l).astype(o_ref.dtype)


def _attn_a(proj, cos_a, sin_a):
    bias = jnp.asarray(_dilated_bias_tiles())
    tab = pl.BlockSpec((SEQ, LANES), lambda b, h: (b, 0))
    return pl.pallas_call(
        _attn_a_kernel,
        out_shape=jax.ShapeDtypeStruct((TOKENS, A_WIDTH), BF16),
        grid=(BATCH, A_HEADS),
        in_specs=[pl.BlockSpec((SEQ, A_HEAD_DIM), lambda b, h: (b, h)),
                  pl.BlockSpec((SEQ, A_HEAD_DIM), lambda b, h: (b, A_HEADS + h)),
                  pl.BlockSpec((SEQ, A_HEAD_DIM), lambda b, h: (b, 2 * A_HEADS + h)),
                  tab, tab,
                  pl.BlockSpec((ATT_NBIAS, ATT_T, ATT_T), lambda b, h: (0, 0, 0))],
        out_specs=pl.BlockSpec((SEQ, A_HEAD_DIM), lambda b, h: (b, h)),
        scratch_shapes=[pltpu.VMEM((SEQ, A_HEAD_DIM), BF16)] * 3,
        compiler_params=_cparams(("parallel", "parallel")),
        name="attn_dilated",
    )(proj, proj, proj, cos_a, sin_a, bias)


LRU_TS = 512
LRU_HALO = SUBLANES


def _lru_kernel(xb_ref, yb_ref, cw_ref, cb_ref, ga_ref, gab_ref, gx_ref, gxb_ref, lam_ref, o_ref,
                ext_ref, a_ref, b_ref, carry_ref):
    t = pl.program_id(1)

    @pl.when(t == 0)
    def _():
        ext_ref[0:LRU_HALO, :] = jnp.zeros((LRU_HALO, B_WIDTH), F32)
        carry_ref[...] = jnp.zeros_like(carry_ref)

    ext_ref[LRU_HALO:, :] = xb_ref[...].astype(F32)
    ext = ext_ref[...]
    base = LRU_HALO - (B_CONV - 1)
    xc = cb_ref[...]
    for i in range(B_CONV):
        xc = xc + cw_ref[i:i + 1, :] * ext[base + i:base + i + LRU_TS, :]
    ext_ref[0:LRU_HALO, :] = ext[LRU_TS:LRU_TS + LRU_HALO, :]

    lam = lam_ref[...]
    neg_sp = -LRU_C * (jnp.maximum(-lam, 0.0) + jnp.log1p(jnp.exp(-jnp.abs(lam))))
    width = B_WIDTH // B_BLOCKS
    for blk in range(B_BLOCKS):
        sl = slice(blk * width, (blk + 1) * width)
        xh = xc[:, sl]
        xh16 = xh.astype(BF16)
        r = jax.nn.sigmoid(jnp.dot(xh16, ga_ref[blk].astype(BF16), preferred_element_type=F32) + gab_ref[:, sl])
        gi = jax.nn.sigmoid(jnp.dot(xh16, gx_ref[blk].astype(BF16), preferred_element_type=F32) + gxb_ref[:, sl])
        log_a = r * neg_sp[:, sl]
        a_ref[:, sl] = jnp.exp(log_a)
        th = jnp.tanh(log_a)
        b_ref[:, sl] = jnp.sqrt(-2.0 * th / (1.0 - th)) * (gi * xh)

    row = lax.broadcasted_iota(jnp.int32, (SUBLANES, B_WIDTH), 0)

    def scan_body(g, h_prev):
        rows = pl.ds(pl.multiple_of(g * SUBLANES, SUBLANES), SUBLANES)
        a = a_ref[rows, :]
        b = b_ref[rows, :]
        for s in (1, 2, 4):
            keep = row >= s
            a_sh = jnp.where(keep, pltpu.roll(a, s, 0), 1.0)
            b_sh = jnp.where(keep, pltpu.roll(b, s, 0), 0.0)
            b = a * b_sh + b
            a = a * a_sh
        h = a * h_prev + b
        b_ref[rows, :] = h
        return jnp.broadcast_to(h[SUBLANES - 1:SUBLANES, :], (SUBLANES, B_WIDTH))

    carry_ref[...] = lax.fori_loop(0, LRU_TS // SUBLANES, scan_body, carry_ref[...])
    o_ref[...] = (b_ref[...] * _gelu_tanh(yb_ref[...].astype(F32))).astype(o_ref.dtype)


def _lru(proj, conv_w, conv_b, ga_w, ga_b, gx_w, gx_b, lam, e):
    nts = SEQ // LRU_TS
    vec = pl.BlockSpec((None, 1, B_WIDTH), lambda b, t: (e, 0, 0))
    gate = pl.BlockSpec((None, B_BLOCKS, B_WIDTH // B_BLOCKS, B_WIDTH // B_BLOCKS), lambda b, t: (e, 0, 0, 0))
    r3 = lambda a: a.reshape(a.shape[0], 1, B_WIDTH)
    return pl.pallas_call(
        _lru_kernel,
        out_shape=jax.ShapeDtypeStruct((TOKENS, B_WIDTH), BF16),
        grid=(BATCH, nts),
        in_specs=[pl.BlockSpec((LRU_TS, B_WIDTH), lambda b, t: (b * nts + t, 3)),
                  pl.BlockSpec((LRU_TS, B_WIDTH), lambda b, t: (b * nts + t, 4)),
                  pl.BlockSpec((None, B_CONV, B_WIDTH), lambda b, t: (e, 0, 0)),
                  vec, gate, vec, gate, vec, vec],
        out_specs=pl.BlockSpec((LRU_TS, B_WIDTH), lambda b, t: (b * nts + t, 0)),
        scratch_shapes=[pltpu.VMEM((LRU_HALO + LRU_TS, B_WIDTH), F32),
                        pltpu.VMEM((LRU_TS, B_WIDTH), F32),
                        pltpu.VMEM((LRU_TS, B_WIDTH), F32),
                        pltpu.VMEM((SUBLANES, B_WIDTH), F32)],
        compiler_params=_cparams(("parallel", "arbitrary")),
        name="rg_lru",
    )(proj, proj, conv_w, r3(conv_b), ga_w, r3(ga_b), gx_w, r3(gx_b), r3(lam))


SWA_T = 128
SWA_PAIRS = C_GROUP // 2


def _swa_bias_tiles():
    qi = np.tile(np.arange(SWA_T), SWA_PAIRS)[:, None]
    kj = np.arange(2 * SWA_T)[None, :]
    delta = qi + SWA_T - kj
    band = (delta >= 0) & (delta <= C_WINDOW - 1)
    tiles = np.zeros((2, SWA_PAIRS * SWA_T, 2 * SWA_T), np.float32)
    tiles[0] = np.where(band & (kj >= SWA_T), 0.0, NEG)
    tiles[1] = np.where(band, 0.0, NEG)
    return tiles


def _swa_kernel(sink_ref, q_ref, kv_ref, cos_ref, sp_ref, sm_ref, bias_ref, o_ref,
                qs_ref, ka_ref, kb_ref, va_ref, vb_ref):
    kvh = pl.program_id(1)
    cos = cos_ref[...]
    s_plus = sp_ref[...]
    s_minus = sm_ref[...]
    quarter = C_HEAD_DIM // 2

    def rope(x):
        return x * cos + pltpu.roll(x, quarter, 1) * s_plus + pltpu.roll(x, LANES - quarter, 1) * s_minus

    scale = C_HEAD_DIM ** -0.5
    for j in range(SWA_PAIRS):
        sl = slice(j * LANES, (j + 1) * LANES)
        qs_ref[:, sl] = (rope(q_ref[:, sl].astype(F32)) * scale).astype(BF16)

    lane = lax.broadcasted_iota(jnp.int32, (SEQ, LANES), 1)
    low = lane < C_HEAD_DIM
    kk = rope(kv_ref[:, 0:LANES].astype(F32))
    vv = kv_ref[:, LANES:2 * LANES].astype(F32)
    kk = jnp.where(kvh == 0, kk, pltpu.roll(kk, C_HEAD_DIM, 1))
    vv = jnp.where(kvh == 0, vv, pltpu.roll(vv, C_HEAD_DIM, 1))
    k_lo = jnp.where(low, kk, 0.0)
    v_lo = jnp.where(low, vv, 0.0)
    zeros = jnp.zeros((SWA_T, LANES), BF16)
    for ref, val in ((ka_ref, k_lo), (kb_ref, pltpu.roll(k_lo, C_HEAD_DIM, 1)),
                     (va_ref, v_lo), (vb_ref, pltpu.roll(v_lo, C_HEAD_DIM, 1))):
        ref[0:SWA_T, :] = zeros
        ref[SWA_T:, :] = val.astype(BF16)

    rows_st = SWA_PAIRS * SWA_T
    pair = lax.broadcasted_iota(jnp.int32, (rows_st, 1), 0) // SWA_T
    sink_a = jnp.zeros((rows_st, 1), F32)
    sink_b = jnp.zeros((rows_st, 1), F32)
    for j in range(SWA_PAIRS):
        sink_a = jnp.where(pair == j, sink_ref[kvh * C_GROUP + 2 * j], sink_a)
        sink_b = jnp.where(pair == j, sink_ref[kvh * C_GROUP + 2 * j + 1], sink_b)

    def q_body(i, carry):
        r0 = pl.multiple_of(i * SWA_T, SWA_T)
        q_st = jnp.concatenate([qs_ref[pl.ds(r0, SWA_T), j * LANES:(j + 1) * LANES]
                                for j in range(SWA_PAIRS)], axis=0)
        bias = bias_ref[jnp.minimum(i, 1)]
        win = pl.ds(r0, 2 * SWA_T)
        out = jnp.zeros((rows_st, LANES), F32)
        for k_ref, v_ref, sink in ((ka_ref, va_ref, sink_a), (kb_ref, vb_ref, sink_b)):
            s = lax.dot_general(q_st, k_ref[win, :], (((1,), (1,)), ((), ())),
                                preferred_element_type=F32) + bias
            m = jnp.maximum(jnp.max(s, axis=-1, keepdims=True), sink)
            p = jnp.exp(s - m)
            den = jnp.sum(p, axis=-1, keepdims=True) + jnp.exp(sink - m)
            out = out + jnp.dot(p.astype(BF16), v_ref[win, :], preferred_element_type=F32) / den
        for j in range(SWA_PAIRS):
            o_ref[pl.ds(r0, SWA_T), j * LANES:(j + 1) * LANES] = out[j * SWA_T:(j + 1) * SWA_T, :].astype(o_ref.dtype)
        return carry

    lax.fori_loop(0, SEQ // SWA_T, q_body, 0)


def _swa(proj, sinks, cos_c, sin_cp, sin_cm, o):
    bias = jnp.asarray(_swa_bias_tiles())
    qw = C_WIDTH // C_KV_HEADS
    tab = pl.BlockSpec((SEQ, LANES), lambda b, g, *_: (b, 0))
    kv_block = C_WIDTH // (2 * C_KV_WIDTH)
    grid_spec = pltpu.PrefetchScalarGridSpec(
        num_scalar_prefetch=1,
        grid=(BATCH, C_KV_HEADS),
        in_specs=[pl.BlockSpec((SEQ, qw), lambda b, g, *_: (b, g)),
                  pl.BlockSpec((SEQ, 2 * C_KV_WIDTH), lambda b, g, *_: (b, kv_block)),
                  tab, tab, tab,
                  pl.BlockSpec((2, SWA_PAIRS * SWA_T, 2 * SWA_T), lambda b, g, *_: (0, 0, 0))],
        out_specs=pl.BlockSpec((SEQ, qw), lambda b, g, *_: (b, g)),
        scratch_shapes=[pltpu.VMEM((SEQ, qw), BF16)] + [pltpu.VMEM((SWA_T + SEQ, LANES), BF16)] * 4,
    )
    return pl.pallas_call(
        _swa_kernel,
        out_shape=jax.ShapeDtypeStruct((TOKENS, C_WIDTH), BF16),
        grid_spec=grid_spec,
        compiler_params=_cparams(("parallel", "parallel")),
        name="attn_swa",
    )(sinks[o].astype(F32), proj, proj, cos_c, sin_cp, sin_cm, bias)


S5_L = SUBLANES
S5_NC = SEQ // S5_L
S5_GPB = LANES // D_GROUP_DIM
S5_NB = D_WIDTH // LANES
S5_SW = S5_GPB * D_STATE
S5_CW = S5_L * LANES


def _s5_kernel(u_ref, wz_ref, ki_ref, mi_ref, dec_ref, d_ref, o_ref, uf_ref, sre_ref, sim_ref, y_ref):
    uf_ref[...] = u_ref[...].astype(F32)
    u_steps = [uf_ref[pl.ds(j, S5_NC, stride=S5_L), :] for j in range(S5_L)]
    u_all = jnp.concatenate(u_steps, axis=1).astype(BF16)

    z = jnp.dot(u_all, wz_ref[...], preferred_element_type=F32)
    ng = S5_NC // SUBLANES
    x_re = z[:, 0:S5_SW].reshape(ng, SUBLANES, S5_SW)
    x_im = z[:, S5_SW:2 * S5_SW].reshape(ng, SUBLANES, S5_SW)
    for idx, s in enumerate((1, 2, 4)):
        m_re = dec_ref[2 * idx]
        m_im = dec_ref[2 * idx + 1]
        r_re = pltpu.roll(x_re, s, 1)
        r_im = pltpu.roll(x_im, s, 1)
        x_re, x_im = x_re + (m_re * r_re - m_im * r_im), x_im + (m_re * r_im + m_im * r_re)
    sre_ref[...] = x_re.reshape(S5_NC, S5_SW)
    sim_ref[...] = x_im.reshape(S5_NC, S5_SW)
    p_re = dec_ref[6]
    p_im = dec_ref[7]

    def carry_body(g, c):
        c_re, c_im = c
        rows = pl.ds(pl.multiple_of(g * SUBLANES, SUBLANES), SUBLANES)
        s_re = sre_ref[rows, :] + (p_re * c_re - p_im * c_im)
        s_im = sim_ref[rows, :] + (p_re * c_im + p_im * c_re)
        sre_ref[rows, :] = s_re
        sim_ref[rows, :] = s_im
        last = slice(SUBLANES - 1, SUBLANES)
        return (jnp.broadcast_to(s_re[last, :], (SUBLANES, S5_SW)),
                jnp.broadcast_to(s_im[last, :], (SUBLANES, S5_SW)))

    zero = jnp.zeros((SUBLANES, S5_SW), F32)
    lax.fori_loop(0, ng, carry_body, (zero, zero))

    first = lax.broadcasted_iota(jnp.int32, (S5_NC, S5_SW), 0) == 0
    prev_re = jnp.where(first, 0.0, pltpu.roll(sre_ref[...], 1, 0))
    prev_im = jnp.where(first, 0.0, pltpu.roll(sim_ref[...], 1, 0))
    s_prev = jnp.concatenate([prev_re, prev_im], axis=1).astype(BF16)
    y = (jnp.dot(s_prev, mi_ref[...], preferred_element_type=F32)
         + jnp.dot(u_all, ki_ref[...], preferred_element_type=F32))
    d = d_ref[...]
    for j in range(S5_L):
        yj = y[:, j * LANES:(j + 1) * LANES] + d * u_steps[j]
        y_ref[pl.ds(j, S5_NC, stride=S5_L), :] = _gelu_tanh(yj)
    o_ref[...] = y_ref[...].astype(o_ref.dtype)


def _s5_tables(a_re, a_im, b_re, b_im, c_re, c_im, log_dt):
    dt = jnp.exp(log_dt.astype(F32))[:, None]
    lr, li = a_re.astype(F32), a_im.astype(F32)
    zr, zi = lr * dt, li * dt
    b_re, b_im, c_re, c_im = (t.astype(F32) for t in (b_re, b_im, c_re, c_im))

    def a_pow(k):
        mag = jnp.exp(k * zr)
        return mag * jnp.cos(k * zi), mag * jnp.sin(k * zi)

    ar, ai = a_pow(1.0)
    den = lr * lr + li * li
    cr = ((ar - 1.0) * lr + ai * li) / den
    ci = (ai * lr - (ar - 1.0) * li) / den
    bb_re = cr[..., None] * b_re - ci[..., None] * b_im
    bb_im = cr[..., None] * b_im + ci[..., None] * b_re
    steps = jnp.arange(S5_L, dtype=F32)[:, None, None]
    eye = jnp.eye(S5_GPB, dtype=F32)
    blk = lambda t: t.reshape(t.shape[0], S5_NB, S5_GPB, *t.shape[2:])

    pr, pi = a_pow(S5_L - 1.0 - steps)
    w_re = pr[..., None] * bb_re - pi[..., None] * bb_im
    w_im = pr[..., None] * bb_im + pi[..., None] * bb_re
    wz = jnp.concatenate(
        [jnp.einsum('jngpc,gh->njgchp', blk(w), eye).reshape(S5_NB, S5_CW, S5_SW) for w in (w_re, w_im)], axis=2)

    pr, pi = a_pow(steps + 1.0)
    m_re = c_re[None] * pr[:, :, None, :] - c_im[None] * pi[:, :, None, :]
    m_im = c_re[None] * pi[:, :, None, :] + c_im[None] * pr[:, :, None, :]
    mi = jnp.concatenate(
        [jnp.einsum('jngcp,gh->ngpjhc', blk(m), eye).reshape(S5_NB, S5_SW, S5_CW) for m in (m_re, -m_im)], axis=1)

    pr, pi = a_pow(steps)
    ca_re = c_re[None] * pr[:, :, None, :] - c_im[None] * pi[:, :, None, :]
    ca_im = c_re[None] * pi[:, :, None, :] + c_im[None] * pr[:, :, None, :]
    k_t = jnp.einsum('tgop,gpi->tgoi', ca_re, bb_re) - jnp.einsum('tgop,gpi->tgoi', ca_im, bb_im)
    lag = (jnp.arange(S5_L)[None, :, None] - jnp.arange(S5_L)[:, None, None]
           == jnp.arange(S5_L)[None, None, :]).astype(F32)
    ki = jnp.einsum('abt,tngoi,gh->nagibho', lag, blk(k_t), eye).reshape(S5_NB, S5_CW, S5_CW)

    rows = jnp.arange(SUBLANES)[:, None]
    tabs = []
    for s in (1, 2, 4):
        for p in a_pow(float(S5_L * s)):
            tabs.append(jnp.where(rows >= s, p.reshape(S5_NB, 1, S5_SW), 0.0))
    kk = (S5_L * (rows + 1)).astype(F32)
    mag = jnp.exp(kk * zr.reshape(S5_NB, 1, S5_SW))
    tabs.append(mag * jnp.cos(kk * zi.reshape(S5_NB, 1, S5_SW)))
    tabs.append(mag * jnp.sin(kk * zi.reshape(S5_NB, 1, S5_SW)))
    dec = jnp.stack(tabs, axis=1)
    return wz.astype(BF16), ki.astype(BF16), mi.astype(BF16), dec


def _s5(proj, tables, d_skip, o):
    wz, ki, mi, dec = tables
    ublock = (C_WIDTH + 2 * C_KV_WIDTH) // LANES
    return pl.pallas_call(
        _s5_kernel,
        out_shape=jax.ShapeDtypeStruct((TOKENS, D_WIDTH), BF16),
        grid=(S5_NB, BATCH),
        in_specs=[pl.BlockSpec((SEQ, LANES), lambda n, b: (b, ublock + n)),
                  pl.BlockSpec((None, S5_CW, 2 * S5_SW), lambda n, b: (n, 0, 0)),
                  pl.BlockSpec((None, S5_CW, S5_CW), lambda n, b: (n, 0, 0)),
                  pl.BlockSpec((None, 2 * S5_SW, S5_CW), lambda n, b: (n, 0, 0)),
                  pl.BlockSpec((None, 8, SUBLANES, S5_SW), lambda n, b: (n, 0, 0, 0)),
                  pl.BlockSpec((None, 1, LANES), lambda n, b: (o, 0, n))],
        out_specs=pl.BlockSpec((SEQ, LANES), lambda n, b: (b, n)),
        scratch_shapes=[pltpu.VMEM((SEQ, LANES), F32),
                        pltpu.VMEM((S5_NC, S5_SW), F32), pltpu.VMEM((S5_NC, S5_SW), F32),
                        pltpu.VMEM((SEQ, LANES), F32)],
        compiler_params=_cparams(("parallel", "parallel")),
        name="s5_ssm",
    )(proj, wz, ki, mi, dec, d_skip.reshape(-1, 1, D_WIDTH))


GLU_TN = 512


def _glu_kernel(z_ref, w_ref, b_ref, zc_ref, o_ref):
    gate = jax.nn.sigmoid(jnp.dot(z_ref[...], w_ref[...].astype(BF16), preferred_element_type=F32) + b_ref[...])
    o_ref[...] = (zc_ref[...].astype(F32) * gate).astype(o_ref.dtype)


def _glu(z, w, b, o):
    return pl.pallas_call(
        _glu_kernel,
        out_shape=jax.ShapeDtypeStruct((TOKENS, D_WIDTH), BF16),
        grid=(TOKENS // MM_TM, D_WIDTH // GLU_TN),
        in_specs=[pl.BlockSpec((MM_TM, D_WIDTH), lambda i, j: (i, 0)),
                  pl.BlockSpec((None, D_WIDTH, GLU_TN), lambda i, j: (o, 0, j)),
                  pl.BlockSpec((None, 1, GLU_TN), lambda i, j: (o, 0, j)),
                  pl.BlockSpec((MM_TM, GLU_TN), lambda i, j: (i, j))],
        out_specs=pl.BlockSpec((MM_TM, GLU_TN), lambda i, j: (i, j)),
        compiler_params=_cparams(("parallel", "parallel")),
        name="s5_glu",
    )(z, w, b.reshape(-1, 1, D_WIDTH), z)


DOWN_TM = 256


def kernel(x, c, positions, ada_w, ada_b, norm_mix, norm_ffn, norm_final, ev_w_in, ev_conv_w, ev_conv_b, ev_gate_a_w, ev_gate_a_b, ev_gate_x_w, ev_gate_x_b, ev_lambda, ev_w_out, od_w_in, od_sinks, od_a_re, od_a_im, od_b_re, od_b_im, od_c_re, od_c_im, od_d, od_log_dt, od_glu_w, od_glu_b, od_w_out, ffn_w_in, ffn_conv_w, ffn_conv_b, ffn_w_out):
    ffn_out16 = ffn_w_out.astype(BF16)
    ffn_cw = ffn_conv_w.astype(F32)
    ffn_cb = ffn_conv_b.astype(F32).reshape(DEPTH, 1, 2 * D_FF)

    mod = _ada_mod(c, ada_w, ada_b)
    mod = mod.reshape(DEPTH, SUBLANES, 6, 1, D_MODEL).transpose(0, 2, 1, 3, 4)
    cos_a, sin_a, cos_c, sin_cp, sin_cm = _rope_tables(positions)

    xt = x.reshape(TOKENS, D_MODEL).astype(F32)
    h = _prenorm(xt, norm_mix, mod, 0)
    for layer in range(DEPTH):
        idx = layer // 2
        if layer % 2 == 0:
            proj = _matmul(h, ev_w_in, idx, BF16, "even_in_proj")
            attn = _attn_a(proj, cos_a, sin_a)
            other = _lru(proj, ev_conv_w, ev_conv_b, ev_gate_a_w, ev_gate_a_b, ev_gate_x_w, ev_gate_x_b,
                         ev_lambda, idx)
            w_out = ev_w_out
        else:
            proj = _matmul(h, od_w_in, idx, BF16, "odd_in_proj")
            attn = _swa(proj, od_sinks, cos_c, sin_cp, sin_cm, idx)
            tables = _s5_tables(od_a_re[idx], od_a_im[idx], od_b_re[idx], od_b_im[idx],
                                od_c_re[idx], od_c_im[idx], od_log_dt[idx])
            z = _s5(proj, tables, od_d, idx)
            other = _glu(z, od_glu_w, od_glu_b, idx)
            w_out = od_w_out
        xt, h2, tails = _mm_res([attn, other], w_out, idx, OUT_TM, xt, mod, layer, 2, norm_ffn, layer, layer, 3,
                                False, "mix_out_proj")
        act = _ffn_up(h2, tails, ffn_w_in, ffn_cw, ffn_cb, layer)
        if layer + 1 < DEPTH:
            xt, h, _ = _mm_res([act], ffn_out16, layer, DOWN_TM, xt, mod, layer, 5, norm_mix, layer + 1, layer + 1,
                               0, False, "ffn_down_proj")
        else:
            out = _mm_res([act], ffn_out16, layer, DOWN_TM, xt, mod, layer, 5, norm_final, 0, layer, 0, True,
                          "ffn_down_final")
    return out.reshape(BATCH, SEQ, D_MODEL).astype(x.dtype)
```

```python
import functools
import math

import jax
import jax.numpy as jnp
import numpy as np
from jax import lax
from jax.experimental import pallas as pl
from jax.experimental.pallas import tpu as pltpu

F32 = jnp.float32
BF16 = jnp.bfloat16

D_MODEL = 2048
BATCH = 4
SEQ = 2048
TOKENS = BATCH * SEQ
DEPTH = 4
ROPE_THETA = 10000.0
NORM_EPS = 1e-6
LANES = 128
SUBLANES = 8
BF16_ROWS = 16

A_HEAD_DIM = 128
A_HEADS = 8
A_WIDTH = 1024
A_PATTERNS = ((128, 1), (512, 4), (2048, 16))
B_WIDTH = 1024
B_BLOCKS = 8
B_CONV = 4
LRU_C = 8.0
EVEN_IN = 3 * A_WIDTH + 2 * B_WIDTH

C_HEAD_DIM = 64
C_HEADS = 16
C_KV_HEADS = 2
C_GROUP = 8
C_WIDTH = 1024
C_KV_WIDTH = 128
C_WINDOW = 128
D_WIDTH = 1024
D_GROUP_DIM = 16
D_GROUPS = 64
D_STATE = 64
ODD_IN = C_WIDTH + 2 * C_KV_WIDTH + D_WIDTH

D_FF = 5504
D_FF_PAD = 5632
FFN_CONV = 3

NEG = -1e30

VMEM_LIMIT = 56 * 1024 * 1024


def _cparams(sem, vmem=VMEM_LIMIT):
    return pltpu.CompilerParams(dimension_semantics=sem, vmem_limit_bytes=vmem)


GELU_C1 = 2.0 * math.sqrt(2.0 / math.pi)
GELU_C2 = 0.044715 * GELU_C1


def _gelu_tanh(x):
    z = x * (GELU_C1 + GELU_C2 * (x * x))
    return x / (1.0 + jnp.exp(-z))


ADA_TN = 1024


def _ada_kernel(c_ref, w_ref, b_ref, o_ref):
    c = c_ref[...]
    cond = (c * jax.nn.sigmoid(c)).astype(BF16)
    o_ref[...] = jnp.dot(cond, w_ref[...].astype(BF16), preferred_element_type=F32) + b_ref[...]


def _ada_mod(c, ada_w, ada_b):
    c8 = jnp.zeros((SUBLANES, D_MODEL), F32).at[:BATCH].set(c.astype(F32))
    n = 6 * D_MODEL
    return pl.pallas_call(
        _ada_kernel,
        out_shape=jax.ShapeDtypeStruct((DEPTH, SUBLANES, n), F32),
        grid=(DEPTH, n // ADA_TN),
        in_specs=[
            pl.BlockSpec((SUBLANES, D_MODEL), lambda l, j: (0, 0)),
            pl.BlockSpec((None, D_MODEL, ADA_TN), lambda l, j: (l, 0, j)),
            pl.BlockSpec((None, 1, ADA_TN), lambda l, j: (l, 0, j)),
        ],
        out_specs=pl.BlockSpec((None, SUBLANES, ADA_TN), lambda l, j: (l, 0, j)),
        compiler_params=_cparams(("parallel", "parallel")),
        name="ada_mod",
    )(c8, ada_w, ada_b.reshape(DEPTH, 1, n))


ROPE_TM = 1024


def _rope_kernel(pos_ref, inva_ref, invc_ref, ca_ref, sa_ref, cc_ref, scp_ref, scm_ref):
    pos = pos_ref[...].astype(F32)
    lane = lax.broadcasted_iota(jnp.int32, (ROPE_TM, LANES), 1)
    ang = pos * inva_ref[...]
    s = jnp.sin(ang)
    ca_ref[...] = jnp.cos(ang)
    sa_ref[...] = jnp.where(lane < A_HEAD_DIM // 2, -s, s)
    ang = pos * invc_ref[...]
    s = jnp.sin(ang)
    cc_ref[...] = jnp.cos(ang)
    second = (lane % C_HEAD_DIM) >= C_HEAD_DIM // 2
    scp_ref[...] = jnp.where(second, s, 0.0)
    scm_ref[...] = jnp.where(second, 0.0, -s)


def _rope_tables(positions):
    half_a, half_c = A_HEAD_DIM // 2, C_HEAD_DIM // 2
    inv_a = ROPE_THETA ** (-jnp.arange(half_a, dtype=F32) / half_a)
    inv_c = ROPE_THETA ** (-jnp.arange(half_c, dtype=F32) / half_c)
    inv_a = jnp.tile(inv_a, LANES // half_a).reshape(1, LANES)
    inv_c = jnp.tile(inv_c, LANES // half_c).reshape(1, LANES)
    tab = jax.ShapeDtypeStruct((TOKENS, LANES), F32)
    row = pl.BlockSpec((ROPE_TM, LANES), lambda i: (i, 0))
    vec = pl.BlockSpec((1, LANES), lambda i: (0, 0))
    return pl.pallas_call(
        _rope_kernel,
        out_shape=(tab,) * 5,
        grid=(TOKENS // ROPE_TM,),
        in_specs=[pl.BlockSpec((ROPE_TM, 1), lambda i: (i, 0)), vec, vec],
        out_specs=(row,) * 5,
        compiler_params=_cparams(("parallel",)),
        name="rope_tables",
    )(positions.reshape(TOKENS, 1), inv_a, inv_c)


def _norm_mod(x, g, sh, sc):
    ms = jnp.mean(x * x, axis=-1, keepdims=True)
    y = x * lax.rsqrt(ms + NORM_EPS) * g
    return y * (1.0 + sc) + sh


def _rmsnorm(x, g):
    ms = jnp.mean(x * x, axis=-1, keepdims=True)
    return x * lax.rsqrt(ms + NORM_EPS) * g


NORM_TM = 512


def _prenorm_kernel(x_ref, g_ref, sh_ref, sc_ref, h_ref):
    h_ref[...] = _norm_mod(x_ref[...], g_ref[...], sh_ref[...], sc_ref[...]).astype(BF16)


def _mod_spec(layer, chunk, tm):
    return pl.BlockSpec((None, None, None, 1, D_MODEL),
                        lambda i, *_: (layer, chunk, (i * tm) // SEQ, 0, 0))


def _prenorm(x, norm_g, mod, layer):
    vec = pl.BlockSpec((None, 1, D_MODEL), lambda i: (layer, 0, 0))
    return pl.pallas_call(
        _prenorm_kernel,
        out_shape=jax.ShapeDtypeStruct((TOKENS, D_MODEL), BF16),
        grid=(TOKENS // NORM_TM,),
        in_specs=[pl.BlockSpec((NORM_TM, D_MODEL), lambda i: (i, 0)), vec,
                  _mod_spec(layer, 0, NORM_TM), _mod_spec(layer, 1, NORM_TM)],
        out_specs=pl.BlockSpec((NORM_TM, D_MODEL), lambda i: (i, 0)),
        compiler_params=_cparams(("parallel",)),
        name="prenorm",
    )(x, norm_g.reshape(DEPTH, 1, D_MODEL), mod, mod)


MM_TM = 1024
MM_TN_CHOICES = (1024, 768, 512, 256)


def _mm_kernel(a_ref, w_ref, o_ref):
    o_ref[...] = jnp.dot(a_ref[...], w_ref[...].astype(BF16), preferred_element_type=F32).astype(o_ref.dtype)


def _matmul(a, w, idx, out_dtype, name):
    m, k = a.shape
    n = w.shape[-1]
    tn = next(t for t in MM_TN_CHOICES if n % t == 0)
    return pl.pallas_call(
        _mm_kernel,
        out_shape=jax.ShapeDtypeStruct((m, n), out_dtype),
        grid=(m // MM_TM, n // tn),
        in_specs=[pl.BlockSpec((MM_TM, k), lambda i, j: (i, 0)),
                  pl.BlockSpec((None, k, tn), lambda i, j: (idx, 0, j))],
        out_specs=pl.BlockSpec((MM_TM, tn), lambda i, j: (i, j)),
        compiler_params=_cparams(("parallel", "parallel")),
        name=name,
    )(a, w)


FFN_HALO = BF16_ROWS


def _mm_res_kernel(*refs, n_lhs, final, cast_w):
    a_refs = refs[:n_lhs]
    w_ref, x_ref, gate_ref, g_ref, sh_ref, sc_ref = refs[n_lhs:n_lhs + 6]
    outs = refs[n_lhs + 6:]
    if cast_w:
        outs, w16_ref = outs[:-1], outs[-1]

        @pl.when(pl.program_id(0) == 0)
        def _():
            w16_ref[...] = w_ref[...].astype(BF16)

        w_ref = w16_ref
    y = None
    row0 = 0
    for a_ref in a_refs:
        kk = min(a_ref.shape[1], w_ref.shape[0] - row0)
        part = jnp.dot(a_ref[:, 0:kk], w_ref[row0:row0 + kk, :], preferred_element_type=F32)
        y = part if y is None else y + part
        row0 += kk
    xn = x_ref[...] + gate_ref[...] * y
    if final:
        outs[0][...] = _rmsnorm(xn, g_ref[...])
    else:
        h = _norm_mod(xn, g_ref[...], sh_ref[...], sc_ref[...]).astype(BF16)
        outs[0][...] = xn
        outs[1][...] = h
        outs[2][...] = h[h.shape[0] - FFN_HALO:, :]


def _mm_res(lhs, w, widx, tm, x, mod, gate_layer, gate_chunk, norm_g, norm_idx, mod_layer, mod_chunk, final, name):
    m = lhs[0].shape[0]
    kdim = w.shape[1]
    cast_w = w.dtype != BF16
    row = pl.BlockSpec((tm, D_MODEL), lambda i: (i, 0))
    if final:
        gvec = pl.BlockSpec((1, D_MODEL), lambda i: (0, 0))
        g_arr = norm_g.reshape(1, D_MODEL)
        out_shape = jax.ShapeDtypeStruct((m, D_MODEL), F32)
        out_specs = row
    else:
        gvec = pl.BlockSpec((None, 1, D_MODEL), lambda i: (norm_idx, 0, 0))
        g_arr = norm_g.reshape(DEPTH, 1, D_MODEL)
        out_shape = (jax.ShapeDtypeStruct((m, D_MODEL), F32), jax.ShapeDtypeStruct((m, D_MODEL), BF16),
                     jax.ShapeDtypeStruct((m // tm, FFN_HALO, D_MODEL), BF16))
        out_specs = (row, row, pl.BlockSpec((None, FFN_HALO, D_MODEL), lambda i: (i, 0, 0)))
    return pl.pallas_call(
        functools.partial(_mm_res_kernel, n_lhs=len(lhs), final=final, cast_w=cast_w),
        out_shape=out_shape,
        grid=(m // tm,),
        in_specs=[pl.BlockSpec((tm, a.shape[1]), lambda i: (i, 0)) for a in lhs] + [
                  pl.BlockSpec((None, kdim, D_MODEL), lambda i: (widx, 0, 0), pipeline_mode=pl.Buffered(1)),
                  row,
                  _mod_spec(gate_layer, gate_chunk, tm),
                  gvec,
                  _mod_spec(mod_layer, mod_chunk, tm),
                  _mod_spec(mod_layer, mod_chunk + 1, tm)],
        out_specs=out_specs,
        scratch_shapes=[pltpu.VMEM((kdim, D_MODEL), BF16)] if cast_w else [],
        compiler_params=_cparams(("arbitrary",)),
        name=name,
    )(*lhs, w, x, mod, g_arr, mod, mod)


FFN_TM = 1024
FFN_TF = 512
FFN_SUB = 256
FFN_NT = TOKENS // FFN_TM
FFN_NF = D_FF_PAD // FFN_TF
FFN_EDGE = D_FF_PAD - D_FF
OUT_TM = 512


def _ffn_up_kernel(h_ref, halo_ref, wg_ref, wv_win_ref, cwg_ref, cwv_win_ref, cbg_ref, cbv_win_ref,
                   o_ref, hcat_ref, wg16_ref, wv_ref, cv_ref):
    i = pl.program_id(0)
    j = pl.program_id(1)
    wg16_ref[...] = wg_ref[...].astype(BF16)

    @pl.when(j == 0)
    def _():
        starts_seq = (i * FFN_TM) % SEQ == 0
        hcat_ref[0:FFN_HALO, :] = jnp.where(starts_seq, jnp.zeros(halo_ref.shape, BF16), halo_ref[...])
        hcat_ref[FFN_HALO:, :] = h_ref[...]

    keep = FFN_TF - FFN_EDGE

    @pl.when(j == FFN_NF - 1)
    def _():
        wv_ref[:, 0:keep] = wv_win_ref[:, FFN_EDGE:FFN_TF].astype(BF16)
        wv_ref[:, keep:FFN_TF] = wv_win_ref[:, 0:FFN_EDGE].astype(BF16)
        cv_ref[0:FFN_CONV, 0:keep] = cwv_win_ref[:, FFN_EDGE:FFN_TF]
        cv_ref[FFN_CONV:FFN_CONV + 1, 0:keep] = cbv_win_ref[:, FFN_EDGE:FFN_TF]
        cv_ref[:, keep:FFN_TF] = jnp.zeros((SUBLANES, FFN_EDGE), F32)

    @pl.when(j != FFN_NF - 1)
    def _():
        wv_ref[...] = wv_win_ref[...].astype(BF16)
        cv_ref[0:FFN_CONV, :] = cwv_win_ref[...]
        cv_ref[FFN_CONV:FFN_CONV + 1, :] = cbv_win_ref[...]

    lhs = hcat_ref[...]

    def conv(u, w, b):
        out = b + w[FFN_CONV - 1:FFN_CONV, :] * u[FFN_HALO:, :]
        for k in range(1, FFN_CONV):
            out = out + w[FFN_CONV - 1 - k:FFN_CONV - k, :] * pltpu.roll(u, k, 0)[FFN_HALO:, :]
        return out

    for c in range(FFN_TF // FFN_SUB):
        sl = slice(c * FFN_SUB, (c + 1) * FFN_SUB)
        ug = jnp.dot(lhs, wg16_ref[:, sl], preferred_element_type=F32)
        uv = jnp.dot(lhs, wv_ref[:, sl], preferred_element_type=F32)
        g = conv(ug, cwg_ref[:, sl], cbg_ref[:, sl])
        v = conv(uv, cv_ref[0:FFN_CONV, sl], cv_ref[FFN_CONV:FFN_CONV + 1, sl])
        o_ref[:, sl] = (_gelu_tanh(g) * v).astype(BF16)


def _ffn_up(h, tails, w_in, conv_w, conv_b, layer):
    per = FFN_TM // OUT_TM

    def voff(j):
        return LANES * jnp.minimum(D_FF // LANES + (FFN_TF // LANES) * j, (2 * D_FF - FFN_TF) // LANES)

    def win(rows):
        return pl.BlockSpec((None, pl.Element(rows), pl.Element(FFN_TF)), lambda i, j: (layer, 0, voff(j)))

    return pl.pallas_call(
        _ffn_up_kernel,
        out_shape=jax.ShapeDtypeStruct((TOKENS, D_FF_PAD), BF16),
        grid=(FFN_NT, FFN_NF),
        in_specs=[pl.BlockSpec((FFN_TM, D_MODEL), lambda i, j: (i, 0)),
                  pl.BlockSpec((None, FFN_HALO, D_MODEL), lambda i, j: (jnp.maximum(per * i - 1, 0), 0, 0)),
                  pl.BlockSpec((None, D_MODEL, FFN_TF), lambda i, j: (layer, 0, j)),
                  win(D_MODEL),
                  pl.BlockSpec((None, FFN_CONV, FFN_TF), lambda i, j: (layer, 0, j)),
                  win(FFN_CONV),
                  pl.BlockSpec((None, 1, FFN_TF), lambda i, j: (layer, 0, j)),
                  win(1)],
        out_specs=pl.BlockSpec((FFN_TM, FFN_TF), lambda i, j: (i, j)),
        scratch_shapes=[pltpu.VMEM((FFN_HALO + FFN_TM, D_MODEL), BF16),
                        pltpu.VMEM((D_MODEL, FFN_TF), BF16),
                        pltpu.VMEM((D_MODEL, FFN_TF), BF16),
                        pltpu.VMEM((SUBLANES, FFN_TF), F32)],
        compiler_params=_cparams(("parallel", "arbitrary")),
        name="ffn_up",
    )(h, tails, w_in, w_in, conv_w, conv_w, conv_b, conv_b)


ATT_T = 256
ATT_NBIAS = 4


def _dilated_bias_tiles():
    tiles = np.zeros((ATT_NBIAS, ATT_T, ATT_T), np.float32)
    qi = np.arange(ATT_T)[:, None]
    kj = np.arange(ATT_T)[None, :]
    for off in range(ATT_NBIAS):
        delta = off * ATT_T + qi - kj
        count = np.zeros_like(delta)
        for window, dil in A_PATTERNS:
            count += ((delta >= 0) & (delta <= window) & (delta % dil == 0)).astype(delta.dtype)
        tiles[off] = np.where(count > 0, np.log(np.maximum(count, 1)), NEG)
    return tiles


def _attn_a_kernel(q_ref, k_ref, v_ref, cos_ref, sin_ref, bias_ref, o_ref, qs_ref, ks_ref, vs_ref):
    cos = cos_ref[...]
    sin = sin_ref[...]
    half = A_HEAD_DIM // 2
    q = q_ref[...].astype(F32)
    k = k_ref[...].astype(F32)
    scale = A_HEAD_DIM ** -0.5
    qs_ref[...] = ((q * cos + pltpu.roll(q, half, 1) * sin) * scale).astype(BF16)
    ks_ref[...] = (k * cos + pltpu.roll(k, half, 1) * sin).astype(BF16)
    vs_ref[...] = v_ref[...].astype(BF16)

    for i in range(SEQ // ATT_T):
        n = (i + 1) * ATT_T
        q_blk = qs_ref[i * ATT_T:n, :]
        s = lax.dot_general(q_blk, ks_ref[0:n, :], (((1,), (1,)), ((), ())), preferred_element_type=F32)
        s = s + jnp.concatenate([bias_ref[min(i - j, ATT_NBIAS - 1)] for j in range(i + 1)], axis=1)
        m = jnp.max(s, axis=-1, keepdims=True)
        p = jnp.exp(s - m)
        l = jnp.sum(p, axis=-1, keepdims=True)
        acc = jnp.dot(p.astype(BF16), vs_ref[0:n, :], preferred_element_type=F32)
        o_ref[i * ATT_T:n, :] = (acc / l).astype(o_ref.dtype)


def _attn_a(proj, cos_a, sin_a):
    bias = jnp.asarray(_dilated_bias_tiles())
    tab = pl.BlockSpec((SEQ, LANES), lambda b, h: (b, 0))
    return pl.pallas_call(
        _attn_a_kernel,
        out_shape=jax.ShapeDtypeStruct((TOKENS, A_WIDTH), BF16),
        grid=(BATCH, A_HEADS),
        in_specs=[pl.BlockSpec((SEQ, A_HEAD_DIM), lambda b, h: (b, h)),
                  pl.BlockSpec((SEQ, A_HEAD_DIM), lambda b, h: (b, A_HEADS + h)),
                  pl.BlockSpec((SEQ, A_HEAD_DIM), lambda b, h: (b, 2 * A_HEADS + h)),
                  tab, tab,
                  pl.BlockSpec((ATT_NBIAS, ATT_T, ATT_T), lambda b, h: (0, 0, 0))],
        out_specs=pl.BlockSpec((SEQ, A_HEAD_DIM), lambda b, h: (b, h)),
        scratch_shapes=[pltpu.VMEM((SEQ, A_HEAD_DIM), BF16)] * 3,
        compiler_params=_cparams(("parallel", "parallel")),
        name="attn_dilated",
    )(proj, proj, proj, cos_a, sin_a, bias)


LRU_TS = 512
LRU_HALO = SUBLANES


def _lru_kernel(xb_ref, yb_ref, cw_ref, cb_ref, ga_ref, gab_ref, gx_ref, gxb_ref, lam_ref, o_ref,
                ext_ref, a_ref, b_ref, carry_ref):
    t = pl.program_id(1)

    @pl.when(t == 0)
    def _():
        ext_ref[0:LRU_HALO, :] = jnp.zeros((LRU_HALO, B_WIDTH), F32)
        carry_ref[...] = jnp.zeros_like(carry_ref)

    ext_ref[LRU_HALO:, :] = xb_ref[...].astype(F32)
    ext = ext_ref[...]
    base = LRU_HALO - (B_CONV - 1)
    xc = cb_ref[...]
    for i in range(B_CONV):
        xc = xc + cw_ref[i:i + 1, :] * ext[base + i:base + i + LRU_TS, :]
    ext_ref[0:LRU_HALO, :] = ext[LRU_TS:LRU_TS + LRU_HALO, :]

    lam = lam_ref[...]
    neg_sp = -LRU_C * (jnp.maximum(-lam, 0.0) + jnp.log1p(jnp.exp(-jnp.abs(lam))))
    width = B_WIDTH // B_BLOCKS
    for blk in range(B_BLOCKS):
        sl = slice(blk * width, (blk + 1) * width)
        xh = xc[:, sl]
        xh16 = xh.astype(BF16)
        r = jax.nn.sigmoid(jnp.dot(xh16, ga_ref[blk].astype(BF16), preferred_element_type=F32) + gab_ref[:, sl])
        gi = jax.nn.sigmoid(jnp.dot(xh16, gx_ref[blk].astype(BF16), preferred_element_type=F32) + gxb_ref[:, sl])
        log_a = r * neg_sp[:, sl]
        a_ref[:, sl] = jnp.exp(log_a)
        th = jnp.tanh(log_a)
        b_ref[:, sl] = jnp.sqrt(-2.0 * th / (1.0 - th)) * (gi * xh)

    row = lax.broadcasted_iota(jnp.int32, (SUBLANES, B_WIDTH), 0)

    def scan_body(g, h_prev):
        rows = pl.ds(pl.multiple_of(g * SUBLANES, SUBLANES), SUBLANES)
        a = a_ref[rows, :]
        b = b_ref[rows, :]
        for s in (1, 2, 4):
            keep = row >= s
            a_sh = jnp.where(keep, pltpu.roll(a, s, 0), 1.0)
            b_sh = jnp.where(keep, pltpu.roll(b, s, 0), 0.0)
            b = a * b_sh + b
            a = a * a_sh
        h = a * h_prev + b
        b_ref[rows, :] = h
        return jnp.broadcast_to(h[SUBLANES - 1:SUBLANES, :], (SUBLANES, B_WIDTH))

    carry_ref[...] = lax.fori_loop(0, LRU_TS // SUBLANES, scan_body, carry_ref[...])
    o_ref[...] = (b_ref[...] * _gelu_tanh(yb_ref[...].astype(F32))).astype(o_ref.dtype)


def _lru(proj, conv_w, conv_b, ga_w, ga_b, gx_w, gx_b, lam, e):
    nts = SEQ // LRU_TS
    vec = pl.BlockSpec((None, 1, B_WIDTH), lambda b, t: (e, 0, 0))
    gate = pl.BlockSpec((None, B_BLOCKS, B_WIDTH // B_BLOCKS, B_WIDTH // B_BLOCKS), lambda b, t: (e, 0, 0, 0))
    r3 = lambda a: a.reshape(a.shape[0], 1, B_WIDTH)
    return pl.pallas_call(
        _lru_kernel,
        out_shape=jax.ShapeDtypeStruct((TOKENS, B_WIDTH), BF16),
        grid=(BATCH, nts),
        in_specs=[pl.BlockSpec((LRU_TS, B_WIDTH), lambda b, t: (b * nts + t, 3)),
                  pl.BlockSpec((LRU_TS, B_WIDTH), lambda b, t: (b * nts + t, 4)),
                  pl.BlockSpec((None, B_CONV, B_WIDTH), lambda b, t: (e, 0, 0)),
                  vec, gate, vec, gate, vec, vec],
        out_specs=pl.BlockSpec((LRU_TS, B_WIDTH), lambda b, t: (b * nts + t, 0)),
        scratch_shapes=[pltpu.VMEM((LRU_HALO + LRU_TS, B_WIDTH), F32),
                        pltpu.VMEM((LRU_TS, B_WIDTH), F32),
                        pltpu.VMEM((LRU_TS, B_WIDTH), F32),
                        pltpu.VMEM((SUBLANES, B_WIDTH), F32)],
        compiler_params=_cparams(("parallel", "arbitrary")),
        name="rg_lru",
    )(proj, proj, conv_w, r3(conv_b), ga_w, r3(ga_b), gx_w, r3(gx_b), r3(lam))


SWA_T = 128
SWA_PAIRS = C_GROUP // 2


def _swa_bias_tiles():
    qi = np.tile(np.arange(SWA_T), SWA_PAIRS)[:, None]
    kj = np.arange(2 * SWA_T)[None, :]
    delta = qi + SWA_T - kj
    band = (delta >= 0) & (delta <= C_WINDOW - 1)
    tiles = np.zeros((2, SWA_PAIRS * SWA_T, 2 * SWA_T), np.float32)
    tiles[0] = np.where(band & (kj >= SWA_T), 0.0, NEG)
    tiles[1] = np.where(band, 0.0, NEG)
    return tiles


def _swa_kernel(sink_ref, q_ref, kv_ref, cos_ref, sp_ref, sm_ref, bias_ref, o_ref,
                qs_ref, ka_ref, kb_ref, va_ref, vb_ref):
    kvh = pl.program_id(1)
    cos = cos_ref[...]
    s_plus = sp_ref[...]
    s_minus = sm_ref[...]
    quarter = C_HEAD_DIM // 2

    def rope(x):
        return x * cos + pltpu.roll(x, quarter, 1) * s_plus + pltpu.roll(x, LANES - quarter, 1) * s_minus

    scale = C_HEAD_DIM ** -0.5
    for j in range(SWA_PAIRS):
        sl = slice(j * LANES, (j + 1) * LANES)
        qs_ref[:, sl] = (rope(q_ref[:, sl].astype(F32)) * scale).astype(BF16)

    lane = lax.broadcasted_iota(jnp.int32, (SEQ, LANES), 1)
    low = lane < C_HEAD_DIM
    kk = rope(kv_ref[:, 0:LANES].astype(F32))
    vv = kv_ref[:, LANES:2 * LANES].astype(F32)
    kk = jnp.where(kvh == 0, kk, pltpu.roll(kk, C_HEAD_DIM, 1))
    vv = jnp.where(kvh == 0, vv, pltpu.roll(vv, C_HEAD_DIM, 1))
    k_lo = jnp.where(low, kk, 0.0)
    v_lo = jnp.where(low, vv, 0.0)
    zeros = jnp.zeros((SWA_T, LANES), BF16)
    for ref, val in ((ka_ref, k_lo), (kb_ref, pltpu.roll(k_lo, C_HEAD_DIM, 1)),
                     (va_ref, v_lo), (vb_ref, pltpu.roll(v_lo, C_HEAD_DIM, 1))):
        ref[0:SWA_T, :] = zeros
        ref[SWA_T:, :] = val.astype(BF16)

    rows_st = SWA_PAIRS * SWA_T
    pair = lax.broadcasted_iota(jnp.int32, (rows_st, 1), 0) // SWA_T
    sink_a = jnp.zeros((rows_st, 1), F32)
    sink_b = jnp.zeros((rows_st, 1), F32)
    for j in range(SWA_PAIRS):
        sink_a = jnp.where(pair == j, sink_ref[kvh * C_GROUP + 2 * j], sink_a)
        sink_b = jnp.where(pair == j, sink_ref[kvh * C_GROUP + 2 * j + 1], sink_b)

    def q_body(i, carry):
        r0 = pl.multiple_of(i * SWA_T, SWA_T)
        q_st = jnp.concatenate([qs_ref[pl.ds(r0, SWA_T), j * LANES:(j + 1) * LANES]
                                for j in range(SWA_PAIRS)], axis=0)
        bias = bias_ref[jnp.minimum(i, 1)]
        win = pl.ds(r0, 2 * SWA_T)
        out = jnp.zeros((rows_st, LANES), F32)
        for k_ref, v_ref, sink in ((ka_ref, va_ref, sink_a), (kb_ref, vb_ref, sink_b)):
            s = lax.dot_general(q_st, k_ref[win, :], (((1,), (1,)), ((), ())),
                                preferred_element_type=F32) + bias
            m = jnp.maximum(jnp.max(s, axis=-1, keepdims=True), sink)
            p = jnp.exp(s - m)
            den = jnp.sum(p, axis=-1, keepdims=True) + jnp.exp(sink - m)
            out = out + jnp.dot(p.astype(BF16), v_ref[win, :], preferred_element_type=F32) / den
        for j in range(SWA_PAIRS):
            o_ref[pl.ds(r0, SWA_T), j * LANES:(j + 1) * LANES] = out[j * SWA_T:(j + 1) * SWA_T, :].astype(o_ref.dtype)
        return carry

    lax.fori_loop(0, SEQ // SWA_T, q_body, 0)


def _swa(proj, sinks, cos_c, sin_cp, sin_cm, o):
    bias = jnp.asarray(_swa_bias_tiles())
    qw = C_WIDTH // C_KV_HEADS
    tab = pl.BlockSpec((SEQ, LANES), lambda b, g, *_: (b, 0))
    kv_block = C_WIDTH // (2 * C_KV_WIDTH)
    grid_spec = pltpu.PrefetchScalarGridSpec(
        num_scalar_prefetch=1,
        grid=(BATCH, C_KV_HEADS),
        in_specs=[pl.BlockSpec((SEQ, qw), lambda b, g, *_: (b, g)),
                  pl.BlockSpec((SEQ, 2 * C_KV_WIDTH), lambda b, g, *_: (b, kv_block)),
                  tab, tab, tab,
                  pl.BlockSpec((2, SWA_PAIRS * SWA_T, 2 * SWA_T), lambda b, g, *_: (0, 0, 0))],
        out_specs=pl.BlockSpec((SEQ, qw), lambda b, g, *_: (b, g)),
        scratch_shapes=[pltpu.VMEM((SEQ, qw), BF16)] + [pltpu.VMEM((SWA_T + SEQ, LANES), BF16)] * 4,
    )
    return pl.pallas_call(
        _swa_kernel,
        out_shape=jax.ShapeDtypeStruct((TOKENS, C_WIDTH), BF16),
        grid_spec=grid_spec,
        compiler_params=_cparams(("parallel", "parallel")),
        name="attn_swa",
    )(sinks[o].astype(F32), proj, proj, cos_c, sin_cp, sin_cm, bias)


S5_L = SUBLANES
S5_NC = SEQ // S5_L
S5_GPB = LANES // D_GROUP_DIM
S5_NB = D_WIDTH // LANES
S5_SW = S5_GPB * D_STATE
S5_CW = S5_L * LANES


def _s5_build_operators(are_ref, aim_ref, dt_ref, bre_ref, bim_ref, cre_ref, cim_ref,
                         wz_ref, ki_ref, mit_ref, dec_ref):
    lr, li, dt = are_ref[...], aim_ref[...], dt_ref[...]
    zr, zi = lr * dt, li * dt

    def a_pow(k):
        mag = jnp.exp(k * zr)
        return mag * jnp.cos(k * zi), mag * jnp.sin(k * zi)

    ar, ai = a_pow(1.0)
    den = lr * lr + li * li
    cr = ((ar - 1.0) * lr + ai * li) / den
    ci = (ai * lr - (ar - 1.0) * li) / den
    row = lax.broadcasted_iota(jnp.int32, (LANES, S5_SW), 0)
    col = lax.broadcasted_iota(jnp.int32, (LANES, S5_SW), 1)
    diag = (row // D_GROUP_DIM) == (col // D_STATE)

    def expand(x_ref):
        return jnp.where(diag, jnp.concatenate([x_ref[...]] * S5_GPB, axis=1), 0.0)

    b_r, b_i = expand(bre_ref), expand(bim_ref)
    bb_r = b_r * cr - b_i * ci
    bb_i = b_r * ci + b_i * cr
    c_r, c_i = expand(cre_ref), expand(cim_ref)

    for i in range(S5_L):
        pr, pi = a_pow(float(S5_L - 1 - i))
        rows = slice(i * LANES, (i + 1) * LANES)
        wz_ref[rows, 0:S5_SW] = (bb_r * pr - bb_i * pi).astype(BF16)
        wz_ref[rows, S5_SW:2 * S5_SW] = (bb_r * pi + bb_i * pr).astype(BF16)

    c_pow = []
    for k in range(S5_L + 1):
        pr, pi = a_pow(float(k))
        c_pow.append((c_r * pr - c_i * pi, c_r * pi + c_i * pr))
    for j in range(S5_L):
        rows = slice(j * LANES, (j + 1) * LANES)
        mit_ref[rows, 0:S5_SW] = c_pow[j + 1][0].astype(BF16)
        mit_ref[rows, S5_SW:2 * S5_SW] = (-c_pow[j + 1][1]).astype(BF16)

    nt = (((1,), (1,)), ((), ()))
    hi = lax.Precision.HIGHEST
    k_lag = []
    for t in range(S5_L):
        k_t = (lax.dot_general(bb_r, c_pow[t][0], nt, precision=hi, preferred_element_type=F32)
               - lax.dot_general(bb_i, c_pow[t][1], nt, precision=hi, preferred_element_type=F32))
        k_lag.append(k_t.astype(BF16))
    zero = jnp.zeros((LANES, LANES), BF16)
    for a in range(S5_L):
        for b in range(S5_L):
            ki_ref[a * LANES:(a + 1) * LANES, b * LANES:(b + 1) * LANES] = k_lag[b - a] if b >= a else zero

    rows8 = lax.broadcasted_iota(jnp.int32, (SUBLANES, S5_SW), 0)
    for idx, s in enumerate((1, 2, 4)):
        pr, pi = a_pow(float(S5_L * s))
        dec_ref[2 * idx] = jnp.where(rows8 >= s, pr, 0.0)
        dec_ref[2 * idx + 1] = jnp.where(rows8 >= s, pi, 0.0)
    kk = (S5_L * (rows8 + 1)).astype(F32)
    mag = jnp.exp(kk * zr)
    dec_ref[6] = mag * jnp.cos(kk * zi)
    dec_ref[7] = mag * jnp.sin(kk * zi)


def _s5_kernel(u_ref, are_ref, aim_ref, dt_ref, bre_ref, bim_ref, cre_ref, cim_ref, d_ref, o_ref,
               wz_ref, ki_ref, mit_ref, dec_ref, uf_ref, sre_ref, sim_ref, y_ref):
    @pl.when(pl.program_id(1) == 0)
    def _():
        _s5_build_operators(are_ref, aim_ref, dt_ref, bre_ref, bim_ref, cre_ref, cim_ref,
                            wz_ref, ki_ref, mit_ref, dec_ref)

    uf_ref[...] = u_ref[...].astype(F32)
    u_steps = [uf_ref[pl.ds(j, S5_NC, stride=S5_L), :] for j in range(S5_L)]
    u_all = jnp.concatenate(u_steps, axis=1).astype(BF16)

    z = jnp.dot(u_all, wz_ref[...], preferred_element_type=F32)
    ng = S5_NC // SUBLANES
    x_re = z[:, 0:S5_SW].reshape(ng, SUBLANES, S5_SW)
    x_im = z[:, S5_SW:2 * S5_SW].reshape(ng, SUBLANES, S5_SW)
    for idx, s in enumerate((1, 2, 4)):
        m_re = dec_ref[2 * idx]
        m_im = dec_ref[2 * idx + 1]
        r_re = pltpu.roll(x_re, s, 1)
        r_im = pltpu.roll(x_im, s, 1)
        x_re, x_im = x_re + (m_re * r_re - m_im * r_im), x_im + (m_re * r_im + m_im * r_re)
    sre_ref[...] = x_re.reshape(S5_NC, S5_SW)
    sim_ref[...] = x_im.reshape(S5_NC, S5_SW)
    p_re = dec_ref[6]
    p_im = dec_ref[7]

    def carry_body(g, c):
        c_re, c_im = c
        rows = pl.ds(pl.multiple_of(g * SUBLANES, SUBLANES), SUBLANES)
        s_re = sre_ref[rows, :] + (p_re * c_re - p_im * c_im)
        s_im = sim_ref[rows, :] + (p_re * c_im + p_im * c_re)
        sre_ref[rows, :] = s_re
        sim_ref[rows, :] = s_im
        last = slice(SUBLANES - 1, SUBLANES)
        return (jnp.broadcast_to(s_re[last, :], (SUBLANES, S5_SW)),
                jnp.broadcast_to(s_im[last, :], (SUBLANES, S5_SW)))

    zero = jnp.zeros((SUBLANES, S5_SW), F32)
    lax.fori_loop(0, ng, carry_body, (zero, zero))

    first = lax.broadcasted_iota(jnp.int32, (S5_NC, S5_SW), 0) == 0
    prev_re = jnp.where(first, 0.0, pltpu.roll(sre_ref[...], 1, 0))
    prev_im = jnp.where(first, 0.0, pltpu.roll(sim_ref[...], 1, 0))
    s_prev = jnp.concatenate([prev_re, prev_im], axis=1).astype(BF16)
    y = (lax.dot_general(s_prev, mit_ref[...], (((1,), (1,)), ((), ())), preferred_element_type=F32)
         + jnp.dot(u_all, ki_ref[...], preferred_element_type=F32))
    d = d_ref[...]
    for j in range(S5_L):
        yj = y[:, j * LANES:(j + 1) * LANES] + d * u_steps[j]
        y_ref[pl.ds(j, S5_NC, stride=S5_L), :] = _gelu_tanh(yj)
    o_ref[...] = y_ref[...].astype(o_ref.dtype)


def _s5(proj, a_re, a_im, b_re, b_im, c_re, c_im, log_dt, d_skip, o):
    ublock = (C_WIDTH + 2 * C_KV_WIDTH) // LANES
    flat = lambda t: t[o].astype(F32).reshape(S5_NB, 1, S5_SW)
    dt = jnp.repeat(jnp.exp(log_dt[o].astype(F32)), D_STATE).reshape(S5_NB, 1, S5_SW)
    rows_gc = lambda t: t.astype(F32).reshape(S5_NB, LANES, D_STATE)
    b_gc = lambda t: rows_gc(t[o].transpose(0, 2, 1))
    vec = pl.BlockSpec((None, 1, S5_SW), lambda n, b: (n, 0, 0))
    mat = pl.BlockSpec((None, LANES, D_STATE), lambda n, b: (n, 0, 0))
    return pl.pallas_call(
        _s5_kernel,
        out_shape=jax.ShapeDtypeStruct((TOKENS, D_WIDTH), BF16),
        grid=(S5_NB, BATCH),
        in_specs=[pl.BlockSpec((SEQ, LANES), lambda n, b: (b, ublock + n)),
                  vec, vec, vec, mat, mat, mat, mat,
                  pl.BlockSpec((None, 1, LANES), lambda n, b: (o, 0, n))],
        out_specs=pl.BlockSpec((SEQ, LANES), lambda n, b: (b, n)),
        scratch_shapes=[pltpu.VMEM((S5_CW, 2 * S5_SW), BF16),
                        pltpu.VMEM((S5_CW, S5_CW), BF16),
                        pltpu.VMEM((S5_CW, 2 * S5_SW), BF16),
                        pltpu.VMEM((8, SUBLANES, S5_SW), F32),
                        pltpu.VMEM((SEQ, LANES), F32),
                        pltpu.VMEM((S5_NC, S5_SW), F32), pltpu.VMEM((S5_NC, S5_SW), F32),
                        pltpu.VMEM((SEQ, LANES), F32)],
        compiler_params=_cparams(("parallel", "arbitrary")),
        name="s5_ssm",
    )(proj, flat(a_re), flat(a_im), dt, b_gc(b_re), b_gc(b_im), rows_gc(c_re[o]), rows_gc(c_im[o]),
      d_skip.reshape(-1, 1, D_WIDTH))


GLU_TN = 512


def _glu_kernel(z_ref, w_ref, b_ref, zc_ref, o_ref):
    gate = jax.nn.sigmoid(jnp.dot(z_ref[...], w_ref[...].astype(BF16), preferred_element_type=F32) + b_ref[...])
    o_ref[...] = (zc_ref[...].astype(F32) * gate).astype(o_ref.dtype)


def _glu(z, w, b, o):
    return pl.pallas_call(
        _glu_kernel,
        out_shape=jax.ShapeDtypeStruct((TOKENS, D_WIDTH), BF16),
        grid=(TOKENS // MM_TM, D_WIDTH // GLU_TN),
        in_specs=[pl.BlockSpec((MM_TM, D_WIDTH), lambda i, j: (i, 0)),
                  pl.BlockSpec((None, D_WIDTH, GLU_TN), lambda i, j: (o, 0, j)),
                  pl.BlockSpec((None, 1, GLU_TN), lambda i, j: (o, 0, j)),
                  pl.BlockSpec((MM_TM, GLU_TN), lambda i, j: (i, j))],
        out_specs=pl.BlockSpec((MM_TM, GLU_TN), lambda i, j: (i, j)),
        compiler_params=_cparams(("parallel", "parallel")),
        name="s5_glu",
    )(z, w, b.reshape(-1, 1, D_WIDTH), z)


DOWN_TM = 256


def kernel(x, c, positions, ada_w, ada_b, norm_mix, norm_ffn, norm_final, ev_w_in, ev_conv_w, ev_conv_b, ev_gate_a_w, ev_gate_a_b, ev_gate_x_w, ev_gate_x_b, ev_lambda, ev_w_out, od_w_in, od_sinks, od_a_re, od_a_im, od_b_re, od_b_im, od_c_re, od_c_im, od_d, od_log_dt, od_glu_w, od_glu_b, od_w_out, ffn_w_in, ffn_conv_w, ffn_conv_b, ffn_w_out):
    ffn_out16 = ffn_w_out.astype(BF16)
    ffn_cw = ffn_conv_w.astype(F32)
    ffn_cb = ffn_conv_b.astype(F32).reshape(DEPTH, 1, 2 * D_FF)

    mod = _ada_mod(c, ada_w, ada_b)
    mod = mod.reshape(DEPTH, SUBLANES, 6, 1, D_MODEL).transpose(0, 2, 1, 3, 4)
    cos_a, sin_a, cos_c, sin_cp, sin_cm = _rope_tables(positions)

    xt = x.reshape(TOKENS, D_MODEL).astype(F32)
    h = _prenorm(xt, norm_mix, mod, 0)
    for layer in range(DEPTH):
        idx = layer // 2
        if layer % 2 == 0:
            proj = _matmul(h, ev_w_in, idx, BF16, "even_in_proj")
            attn = _attn_a(proj, cos_a, sin_a)
            other = _lru(proj, ev_conv_w, ev_conv_b, ev_gate_a_w, ev_gate_a_b, ev_gate_x_w, ev_gate_x_b,
                         ev_lambda, idx)
            w_out = ev_w_out
        else:
            proj = _matmul(h, od_w_in, idx, BF16, "odd_in_proj")
            attn = _swa(proj, od_sinks, cos_c, sin_cp, sin_cm, idx)
            z = _s5(proj, od_a_re, od_a_im, od_b_re, od_b_im, od_c_re, od_c_im, od_log_dt, od_d, idx)
            other = _glu(z, od_glu_w, od_glu_b, idx)
            w_out = od_w_out
        xt, h2, tails = _mm_res([attn, other], w_out, idx, OUT_TM, xt, mod, layer, 2, norm_ffn, layer, layer, 3,
                                False, "mix_out_proj")
        act = _ffn_up(h2, tails, ffn_w_in, ffn_cw, ffn_cb, layer)
        if layer + 1 < DEPTH:
            xt, h, _ = _mm_res([act], ffn_out16, layer, DOWN_TM, xt, mod, layer, 5, norm_mix, layer + 1, layer + 1,
                               0, False, "ffn_down_proj")
        else:
            out = _mm_res([act], ffn_out16, layer, DOWN_TM, xt, mod, layer, 5, norm_final, 0, layer, 0, True,
                          "ffn_down_final")
    return out.reshape(BATCH, SEQ, D_MODEL).astype(x.dtype)
```

```python
import functools
import math

import jax
import jax.numpy as jnp
import numpy as np
from jax import lax
from jax.experimental import pallas as pl
from jax.experimental.pallas import tpu as pltpu

F32 = jnp.float32
BF16 = jnp.bfloat16

D_MODEL = 2048
BATCH = 4
SEQ = 2048
TOKENS = BATCH * SEQ
DEPTH = 4
ROPE_THETA = 10000.0
NORM_EPS = 1e-6
LANES = 128
SUBLANES = 8
BF16_ROWS = 16

A_HEAD_DIM = 128
A_HEADS = 8
A_WIDTH = 1024
A_PATTERNS = ((128, 1), (512, 4), (2048, 16))
B_WIDTH = 1024
B_BLOCKS = 8
B_CONV = 4
LRU_C = 8.0
EVEN_IN = 3 * A_WIDTH + 2 * B_WIDTH

C_HEAD_DIM = 64
C_HEADS = 16
C_KV_HEADS = 2
C_GROUP = 8
C_WIDTH = 1024
C_KV_WIDTH = 128
C_WINDOW = 128
D_WIDTH = 1024
D_GROUP_DIM = 16
D_GROUPS = 64
D_STATE = 64
ODD_IN = C_WIDTH + 2 * C_KV_WIDTH + D_WIDTH

D_FF = 5504
D_FF_PAD = 5632
FFN_CONV = 3

NEG = -1e30

VMEM_LIMIT = 56 * 1024 * 1024


def _cparams(sem, vmem=VMEM_LIMIT):
    return pltpu.CompilerParams(dimension_semantics=sem, vmem_limit_bytes=vmem)


GELU_C1 = 2.0 * math.sqrt(2.0 / math.pi)
GELU_C2 = 0.044715 * GELU_C1


def _gelu_tanh(x):
    z = x * (GELU_C1 + GELU_C2 * (x * x))
    return x / (1.0 + jnp.exp(-z))


ADA_TN = 1024


def _ada_kernel(c_ref, w_ref, b_ref, o_ref):
    c = c_ref[...]
    cond = (c * jax.nn.sigmoid(c)).astype(BF16)
    o_ref[...] = jnp.dot(cond, w_ref[...].astype(BF16), preferred_element_type=F32) + b_ref[...]


def _ada_mod(c, ada_w, ada_b):
    c8 = jnp.zeros((SUBLANES, D_MODEL), F32).at[:BATCH].set(c.astype(F32))
    n = 6 * D_MODEL
    return pl.pallas_call(
        _ada_kernel,
        out_shape=jax.ShapeDtypeStruct((DEPTH, SUBLANES, n), F32),
        grid=(DEPTH, n // ADA_TN),
        in_specs=[
            pl.BlockSpec((SUBLANES, D_MODEL), lambda l, j: (0, 0)),
            pl.BlockSpec((None, D_MODEL, ADA_TN), lambda l, j: (l, 0, j)),
            pl.BlockSpec((None, 1, ADA_TN), lambda l, j: (l, 0, j)),
        ],
        out_specs=pl.BlockSpec((None, SUBLANES, ADA_TN), lambda l, j: (l, 0, j)),
        compiler_params=_cparams(("parallel", "parallel")),
        name="ada_mod",
    )(c8, ada_w, ada_b.reshape(DEPTH, 1, n))


ROPE_TM = 1024


def _rope_kernel(pos_ref, inva_ref, invc_ref, ca_ref, sa_ref, cc_ref, scp_ref, scm_ref):
    pos = pos_ref[...].astype(F32)
    lane = lax.broadcasted_iota(jnp.int32, (ROPE_TM, LANES), 1)
    ang = pos * inva_ref[...]
    s = jnp.sin(ang)
    ca_ref[...] = jnp.cos(ang)
    sa_ref[...] = jnp.where(lane < A_HEAD_DIM // 2, -s, s)
    ang = pos * invc_ref[...]
    s = jnp.sin(ang)
    cc_ref[...] = jnp.cos(ang)
    second = (lane % C_HEAD_DIM) >= C_HEAD_DIM // 2
    scp_ref[...] = jnp.where(second, s, 0.0)
    scm_ref[...] = jnp.where(second, 0.0, -s)


def _rope_tables(positions):
    half_a, half_c = A_HEAD_DIM // 2, C_HEAD_DIM // 2
    inv_a = ROPE_THETA ** (-jnp.arange(half_a, dtype=F32) / half_a)
    inv_c = ROPE_THETA ** (-jnp.arange(half_c, dtype=F32) / half_c)
    inv_a = jnp.tile(inv_a, LANES // half_a).reshape(1, LANES)
    inv_c = jnp.tile(inv_c, LANES // half_c).reshape(1, LANES)
    tab = jax.ShapeDtypeStruct((TOKENS, LANES), F32)
    row = pl.BlockSpec((ROPE_TM, LANES), lambda i: (i, 0))
    vec = pl.BlockSpec((1, LANES), lambda i: (0, 0))
    return pl.pallas_call(
        _rope_kernel,
        out_shape=(tab,) * 5,
        grid=(TOKENS // ROPE_TM,),
        in_specs=[pl.BlockSpec((ROPE_TM, 1), lambda i: (i, 0)), vec, vec],
        out_specs=(row,) * 5,
        compiler_params=_cparams(("parallel",)),
        name="rope_tables",
    )(positions.reshape(TOKENS, 1), inv_a, inv_c)


def _norm_mod(x, g, sh, sc):
    ms = jnp.mean(x * x, axis=-1, keepdims=True)
    y = x * lax.rsqrt(ms + NORM_EPS) * g
    return y * (1.0 + sc) + sh


def _rmsnorm(x, g):
    ms = jnp.mean(x * x, axis=-1, keepdims=True)
    return x * lax.rsqrt(ms + NORM_EPS) * g


NORM_TM = 512


def _prenorm_kernel(x_ref, g_ref, sh_ref, sc_ref, h_ref):
    h_ref[...] = _norm_mod(x_ref[...], g_ref[...], sh_ref[...], sc_ref[...]).astype(BF16)


def _mod_spec(layer, chunk, tm):
    return pl.BlockSpec((None, None, None, 1, D_MODEL),
                        lambda i, *_: (layer, chunk, (i * tm) // SEQ, 0, 0))


def _prenorm(x, norm_g, mod, layer):
    vec = pl.BlockSpec((None, 1, D_MODEL), lambda i: (layer, 0, 0))
    return pl.pallas_call(
        _prenorm_kernel,
        out_shape=jax.ShapeDtypeStruct((TOKENS, D_MODEL), BF16),
        grid=(TOKENS // NORM_TM,),
        in_specs=[pl.BlockSpec((NORM_TM, D_MODEL), lambda i: (i, 0)), vec,
                  _mod_spec(layer, 0, NORM_TM), _mod_spec(layer, 1, NORM_TM)],
        out_specs=pl.BlockSpec((NORM_TM, D_MODEL), lambda i: (i, 0)),
        compiler_params=_cparams(("parallel",)),
        name="prenorm",
    )(x, norm_g.reshape(DEPTH, 1, D_MODEL), mod, mod)


MM_TM = 1024
MM_TN_CHOICES = (1024, 768, 512, 256)


def _mm_kernel(a_ref, w_ref, o_ref):
    o_ref[...] = jnp.dot(a_ref[...], w_ref[...].astype(BF16), preferred_element_type=F32).astype(o_ref.dtype)


def _matmul(a, w, idx, out_dtype, name):
    m, k = a.shape
    n = w.shape[-1]
    tn = next(t for t in MM_TN_CHOICES if n % t == 0)
    return pl.pallas_call(
        _mm_kernel,
        out_shape=jax.ShapeDtypeStruct((m, n), out_dtype),
        grid=(m // MM_TM, n // tn),
        in_specs=[pl.BlockSpec((MM_TM, k), lambda i, j: (i, 0)),
                  pl.BlockSpec((None, k, tn), lambda i, j: (idx, 0, j))],
        out_specs=pl.BlockSpec((MM_TM, tn), lambda i, j: (i, j)),
        compiler_params=_cparams(("parallel", "parallel")),
        name=name,
    )(a, w)


def _mm_res_kernel(*refs, n_lhs, final, cast_w):
    a_refs = refs[:n_lhs]
    w_ref, x_ref, gate_ref, g_ref, sh_ref, sc_ref = refs[n_lhs:n_lhs + 6]
    outs = refs[n_lhs + 6:]
    if cast_w:
        outs, w16_ref = outs[:-1], outs[-1]

        @pl.when(pl.program_id(0) == 0)
        def _():
            w16_ref[...] = w_ref[...].astype(BF16)

        w_ref = w16_ref
    y = None
    row0 = 0
    for a_ref in a_refs:
        kk = min(a_ref.shape[1], w_ref.shape[0] - row0)
        part = jnp.dot(a_ref[:, 0:kk], w_ref[row0:row0 + kk, :], preferred_element_type=F32)
        y = part if y is None else y + part
        row0 += kk
    xn = x_ref[...] + gate_ref[...] * y
    if final:
        outs[0][...] = _rmsnorm(xn, g_ref[...])
    else:
        outs[0][...] = xn
        outs[1][...] = _norm_mod(xn, g_ref[...], sh_ref[...], sc_ref[...]).astype(BF16)


def _mm_res(lhs, w, widx, tm, x, mod, gate_layer, gate_chunk, norm_g, norm_idx, mod_layer, mod_chunk, final, name):
    m = lhs[0].shape[0]
    kdim = w.shape[1]
    cast_w = w.dtype != BF16
    row = pl.BlockSpec((tm, D_MODEL), lambda i: (i, 0))
    if final:
        gvec = pl.BlockSpec((1, D_MODEL), lambda i: (0, 0))
        g_arr = norm_g.reshape(1, D_MODEL)
        out_shape = jax.ShapeDtypeStruct((m, D_MODEL), F32)
        out_specs = row
    else:
        gvec = pl.BlockSpec((None, 1, D_MODEL), lambda i: (norm_idx, 0, 0))
        g_arr = norm_g.reshape(DEPTH, 1, D_MODEL)
        out_shape = (jax.ShapeDtypeStruct((m, D_MODEL), F32), jax.ShapeDtypeStruct((m, D_MODEL), BF16))
        out_specs = (row, row)
    return pl.pallas_call(
        functools.partial(_mm_res_kernel, n_lhs=len(lhs), final=final, cast_w=cast_w),
        out_shape=out_shape,
        grid=(m // tm,),
        in_specs=[pl.BlockSpec((tm, a.shape[1]), lambda i: (i, 0)) for a in lhs] + [
                  pl.BlockSpec((None, kdim, D_MODEL), lambda i: (widx, 0, 0), pipeline_mode=pl.Buffered(1)),
                  row,
                  _mod_spec(gate_layer, gate_chunk, tm),
                  gvec,
                  _mod_spec(mod_layer, mod_chunk, tm),
                  _mod_spec(mod_layer, mod_chunk + 1, tm)],
        out_specs=out_specs,
        scratch_shapes=[pltpu.VMEM((kdim, D_MODEL), BF16)] if cast_w else [],
        compiler_params=_cparams(("arbitrary",)),
        name=name,
    )(*lhs, w, x, mod, g_arr, mod, mod)


FFN_TM = 1024
FFN_TF = 512
FFN_SUB = 256
FFN_NT = TOKENS // FFN_TM
FFN_NF = D_FF_PAD // FFN_TF
FFN_EDGE = D_FF_PAD - D_FF
FFN_TAIL = SUBLANES


def _ffn_up_kernel(h_ref, wg_ref, wv_win_ref, cwg_ref, cwv_win_ref, cbg_ref, cbv_win_ref,
                   o_ref, wg16_ref, wv_ref, cv_ref, tail_ref):
    j = pl.program_id(0)
    i = pl.program_id(1)
    keep = FFN_TF - FFN_EDGE

    @pl.when(i == 0)
    def _():
        wg16_ref[...] = wg_ref[...].astype(BF16)

    @pl.when((i == 0) & (j == FFN_NF - 1))
    def _():
        wv_ref[:, 0:keep] = wv_win_ref[:, FFN_EDGE:FFN_TF].astype(BF16)
        wv_ref[:, keep:FFN_TF] = wv_win_ref[:, 0:FFN_EDGE].astype(BF16)
        cv_ref[0:FFN_CONV, 0:keep] = cwv_win_ref[:, FFN_EDGE:FFN_TF]
        cv_ref[FFN_CONV:FFN_CONV + 1, 0:keep] = cbv_win_ref[:, FFN_EDGE:FFN_TF]
        cv_ref[:, keep:FFN_TF] = jnp.zeros((SUBLANES, FFN_EDGE), F32)

    @pl.when((i == 0) & (j != FFN_NF - 1))
    def _():
        wv_ref[...] = wv_win_ref[...].astype(BF16)
        cv_ref[0:FFN_CONV, :] = cwv_win_ref[...]
        cv_ref[FFN_CONV:FFN_CONV + 1, :] = cbv_win_ref[...]

    lhs = h_ref[...]
    starts_seq = (i * FFN_TM) % SEQ == 0

    def conv(u, k, sl, w, b):
        prev = jnp.where(starts_seq, 0.0, tail_ref[k, :, sl])
        tail_ref[k, :, sl] = u[FFN_TM - FFN_TAIL:, :]
        ext = jnp.concatenate([prev, u], axis=0)
        out = b + w[FFN_CONV - 1:FFN_CONV, :] * u
        for t in range(1, FFN_CONV):
            out = out + w[FFN_CONV - 1 - t:FFN_CONV - t, :] * pltpu.roll(ext, t, 0)[FFN_TAIL:, :]
        return out

    for c in range(FFN_TF // FFN_SUB):
        sl = slice(c * FFN_SUB, (c + 1) * FFN_SUB)
        ug = jnp.dot(lhs, wg16_ref[:, sl], preferred_element_type=F32)
        uv = jnp.dot(lhs, wv_ref[:, sl], preferred_element_type=F32)
        g = conv(ug, 0, sl, cwg_ref[:, sl], cbg_ref[:, sl])
        v = conv(uv, 1, sl, cv_ref[0:FFN_CONV, sl], cv_ref[FFN_CONV:FFN_CONV + 1, sl])
        o_ref[:, sl] = (_gelu_tanh(g) * v).astype(BF16)


def _ffn_up(h, w_in, conv_w, conv_b, layer):
    def voff(j):
        return LANES * jnp.minimum(D_FF // LANES + (FFN_TF // LANES) * j, (2 * D_FF - FFN_TF) // LANES)

    def win(rows):
        return pl.BlockSpec((None, pl.Element(rows), pl.Element(FFN_TF)), lambda j, i: (layer, 0, voff(j)))

    return pl.pallas_call(
        _ffn_up_kernel,
        out_shape=jax.ShapeDtypeStruct((TOKENS, D_FF_PAD), BF16),
        grid=(FFN_NF, FFN_NT),
        in_specs=[pl.BlockSpec((FFN_TM, D_MODEL), lambda j, i: (i, 0)),
                  pl.BlockSpec((None, D_MODEL, FFN_TF), lambda j, i: (layer, 0, j)),
                  win(D_MODEL),
                  pl.BlockSpec((None, FFN_CONV, FFN_TF), lambda j, i: (layer, 0, j)),
                  win(FFN_CONV),
                  pl.BlockSpec((None, 1, FFN_TF), lambda j, i: (layer, 0, j)),
                  win(1)],
        out_specs=pl.BlockSpec((FFN_TM, FFN_TF), lambda j, i: (i, j)),
        scratch_shapes=[pltpu.VMEM((D_MODEL, FFN_TF), BF16),
                        pltpu.VMEM((D_MODEL, FFN_TF), BF16),
                        pltpu.VMEM((SUBLANES, FFN_TF), F32),
                        pltpu.VMEM((2, FFN_TAIL, FFN_TF), F32)],
        compiler_params=_cparams(("arbitrary", "arbitrary")),
        name="ffn_up",
    )(h, w_in, w_in, conv_w, conv_w, conv_b, conv_b)


ATT_T = 256
ATT_NBIAS = 4


def _dilated_bias_tiles():
    tiles = np.zeros((ATT_NBIAS, ATT_T, ATT_T), np.float32)
    qi = np.arange(ATT_T)[:, None]
    kj = np.arange(ATT_T)[None, :]
    for off in range(ATT_NBIAS):
        delta = off * ATT_T + qi - kj
        count = np.zeros_like(delta)
        for window, dil in A_PATTERNS:
            count += ((delta >= 0) & (delta <= window) & (delta % dil == 0)).astype(delta.dtype)
        tiles[off] = np.where(count > 0, np.log(np.maximum(count, 1)), NEG)
    return tiles


def _attn_a_kernel(q_ref, k_ref, v_ref, cos_ref, sin_ref, bias_ref, o_ref, qs_ref, ks_ref, vs_ref):
    cos = cos_ref[...]
    sin = sin_ref[...]
    half = A_HEAD_DIM // 2
    q = q_ref[...].astype(F32)
    k = k_ref[...].astype(F32)
    scale = A_HEAD_DIM ** -0.5
    qs_ref[...] = ((q * cos + pltpu.roll(q, half, 1) * sin) * scale).astype(BF16)
    ks_ref[...] = (k * cos + pltpu.roll(k, half, 1) * sin).astype(BF16)
    vs_ref[...] = v_ref[...].astype(BF16)

    for i in range(SEQ // ATT_T):
        n = (i + 1) * ATT_T
        q_blk = qs_ref[i * ATT_T:n, :]
        s = lax.dot_general(q_blk, ks_ref[0:n, :], (((1,), (1,)), ((), ())), preferred_element_type=F32)
        s = s + jnp.concatenate([bias_ref[min(i - j, ATT_NBIAS - 1)] for j in range(i + 1)], axis=1)
        m = jnp.max(s, axis=-1, keepdims=True)
        p = jnp.exp(s - m)
        l = jnp.sum(p, axis=-1, keepdims=True)
        acc = jnp.dot(p.astype(BF16), vs_ref[0:n, :], preferred_element_type=F32)
        o_ref[i * ATT_T:n, :] = (acc / l).astype(o_ref.dtype)


def _attn_a(proj, cos_a, sin_a):
    bias = jnp.asarray(_dilated_bias_tiles())
    tab = pl.BlockSpec((SEQ, LANES), lambda b, h: (b, 0))
    return pl.pallas_call(
        _attn_a_kernel,
        out_shape=jax.ShapeDtypeStruct((TOKENS, A_WIDTH), BF16),
        grid=(BATCH, A_HEADS),
        in_specs=[pl.BlockSpec((SEQ, A_HEAD_DIM), lambda b, h: (b, h)),
                  pl.BlockSpec((SEQ, A_HEAD_DIM), lambda b, h: (b, A_HEADS + h)),
                  pl.BlockSpec((SEQ, A_HEAD_DIM), lambda b, h: (b, 2 * A_HEADS + h)),
                  tab, tab,
                  pl.BlockSpec((ATT_NBIAS, ATT_T, ATT_T), lambda b, h: (0, 0, 0))],
        out_specs=pl.BlockSpec((SEQ, A_HEAD_DIM), lambda b, h: (b, h)),
        scratch_shapes=[pltpu.VMEM((SEQ, A_HEAD_DIM), BF16)] * 3,
        compiler_params=_cparams(("parallel", "parallel")),
        name="attn_dilated",
    )(proj, proj, proj, cos_a, sin_a, bias)


LRU_TS = 512
LRU_HALO = SUBLANES


def _lru_kernel(xb_ref, yb_ref, cw_ref, cb_ref, ga_ref, gab_ref, gx_ref, gxb_ref, lam_ref, o_ref,
                ext_ref, a_ref, b_ref, carry_ref):
    t = pl.program_id(1)

    @pl.when(t == 0)
    def _():
        ext_ref[0:LRU_HALO, :] = jnp.zeros((LRU_HALO, B_WIDTH), F32)
        carry_ref[...] = jnp.zeros_like(carry_ref)

    ext_ref[LRU_HALO:, :] = xb_ref[...].astype(F32)
    ext = ext_ref[...]
    base = LRU_HALO - (B_CONV - 1)
    xc = cb_ref[...]
    for i in range(B_CONV):
        xc = xc + cw_ref[i:i + 1, :] * ext[base + i:base + i + LRU_TS, :]
    ext_ref[0:LRU_HALO, :] = ext[LRU_TS:LRU_TS + LRU_HALO, :]

    lam = lam_ref[...]
    neg_sp = -LRU_C * (jnp.maximum(-lam, 0.0) + jnp.log1p(jnp.exp(-jnp.abs(lam))))
    width = B_WIDTH // B_BLOCKS
    for blk in range(B_BLOCKS):
        sl = slice(blk * width, (blk + 1) * width)
        xh = xc[:, sl]
        xh16 = xh.astype(BF16)
        r = jax.nn.sigmoid(jnp.dot(xh16, ga_ref[blk].astype(BF16), preferred_element_type=F32) + gab_ref[:, sl])
        gi = jax.nn.sigmoid(jnp.dot(xh16, gx_ref[blk].astype(BF16), preferred_element_type=F32) + gxb_ref[:, sl])
        log_a = r * neg_sp[:, sl]
        a_ref[:, sl] = jnp.exp(log_a)
        th = jnp.tanh(log_a)
        b_ref[:, sl] = jnp.sqrt(-2.0 * th / (1.0 - th)) * (gi * xh)

    row = lax.broadcasted_iota(jnp.int32, (SUBLANES, B_WIDTH), 0)

    def scan_body(g, h_prev):
        rows = pl.ds(pl.multiple_of(g * SUBLANES, SUBLANES), SUBLANES)
        a = a_ref[rows, :]
        b = b_ref[rows, :]
        for s in (1, 2, 4):
            keep = row >= s
            a_sh = jnp.where(keep, pltpu.roll(a, s, 0), 1.0)
            b_sh = jnp.where(keep, pltpu.roll(b, s, 0), 0.0)
            b = a * b_sh + b
            a = a * a_sh
        h = a * h_prev + b
        b_ref[rows, :] = h
        return jnp.broadcast_to(h[SUBLANES - 1:SUBLANES, :], (SUBLANES, B_WIDTH))

    carry_ref[...] = lax.fori_loop(0, LRU_TS // SUBLANES, scan_body, carry_ref[...])
    o_ref[...] = (b_ref[...] * _gelu_tanh(yb_ref[...].astype(F32))).astype(o_ref.dtype)


def _lru(proj, conv_w, conv_b, ga_w, ga_b, gx_w, gx_b, lam, e):
    nts = SEQ // LRU_TS
    vec = pl.BlockSpec((None, 1, B_WIDTH), lambda b, t: (e, 0, 0))
    gate = pl.BlockSpec((None, B_BLOCKS, B_WIDTH // B_BLOCKS, B_WIDTH // B_BLOCKS), lambda b, t: (e, 0, 0, 0))
    r3 = lambda a: a.reshape(a.shape[0], 1, B_WIDTH)
    return pl.pallas_call(
        _lru_kernel,
        out_shape=jax.ShapeDtypeStruct((TOKENS, B_WIDTH), BF16),
        grid=(BATCH, nts),
        in_specs=[pl.BlockSpec((LRU_TS, B_WIDTH), lambda b, t: (b * nts + t, 3)),
                  pl.BlockSpec((LRU_TS, B_WIDTH), lambda b, t: (b * nts + t, 4)),
                  pl.BlockSpec((None, B_CONV, B_WIDTH), lambda b, t: (e, 0, 0)),
                  vec, gate, vec, gate, vec, vec],
        out_specs=pl.BlockSpec((LRU_TS, B_WIDTH), lambda b, t: (b * nts + t, 0)),
        scratch_shapes=[pltpu.VMEM((LRU_HALO + LRU_TS, B_WIDTH), F32),
                        pltpu.VMEM((LRU_TS, B_WIDTH), F32),
                        pltpu.VMEM((LRU_TS, B_WIDTH), F32),
                        pltpu.VMEM((SUBLANES, B_WIDTH), F32)],
        compiler_params=_cparams(("parallel", "arbitrary")),
        name="rg_lru",
    )(proj, proj, conv_w, r3(conv_b), ga_w, r3(ga_b), gx_w, r3(gx_b), r3(lam))


SWA_T = 128
SWA_PAIRS = C_GROUP // 2


def _swa_bias_tiles():
    qi = np.tile(np.arange(SWA_T), SWA_PAIRS)[:, None]
    kj = np.arange(2 * SWA_T)[None, :]
    delta = qi + SWA_T - kj
    band = (delta >= 0) & (delta <= C_WINDOW - 1)
    tiles = np.zeros((2, SWA_PAIRS * SWA_T, 2 * SWA_T), np.float32)
    tiles[0] = np.where(band & (kj >= SWA_T), 0.0, NEG)
    tiles[1] = np.where(band, 0.0, NEG)
    return tiles


def _swa_kernel(sink_ref, q_ref, kv_ref, cos_ref, sp_ref, sm_ref, bias_ref, o_ref,
                qs_ref, ka_ref, kb_ref, va_ref, vb_ref):
    kvh = pl.program_id(1)
    cos = cos_ref[...]
    s_plus = sp_ref[...]
    s_minus = sm_ref[...]
    quarter = C_HEAD_DIM // 2

    def rope(x):
        return x * cos + pltpu.roll(x, quarter, 1) * s_plus + pltpu.roll(x, LANES - quarter, 1) * s_minus

    scale = C_HEAD_DIM ** -0.5
    for j in range(SWA_PAIRS):
        sl = slice(j * LANES, (j + 1) * LANES)
        qs_ref[:, sl] = (rope(q_ref[:, sl].astype(F32)) * scale).astype(BF16)

    lane = lax.broadcasted_iota(jnp.int32, (SEQ, LANES), 1)
    low = lane < C_HEAD_DIM
    kk = rope(kv_ref[:, 0:LANES].astype(F32))
    vv = kv_ref[:, LANES:2 * LANES].astype(F32)
    kk = jnp.where(kvh == 0, kk, pltpu.roll(kk, C_HEAD_DIM, 1))
    vv = jnp.where(kvh == 0, vv, pltpu.roll(vv, C_HEAD_DIM, 1))
    k_lo = jnp.where(low, kk, 0.0)
    v_lo = jnp.where(low, vv, 0.0)
    zeros = jnp.zeros((SWA_T, LANES), BF16)
    for ref, val in ((ka_ref, k_lo), (kb_ref, pltpu.roll(k_lo, C_HEAD_DIM, 1)),
                     (va_ref, v_lo), (vb_ref, pltpu.roll(v_lo, C_HEAD_DIM, 1))):
        ref[0:SWA_T, :] = zeros
        ref[SWA_T:, :] = val.astype(BF16)

    rows_st = SWA_PAIRS * SWA_T
    pair = lax.broadcasted_iota(jnp.int32, (rows_st, 1), 0) // SWA_T
    sink_a = jnp.zeros((rows_st, 1), F32)
    sink_b = jnp.zeros((rows_st, 1), F32)
    for j in range(SWA_PAIRS):
        sink_a = jnp.where(pair == j, sink_ref[kvh * C_GROUP + 2 * j], sink_a)
        sink_b = jnp.where(pair == j, sink_ref[kvh * C_GROUP + 2 * j + 1], sink_b)

    def q_body(i, carry):
        r0 = pl.multiple_of(i * SWA_T, SWA_T)
        q_st = jnp.concatenate([qs_ref[pl.ds(r0, SWA_T), j * LANES:(j + 1) * LANES]
                                for j in range(SWA_PAIRS)], axis=0)
        bias = bias_ref[jnp.minimum(i, 1)]
        win = pl.ds(r0, 2 * SWA_T)
        out = jnp.zeros((rows_st, LANES), F32)
        for k_ref, v_ref, sink in ((ka_ref, va_ref, sink_a), (kb_ref, vb_ref, sink_b)):
            s = lax.dot_general(q_st, k_ref[win, :], (((1,), (1,)), ((), ())),
                                preferred_element_type=F32) + bias
            m = jnp.maximum(jnp.max(s, axis=-1, keepdims=True), sink)
            p = jnp.exp(s - m)
            den = jnp.sum(p, axis=-1, keepdims=True) + jnp.exp(sink - m)
            out = out + jnp.dot(p.astype(BF16), v_ref[win, :], preferred_element_type=F32) / den
        for j in range(SWA_PAIRS):
            o_ref[pl.ds(r0, SWA_T), j * LANES:(j + 1) * LANES] = out[j * SWA_T:(j + 1) * SWA_T, :].astype(o_ref.dtype)
        return carry

    lax.fori_loop(0, SEQ // SWA_T, q_body, 0)


def _swa(proj, sinks, cos_c, sin_cp, sin_cm, o):
    bias = jnp.asarray(_swa_bias_tiles())
    qw = C_WIDTH // C_KV_HEADS
    tab = pl.BlockSpec((SEQ, LANES), lambda b, g, *_: (b, 0))
    kv_block = C_WIDTH // (2 * C_KV_WIDTH)
    grid_spec = pltpu.PrefetchScalarGridSpec(
        num_scalar_prefetch=1,
        grid=(BATCH, C_KV_HEADS),
        in_specs=[pl.BlockSpec((SEQ, qw), lambda b, g, *_: (b, g)),
                  pl.BlockSpec((SEQ, 2 * C_KV_WIDTH), lambda b, g, *_: (b, kv_block)),
                  tab, tab, tab,
                  pl.BlockSpec((2, SWA_PAIRS * SWA_T, 2 * SWA_T), lambda b, g, *_: (0, 0, 0))],
        out_specs=pl.BlockSpec((SEQ, qw), lambda b, g, *_: (b, g)),
        scratch_shapes=[pltpu.VMEM((SEQ, qw), BF16)] + [pltpu.VMEM((SWA_T + SEQ, LANES), BF16)] * 4,
    )
    return pl.pallas_call(
        _swa_kernel,
        out_shape=jax.ShapeDtypeStruct((TOKENS, C_WIDTH), BF16),
        grid_spec=grid_spec,
        compiler_params=_cparams(("parallel", "parallel")),
        name="attn_swa",
    )(sinks[o].astype(F32), proj, proj, cos_c, sin_cp, sin_cm, bias)


S5_L = SUBLANES
S5_NC = SEQ // S5_L
S5_GPB = LANES // D_GROUP_DIM
S5_NB = D_WIDTH // LANES
S5_SW = S5_GPB * D_STATE
S5_CW = S5_L * LANES


def _s5_build_operators(are_ref, aim_ref, dt_ref, bre_ref, bim_ref, cre_ref, cim_ref,
                         wz_ref, ki_ref, mit_ref, dec_ref):
    lr, li, dt = are_ref[...], aim_ref[...], dt_ref[...]
    zr, zi = lr * dt, li * dt

    def a_pow(k):
        mag = jnp.exp(k * zr)
        return mag * jnp.cos(k * zi), mag * jnp.sin(k * zi)

    ar, ai = a_pow(1.0)
    den = lr * lr + li * li
    cr = ((ar - 1.0) * lr + ai * li) / den
    ci = (ai * lr - (ar - 1.0) * li) / den
    row = lax.broadcasted_iota(jnp.int32, (LANES, S5_SW), 0)
    col = lax.broadcasted_iota(jnp.int32, (LANES, S5_SW), 1)
    diag = (row // D_GROUP_DIM) == (col // D_STATE)

    def expand(x_ref):
        return jnp.where(diag, jnp.concatenate([x_ref[...]] * S5_GPB, axis=1), 0.0)

    b_r, b_i = expand(bre_ref), expand(bim_ref)
    bb_r = b_r * cr - b_i * ci
    bb_i = b_r * ci + b_i * cr
    c_r, c_i = expand(cre_ref), expand(cim_ref)

    for i in range(S5_L):
        pr, pi = a_pow(float(S5_L - 1 - i))
        rows = slice(i * LANES, (i + 1) * LANES)
        wz_ref[rows, 0:S5_SW] = (bb_r * pr - bb_i * pi).astype(BF16)
        wz_ref[rows, S5_SW:2 * S5_SW] = (bb_r * pi + bb_i * pr).astype(BF16)

    c_pow = []
    for k in range(S5_L + 1):
        pr, pi = a_pow(float(k))
        c_pow.append((c_r * pr - c_i * pi, c_r * pi + c_i * pr))
    for j in range(S5_L):
        rows = slice(j * LANES, (j + 1) * LANES)
        mit_ref[rows, 0:S5_SW] = c_pow[j + 1][0].astype(BF16)
        mit_ref[rows, S5_SW:2 * S5_SW] = (-c_pow[j + 1][1]).astype(BF16)

    nt = (((1,), (1,)), ((), ()))
    hi = lax.Precision.HIGHEST
    k_lag = []
    for t in range(S5_L):
        k_t = (lax.dot_general(bb_r, c_pow[t][0], nt, precision=hi, preferred_element_type=F32)
               - lax.dot_general(bb_i, c_pow[t][1], nt, precision=hi, preferred_element_type=F32))
        k_lag.append(k_t.astype(BF16))
    zero = jnp.zeros((LANES, LANES), BF16)
    for a in range(S5_L):
        for b in range(S5_L):
            ki_ref[a * LANES:(a + 1) * LANES, b * LANES:(b + 1) * LANES] = k_lag[b - a] if b >= a else zero

    rows8 = lax.broadcasted_iota(jnp.int32, (SUBLANES, S5_SW), 0)
    for idx, s in enumerate((1, 2, 4)):
        pr, pi = a_pow(float(S5_L * s))
        dec_ref[2 * idx] = jnp.where(rows8 >= s, pr, 0.0)
        dec_ref[2 * idx + 1] = jnp.where(rows8 >= s, pi, 0.0)
    kk = (S5_L * (rows8 + 1)).astype(F32)
    mag = jnp.exp(kk * zr)
    dec_ref[6] = mag * jnp.cos(kk * zi)
    dec_ref[7] = mag * jnp.sin(kk * zi)


def _s5_kernel(u_ref, are_ref, aim_ref, dt_ref, bre_ref, bim_ref, cre_ref, cim_ref, d_ref, o_ref,
               wz_ref, ki_ref, mit_ref, dec_ref, uf_ref, sre_ref, sim_ref, y_ref):
    @pl.when(pl.program_id(1) == 0)
    def _():
        _s5_build_operators(are_ref, aim_ref, dt_ref, bre_ref, bim_ref, cre_ref, cim_ref,
                            wz_ref, ki_ref, mit_ref, dec_ref)

    uf_ref[...] = u_ref[...].astype(F32)
    u_steps = [uf_ref[pl.ds(j, S5_NC, stride=S5_L), :] for j in range(S5_L)]
    u_all = jnp.concatenate(u_steps, axis=1).astype(BF16)

    z = jnp.dot(u_all, wz_ref[...], preferred_element_type=F32)
    ng = S5_NC // SUBLANES
    x_re = z[:, 0:S5_SW].reshape(ng, SUBLANES, S5_SW)
    x_im = z[:, S5_SW:2 * S5_SW].reshape(ng, SUBLANES, S5_SW)
    for idx, s in enumerate((1, 2, 4)):
        m_re = dec_ref[2 * idx]
        m_im = dec_ref[2 * idx + 1]
        r_re = pltpu.roll(x_re, s, 1)
        r_im = pltpu.roll(x_im, s, 1)
        x_re, x_im = x_re + (m_re * r_re - m_im * r_im), x_im + (m_re * r_im + m_im * r_re)
    sre_ref[...] = x_re.reshape(S5_NC, S5_SW)
    sim_ref[...] = x_im.reshape(S5_NC, S5_SW)
    p_re = dec_ref[6]
    p_im = dec_ref[7]

    def carry_body(g, c):
        c_re, c_im = c
        rows = pl.ds(pl.multiple_of(g * SUBLANES, SUBLANES), SUBLANES)
        s_re = sre_ref[rows, :] + (p_re * c_re - p_im * c_im)
        s_im = sim_ref[rows, :] + (p_re * c_im + p_im * c_re)
        sre_ref[rows, :] = s_re
        sim_ref[rows, :] = s_im
        last = slice(SUBLANES - 1, SUBLANES)
        return (jnp.broadcast_to(s_re[last, :], (SUBLANES, S5_SW)),
                jnp.broadcast_to(s_im[last, :], (SUBLANES, S5_SW)))

    zero = jnp.zeros((SUBLANES, S5_SW), F32)
    lax.fori_loop(0, ng, carry_body, (zero, zero))

    first = lax.broadcasted_iota(jnp.int32, (S5_NC, S5_SW), 0) == 0
    prev_re = jnp.where(first, 0.0, pltpu.roll(sre_ref[...], 1, 0))
    prev_im = jnp.where(first, 0.0, pltpu.roll(sim_ref[...], 1, 0))
    s_prev = jnp.concatenate([prev_re, prev_im], axis=1).astype(BF16)
    y = (lax.dot_general(s_prev, mit_ref[...], (((1,), (1,)), ((), ())), preferred_element_type=F32)
         + jnp.dot(u_all, ki_ref[...], preferred_element_type=F32))
    d = d_ref[...]
    for j in range(S5_L):
        yj = y[:, j * LANES:(j + 1) * LANES] + d * u_steps[j]
        y_ref[pl.ds(j, S5_NC, stride=S5_L), :] = _gelu_tanh(yj)
    o_ref[...] = y_ref[...].astype(o_ref.dtype)


def _s5(proj, a_re, a_im, b_re, b_im, c_re, c_im, log_dt, d_skip, o):
    ublock = (C_WIDTH + 2 * C_KV_WIDTH) // LANES
    flat = lambda t: t[o].astype(F32).reshape(S5_NB, 1, S5_SW)
    dt = jnp.repeat(jnp.exp(log_dt[o].astype(F32)), D_STATE).reshape(S5_NB, 1, S5_SW)
    rows_gc = lambda t: t.astype(F32).reshape(S5_NB, LANES, D_STATE)
    b_gc = lambda t: rows_gc(t[o].transpose(0, 2, 1))
    vec = pl.BlockSpec((None, 1, S5_SW), lambda n, b: (n, 0, 0))
    mat = pl.BlockSpec((None, LANES, D_STATE), lambda n, b: (n, 0, 0))
    return pl.pallas_call(
        _s5_kernel,
        out_shape=jax.ShapeDtypeStruct((TOKENS, D_WIDTH), BF16),
        grid=(S5_NB, BATCH),
        in_specs=[pl.BlockSpec((SEQ, LANES), lambda n, b: (b, ublock + n)),
                  vec, vec, vec, mat, mat, mat, mat,
                  pl.BlockSpec((None, 1, LANES), lambda n, b: (o, 0, n))],
        out_specs=pl.BlockSpec((SEQ, LANES), lambda n, b: (b, n)),
        scratch_shapes=[pltpu.VMEM((S5_CW, 2 * S5_SW), BF16),
                        pltpu.VMEM((S5_CW, S5_CW), BF16),
                        pltpu.VMEM((S5_CW, 2 * S5_SW), BF16),
                        pltpu.VMEM((8, SUBLANES, S5_SW), F32),
                        pltpu.VMEM((SEQ, LANES), F32),
                        pltpu.VMEM((S5_NC, S5_SW), F32), pltpu.VMEM((S5_NC, S5_SW), F32),
                        pltpu.VMEM((SEQ, LANES), F32)],
        compiler_params=_cparams(("parallel", "arbitrary")),
        name="s5_ssm",
    )(proj, flat(a_re), flat(a_im), dt, b_gc(b_re), b_gc(b_im), rows_gc(c_re[o]), rows_gc(c_im[o]),
      d_skip.reshape(-1, 1, D_WIDTH))


GLU_TN = 512


def _glu_kernel(z_ref, w_ref, b_ref, zc_ref, o_ref):
    gate = jax.nn.sigmoid(jnp.dot(z_ref[...], w_ref[...].astype(BF16), preferred_element_type=F32) + b_ref[...])
    o_ref[...] = (zc_ref[...].astype(F32) * gate).astype(o_ref.dtype)


def _glu(z, w, b, o):
    return pl.pallas_call(
        _glu_kernel,
        out_shape=jax.ShapeDtypeStruct((TOKENS, D_WIDTH), BF16),
        grid=(TOKENS // MM_TM, D_WIDTH // GLU_TN),
        in_specs=[pl.BlockSpec((MM_TM, D_WIDTH), lambda i, j: (i, 0)),
                  pl.BlockSpec((None, D_WIDTH, GLU_TN), lambda i, j: (o, 0, j)),
                  pl.BlockSpec((None, 1, GLU_TN), lambda i, j: (o, 0, j)),
                  pl.BlockSpec((MM_TM, GLU_TN), lambda i, j: (i, j))],
        out_specs=pl.BlockSpec((MM_TM, GLU_TN), lambda i, j: (i, j)),
        compiler_params=_cparams(("parallel", "parallel")),
        name="s5_glu",
    )(z, w, b.reshape(-1, 1, D_WIDTH), z)


OUT_TM = 512
DOWN_TM = 256


def kernel(x, c, positions, ada_w, ada_b, norm_mix, norm_ffn, norm_final, ev_w_in, ev_conv_w, ev_conv_b, ev_gate_a_w, ev_gate_a_b, ev_gate_x_w, ev_gate_x_b, ev_lambda, ev_w_out, od_w_in, od_sinks, od_a_re, od_a_im, od_b_re, od_b_im, od_c_re, od_c_im, od_d, od_log_dt, od_glu_w, od_glu_b, od_w_out, ffn_w_in, ffn_conv_w, ffn_conv_b, ffn_w_out):
    ffn_out16 = ffn_w_out.astype(BF16)
    ffn_cw = ffn_conv_w.astype(F32)
    ffn_cb = ffn_conv_b.astype(F32).reshape(DEPTH, 1, 2 * D_FF)

    mod = _ada_mod(c, ada_w, ada_b)
    mod = mod.reshape(DEPTH, SUBLANES, 6, 1, D_MODEL).transpose(0, 2, 1, 3, 4)
    cos_a, sin_a, cos_c, sin_cp, sin_cm = _rope_tables(positions)

    xt = x.reshape(TOKENS, D_MODEL).astype(F32)
    h = _prenorm(xt, norm_mix, mod, 0)
    for layer in range(DEPTH):
        idx = layer // 2
        if layer % 2 == 0:
            proj = _matmul(h, ev_w_in, idx, BF16, "even_in_proj")
            attn = _attn_a(proj, cos_a, sin_a)
            other = _lru(proj, ev_conv_w, ev_conv_b, ev_gate_a_w, ev_gate_a_b, ev_gate_x_w, ev_gate_x_b,
                         ev_lambda, idx)
            w_out = ev_w_out
        else:
            proj = _matmul(h, od_w_in, idx, BF16, "odd_in_proj")
            attn = _swa(proj, od_sinks, cos_c, sin_cp, sin_cm, idx)
            z = _s5(proj, od_a_re, od_a_im, od_b_re, od_b_im, od_c_re, od_c_im, od_log_dt, od_d, idx)
            other = _glu(z, od_glu_w, od_glu_b, idx)
            w_out = od_w_out
        xt, h2 = _mm_res([attn, other], w_out, idx, OUT_TM, xt, mod, layer, 2, norm_ffn, layer, layer, 3,
                         False, "mix_out_proj")
        act = _ffn_up(h2, ffn_w_in, ffn_cw, ffn_cb, layer)
        if layer + 1 < DEPTH:
            xt, h = _mm_res([act], ffn_out16, layer, DOWN_TM, xt, mod, layer, 5, norm_mix, layer + 1, layer + 1,
                            0, False, "ffn_down_proj")
        else:
            out = _mm_res([act], ffn_out16, layer, DOWN_TM, xt, mod, layer, 5, norm_final, 0, layer, 0, True,
                          "ffn_down_final")
    return out.reshape(BATCH, SEQ, D_MODEL).astype(x.dtype)
```

```python
import functools
import math

import jax
import jax.numpy as jnp
import numpy as np
from jax import lax
from jax.experimental import pallas as pl
from jax.experimental.pallas import tpu as pltpu

F32 = jnp.float32
BF16 = jnp.bfloat16

D_MODEL = 2048
BATCH = 4
SEQ = 2048
TOKENS = BATCH * SEQ
DEPTH = 4
ROPE_THETA = 10000.0
NORM_EPS = 1e-6
LANES = 128
SUBLANES = 8
BF16_ROWS = 16

A_HEAD_DIM = 128
A_HEADS = 8
A_WIDTH = 1024
A_PATTERNS = ((128, 1), (512, 4), (2048, 16))
B_WIDTH = 1024
B_BLOCKS = 8
B_CONV = 4
LRU_C = 8.0
EVEN_IN = 3 * A_WIDTH + 2 * B_WIDTH

C_HEAD_DIM = 64
C_HEADS = 16
C_KV_HEADS = 2
C_GROUP = 8
C_WIDTH = 1024
C_KV_WIDTH = 128
C_WINDOW = 128
D_WIDTH = 1024
D_GROUP_DIM = 16
D_GROUPS = 64
D_STATE = 64
ODD_IN = C_WIDTH + 2 * C_KV_WIDTH + D_WIDTH

D_FF = 5504
D_FF_PAD = 5632
FFN_CONV = 3

NEG = -1e30

VMEM_LIMIT = 56 * 1024 * 1024


def _cparams(sem, vmem=VMEM_LIMIT):
    return pltpu.CompilerParams(dimension_semantics=sem, vmem_limit_bytes=vmem)


GELU_C1 = 2.0 * math.sqrt(2.0 / math.pi)
GELU_C2 = 0.044715 * GELU_C1


def _gelu_tanh(x):
    z = x * (GELU_C1 + GELU_C2 * (x * x))
    return x / (1.0 + jnp.exp(-z))


ADA_TN = 1024


def _ada_kernel(c_ref, w_ref, b_ref, o_ref):
    c = c_ref[...]
    cond = (c * jax.nn.sigmoid(c)).astype(BF16)
    o_ref[...] = jnp.dot(cond, w_ref[...].astype(BF16), preferred_element_type=F32) + b_ref[...]


def _ada_mod(c, ada_w, ada_b):
    c8 = jnp.zeros((SUBLANES, D_MODEL), F32).at[:BATCH].set(c.astype(F32))
    n = 6 * D_MODEL
    return pl.pallas_call(
        _ada_kernel,
        out_shape=jax.ShapeDtypeStruct((DEPTH, SUBLANES, n), F32),
        grid=(DEPTH, n // ADA_TN),
        in_specs=[
            pl.BlockSpec((SUBLANES, D_MODEL), lambda l, j: (0, 0)),
            pl.BlockSpec((None, D_MODEL, ADA_TN), lambda l, j: (l, 0, j)),
            pl.BlockSpec((None, 1, ADA_TN), lambda l, j: (l, 0, j)),
        ],
        out_specs=pl.BlockSpec((None, SUBLANES, ADA_TN), lambda l, j: (l, 0, j)),
        compiler_params=_cparams(("parallel", "parallel")),
        name="ada_mod",
    )(c8, ada_w, ada_b.reshape(DEPTH, 1, n))


ROPE_TM = 1024


def _rope_kernel(pos_ref, inva_ref, invc_ref, ca_ref, sa_ref, cc_ref, scp_ref, scm_ref):
    pos = pos_ref[...].astype(F32)
    lane = lax.broadcasted_iota(jnp.int32, (ROPE_TM, LANES), 1)
    ang = pos * inva_ref[...]
    s = jnp.sin(ang)
    ca_ref[...] = jnp.cos(ang)
    sa_ref[...] = jnp.where(lane < A_HEAD_DIM // 2, -s, s)
    ang = pos * invc_ref[...]
    s = jnp.sin(ang)
    cc_ref[...] = jnp.cos(ang)
    second = (lane % C_HEAD_DIM) >= C_HEAD_DIM // 2
    scp_ref[...] = jnp.where(second, s, 0.0)
    scm_ref[...] = jnp.where(second, 0.0, -s)


def _rope_tables(positions):
    half_a, half_c = A_HEAD_DIM // 2, C_HEAD_DIM // 2
    inv_a = ROPE_THETA ** (-jnp.arange(half_a, dtype=F32) / half_a)
    inv_c = ROPE_THETA ** (-jnp.arange(half_c, dtype=F32) / half_c)
    inv_a = jnp.tile(inv_a, LANES // half_a).reshape(1, LANES)
    inv_c = jnp.tile(inv_c, LANES // half_c).reshape(1, LANES)
    tab = jax.ShapeDtypeStruct((TOKENS, LANES), F32)
    row = pl.BlockSpec((ROPE_TM, LANES), lambda i: (i, 0))
    vec = pl.BlockSpec((1, LANES), lambda i: (0, 0))
    return pl.pallas_call(
        _rope_kernel,
        out_shape=(tab,) * 5,
        grid=(TOKENS // ROPE_TM,),
        in_specs=[pl.BlockSpec((ROPE_TM, 1), lambda i: (i, 0)), vec, vec],
        out_specs=(row,) * 5,
        compiler_params=_cparams(("parallel",)),
        name="rope_tables",
    )(positions.reshape(TOKENS, 1), inv_a, inv_c)


def _norm_mod(x, g, sh, sc):
    ms = jnp.mean(x * x, axis=-1, keepdims=True)
    y = x * lax.rsqrt(ms + NORM_EPS) * g
    return y * (1.0 + sc) + sh


def _rmsnorm(x, g):
    ms = jnp.mean(x * x, axis=-1, keepdims=True)
    return x * lax.rsqrt(ms + NORM_EPS) * g


NORM_TM = 512


def _prenorm_kernel(x_ref, g_ref, sh_ref, sc_ref, h_ref):
    h_ref[...] = _norm_mod(x_ref[...], g_ref[...], sh_ref[...], sc_ref[...]).astype(BF16)


def _mod_spec(layer, chunk, tm):
    return pl.BlockSpec((None, None, None, 1, D_MODEL),
                        lambda i, *_: (layer, chunk, (i * tm) // SEQ, 0, 0))


def _prenorm(x, norm_g, mod, layer):
    vec = pl.BlockSpec((None, 1, D_MODEL), lambda i: (layer, 0, 0))
    return pl.pallas_call(
        _prenorm_kernel,
        out_shape=jax.ShapeDtypeStruct((TOKENS, D_MODEL), BF16),
        grid=(TOKENS // NORM_TM,),
        in_specs=[pl.BlockSpec((NORM_TM, D_MODEL), lambda i: (i, 0)), vec,
                  _mod_spec(layer, 0, NORM_TM), _mod_spec(layer, 1, NORM_TM)],
        out_specs=pl.BlockSpec((NORM_TM, D_MODEL), lambda i: (i, 0)),
        compiler_params=_cparams(("parallel",)),
        name="prenorm",
    )(x, norm_g.reshape(DEPTH, 1, D_MODEL), mod, mod)


MM_TM = 1024
MM_TN_CHOICES = (1024, 768, 512, 256)


def _mm_kernel(a_ref, w_ref, o_ref):
    o_ref[...] = jnp.dot(a_ref[...], w_ref[...].astype(BF16), preferred_element_type=F32).astype(o_ref.dtype)


def _matmul(a, w, idx, out_dtype, name):
    m, k = a.shape
    n = w.shape[-1]
    tn = next(t for t in MM_TN_CHOICES if n % t == 0)
    return pl.pallas_call(
        _mm_kernel,
        out_shape=jax.ShapeDtypeStruct((m, n), out_dtype),
        grid=(m // MM_TM, n // tn),
        in_specs=[pl.BlockSpec((MM_TM, k), lambda i, j: (i, 0)),
                  pl.BlockSpec((None, k, tn), lambda i, j: (idx, 0, j))],
        out_specs=pl.BlockSpec((MM_TM, tn), lambda i, j: (i, j)),
        compiler_params=_cparams(("parallel", "parallel")),
        name=name,
    )(a, w)


def _mm_res_kernel(*refs, n_lhs, final, cast_w):
    a_refs = refs[:n_lhs]
    w_ref, x_ref, gate_ref, g_ref, sh_ref, sc_ref = refs[n_lhs:n_lhs + 6]
    outs = refs[n_lhs + 6:]
    if cast_w:
        outs, w16_ref = outs[:-1], outs[-1]

        @pl.when(pl.program_id(0) == 0)
        def _():
            w16_ref[...] = w_ref[...].astype(BF16)

        w_ref = w16_ref
    y = None
    row0 = 0
    for a_ref in a_refs:
        kk = min(a_ref.shape[1], w_ref.shape[0] - row0)
        part = jnp.dot(a_ref[:, 0:kk], w_ref[row0:row0 + kk, :], preferred_element_type=F32)
        y = part if y is None else y + part
        row0 += kk
    xn = x_ref[...] + gate_ref[...] * y
    if final:
        outs[0][...] = _rmsnorm(xn, g_ref[...])
    else:
        outs[0][...] = xn
        outs[1][...] = _norm_mod(xn, g_ref[...], sh_ref[...], sc_ref[...]).astype(BF16)


def _mm_res(lhs, w, widx, tm, x, mod, gate_layer, gate_chunk, norm_g, norm_idx, mod_layer, mod_chunk, final, name):
    m = lhs[0].shape[0]
    kdim = w.shape[1]
    cast_w = w.dtype != BF16
    row = pl.BlockSpec((tm, D_MODEL), lambda i: (i, 0))
    if final:
        gvec = pl.BlockSpec((1, D_MODEL), lambda i: (0, 0))
        g_arr = norm_g.reshape(1, D_MODEL)
        out_shape = jax.ShapeDtypeStruct((m, D_MODEL), F32)
        out_specs = row
    else:
        gvec = pl.BlockSpec((None, 1, D_MODEL), lambda i: (norm_idx, 0, 0))
        g_arr = norm_g.reshape(DEPTH, 1, D_MODEL)
        out_shape = (jax.ShapeDtypeStruct((m, D_MODEL), F32), jax.ShapeDtypeStruct((m, D_MODEL), BF16))
        out_specs = (row, row)
    return pl.pallas_call(
        functools.partial(_mm_res_kernel, n_lhs=len(lhs), final=final, cast_w=cast_w),
        out_shape=out_shape,
        grid=(m // tm,),
        in_specs=[pl.BlockSpec((tm, a.shape[1]), lambda i: (i, 0)) for a in lhs] + [
                  pl.BlockSpec((None, kdim, D_MODEL), lambda i: (widx, 0, 0), pipeline_mode=pl.Buffered(1)),
                  row,
                  _mod_spec(gate_layer, gate_chunk, tm),
                  gvec,
                  _mod_spec(mod_layer, mod_chunk, tm),
                  _mod_spec(mod_layer, mod_chunk + 1, tm)],
        out_specs=out_specs,
        scratch_shapes=[pltpu.VMEM((kdim, D_MODEL), BF16)] if cast_w else [],
        compiler_params=_cparams(("arbitrary",)),
        name=name,
    )(*lhs, w, x, mod, g_arr, mod, mod)


FFN_TM = 1024
FFN_TF = 512
FFN_NT = TOKENS // FFN_TM
FFN_NF = D_FF_PAD // FFN_TF
FFN_EDGE = D_FF_PAD - D_FF
FFN_TAIL = SUBLANES


def _ffn_up_kernel(h_ref, wg_ref, wv_win_ref, cwg_ref, cwv_win_ref, cbg_ref, cbv_win_ref,
                   o_ref, w16_ref, cv_ref, tail_ref):
    j = pl.program_id(0)
    i = pl.program_id(1)
    keep = FFN_TF - FFN_EDGE
    wg16_ref = w16_ref.at[:, 0:FFN_TF]
    wv_ref = w16_ref.at[:, FFN_TF:2 * FFN_TF]

    @pl.when(i == 0)
    def _():
        wg16_ref[...] = wg_ref[...].astype(BF16)

    @pl.when((i == 0) & (j == FFN_NF - 1))
    def _():
        wv_ref[:, 0:keep] = wv_win_ref[:, FFN_EDGE:FFN_TF].astype(BF16)
        wv_ref[:, keep:FFN_TF] = wv_win_ref[:, 0:FFN_EDGE].astype(BF16)
        cv_ref[0:FFN_CONV, 0:keep] = cwv_win_ref[:, FFN_EDGE:FFN_TF]
        cv_ref[FFN_CONV:FFN_CONV + 1, 0:keep] = cbv_win_ref[:, FFN_EDGE:FFN_TF]
        cv_ref[:, keep:FFN_TF] = jnp.zeros((SUBLANES, FFN_EDGE), F32)

    @pl.when((i == 0) & (j != FFN_NF - 1))
    def _():
        wv_ref[...] = wv_win_ref[...].astype(BF16)
        cv_ref[0:FFN_CONV, :] = cwv_win_ref[...]
        cv_ref[FFN_CONV:FFN_CONV + 1, :] = cbv_win_ref[...]

    lhs = h_ref[...]
    starts_seq = (i * FFN_TM) % SEQ == 0

    def conv(u, k, sl, w, b):
        prev = jnp.where(starts_seq, 0.0, tail_ref[k, :, sl])
        tail_ref[k, :, sl] = u[FFN_TM - FFN_TAIL:, :]
        ext = jnp.concatenate([prev, u], axis=0)
        out = b + w[FFN_CONV - 1:FFN_CONV, :] * u
        for t in range(1, FFN_CONV):
            out = out + w[FFN_CONV - 1 - t:FFN_CONV - t, :] * pltpu.roll(ext, t, 0)[FFN_TAIL:, :]
        return out

    u = jnp.dot(lhs, w16_ref[...], preferred_element_type=F32)
    sl = slice(0, FFN_TF)
    g = conv(u[:, 0:FFN_TF], 0, sl, cwg_ref[...], cbg_ref[...])
    v = conv(u[:, FFN_TF:2 * FFN_TF], 1, sl, cv_ref[0:FFN_CONV, :], cv_ref[FFN_CONV:FFN_CONV + 1, :])
    o_ref[...] = (_gelu_tanh(g) * v).astype(BF16)


def _ffn_up(h, w_in, conv_w, conv_b, layer):
    def voff(j):
        return LANES * jnp.minimum(D_FF // LANES + (FFN_TF // LANES) * j, (2 * D_FF - FFN_TF) // LANES)

    def win(rows):
        return pl.BlockSpec((None, pl.Element(rows), pl.Element(FFN_TF)), lambda j, i: (layer, 0, voff(j)))

    return pl.pallas_call(
        _ffn_up_kernel,
        out_shape=jax.ShapeDtypeStruct((TOKENS, D_FF_PAD), BF16),
        grid=(FFN_NF, FFN_NT),
        in_specs=[pl.BlockSpec((FFN_TM, D_MODEL), lambda j, i: (i, 0)),
                  pl.BlockSpec((None, D_MODEL, FFN_TF), lambda j, i: (layer, 0, j)),
                  win(D_MODEL),
                  pl.BlockSpec((None, FFN_CONV, FFN_TF), lambda j, i: (layer, 0, j)),
                  win(FFN_CONV),
                  pl.BlockSpec((None, 1, FFN_TF), lambda j, i: (layer, 0, j)),
                  win(1)],
        out_specs=pl.BlockSpec((FFN_TM, FFN_TF), lambda j, i: (i, j)),
        scratch_shapes=[pltpu.VMEM((D_MODEL, 2 * FFN_TF), BF16),
                        pltpu.VMEM((SUBLANES, FFN_TF), F32),
                        pltpu.VMEM((2, FFN_TAIL, FFN_TF), F32)],
        compiler_params=_cparams(("arbitrary", "arbitrary")),
        name="ffn_up",
    )(h, w_in, w_in, conv_w, conv_w, conv_b, conv_b)


ATT_T = 256
ATT_NBIAS = 4


def _dilated_bias_tiles():
    tiles = np.zeros((ATT_NBIAS, ATT_T, ATT_T), np.float32)
    qi = np.arange(ATT_T)[:, None]
    kj = np.arange(ATT_T)[None, :]
    for off in range(ATT_NBIAS):
        delta = off * ATT_T + qi - kj
        count = np.zeros_like(delta)
        for window, dil in A_PATTERNS:
            count += ((delta >= 0) & (delta <= window) & (delta % dil == 0)).astype(delta.dtype)
        tiles[off] = np.where(count > 0, np.log(np.maximum(count, 1)), NEG)
    return tiles


def _attn_a_kernel(q_ref, k_ref, v_ref, cos_ref, sin_ref, bias_ref, o_ref, qs_ref, ks_ref, vs_ref):
    cos = cos_ref[...]
    sin = sin_ref[...]
    half = A_HEAD_DIM // 2
    q = q_ref[...].astype(F32)
    k = k_ref[...].astype(F32)
    scale = A_HEAD_DIM ** -0.5
    qs_ref[...] = ((q * cos + pltpu.roll(q, half, 1) * sin) * scale).astype(BF16)
    ks_ref[...] = (k * cos + pltpu.roll(k, half, 1) * sin).astype(BF16)
    vs_ref[...] = v_ref[...].astype(BF16)

    for i in range(SEQ // ATT_T):
        n = (i + 1) * ATT_T
        q_blk = qs_ref[i * ATT_T:n, :]
        s = lax.dot_general(q_blk, ks_ref[0:n, :], (((1,), (1,)), ((), ())), preferred_element_type=F32)
        s = s + jnp.concatenate([bias_ref[min(i - j, ATT_NBIAS - 1)] for j in range(i + 1)], axis=1)
        m = jnp.max(s, axis=-1, keepdims=True)
        p = jnp.exp(s - m)
        l = jnp.sum(p, axis=-1, keepdims=True)
        acc = jnp.dot(p.astype(BF16), vs_ref[0:n, :], preferred_element_type=F32)
        o_ref[i * ATT_T:n, :] = (acc / l).astype(o_ref.dtype)


def _attn_a(proj, cos_a, sin_a):
    bias = jnp.asarray(_dilated_bias_tiles())
    tab = pl.BlockSpec((SEQ, LANES), lambda b, h: (b, 0))
    return pl.pallas_call(
        _attn_a_kernel,
        out_shape=jax.ShapeDtypeStruct((TOKENS, A_WIDTH), BF16),
        grid=(BATCH, A_HEADS),
        in_specs=[pl.BlockSpec((SEQ, A_HEAD_DIM), lambda b, h: (b, h)),
                  pl.BlockSpec((SEQ, A_HEAD_DIM), lambda b, h: (b, A_HEADS + h)),
                  pl.BlockSpec((SEQ, A_HEAD_DIM), lambda b, h: (b, 2 * A_HEADS + h)),
                  tab, tab,
                  pl.BlockSpec((ATT_NBIAS, ATT_T, ATT_T), lambda b, h: (0, 0, 0))],
        out_specs=pl.BlockSpec((SEQ, A_HEAD_DIM), lambda b, h: (b, h)),
        scratch_shapes=[pltpu.VMEM((SEQ, A_HEAD_DIM), BF16)] * 3,
        compiler_params=_cparams(("parallel", "parallel")),
        name="attn_dilated",
    )(proj, proj, proj, cos_a, sin_a, bias)


LRU_TS = 512
LRU_HALO = SUBLANES


def _lru_kernel(xb_ref, yb_ref, cw_ref, cb_ref, ga_ref, gab_ref, gx_ref, gxb_ref, lam_ref, o_ref,
                ext_ref, a_ref, b_ref, carry_ref):
    t = pl.program_id(1)

    @pl.when(t == 0)
    def _():
        ext_ref[0:LRU_HALO, :] = jnp.zeros((LRU_HALO, B_WIDTH), F32)
        carry_ref[...] = jnp.zeros_like(carry_ref)

    ext_ref[LRU_HALO:, :] = xb_ref[...].astype(F32)
    ext = ext_ref[...]
    base = LRU_HALO - (B_CONV - 1)
    xc = cb_ref[...]
    for i in range(B_CONV):
        xc = xc + cw_ref[i:i + 1, :] * ext[base + i:base + i + LRU_TS, :]
    ext_ref[0:LRU_HALO, :] = ext[LRU_TS:LRU_TS + LRU_HALO, :]

    lam = lam_ref[...]
    neg_sp = -LRU_C * (jnp.maximum(-lam, 0.0) + jnp.log1p(jnp.exp(-jnp.abs(lam))))
    width = B_WIDTH // B_BLOCKS
    for blk in range(B_BLOCKS):
        sl = slice(blk * width, (blk + 1) * width)
        xh = xc[:, sl]
        xh16 = xh.astype(BF16)
        r = jax.nn.sigmoid(jnp.dot(xh16, ga_ref[blk].astype(BF16), preferred_element_type=F32) + gab_ref[:, sl])
        gi = jax.nn.sigmoid(jnp.dot(xh16, gx_ref[blk].astype(BF16), preferred_element_type=F32) + gxb_ref[:, sl])
        log_a = r * neg_sp[:, sl]
        a_ref[:, sl] = jnp.exp(log_a)
        th = jnp.tanh(log_a)
        b_ref[:, sl] = jnp.sqrt(-2.0 * th / (1.0 - th)) * (gi * xh)

    row = lax.broadcasted_iota(jnp.int32, (SUBLANES, B_WIDTH), 0)

    def scan_body(g, h_prev):
        rows = pl.ds(pl.multiple_of(g * SUBLANES, SUBLANES), SUBLANES)
        a = a_ref[rows, :]
        b = b_ref[rows, :]
        for s in (1, 2, 4):
            keep = row >= s
            a_sh = jnp.where(keep, pltpu.roll(a, s, 0), 1.0)
            b_sh = jnp.where(keep, pltpu.roll(b, s, 0), 0.0)
            b = a * b_sh + b
            a = a * a_sh
        h = a * h_prev + b
        b_ref[rows, :] = h
        return jnp.broadcast_to(h[SUBLANES - 1:SUBLANES, :], (SUBLANES, B_WIDTH))

    carry_ref[...] = lax.fori_loop(0, LRU_TS // SUBLANES, scan_body, carry_ref[...])
    o_ref[...] = (b_ref[...] * _gelu_tanh(yb_ref[...].astype(F32))).astype(o_ref.dtype)


def _lru(proj, conv_w, conv_b, ga_w, ga_b, gx_w, gx_b, lam, e):
    nts = SEQ // LRU_TS
    vec = pl.BlockSpec((None, 1, B_WIDTH), lambda b, t: (e, 0, 0))
    gate = pl.BlockSpec((None, B_BLOCKS, B_WIDTH // B_BLOCKS, B_WIDTH // B_BLOCKS), lambda b, t: (e, 0, 0, 0))
    r3 = lambda a: a.reshape(a.shape[0], 1, B_WIDTH)
    return pl.pallas_call(
        _lru_kernel,
        out_shape=jax.ShapeDtypeStruct((TOKENS, B_WIDTH), BF16),
        grid=(BATCH, nts),
        in_specs=[pl.BlockSpec((LRU_TS, B_WIDTH), lambda b, t: (b * nts + t, 3)),
                  pl.BlockSpec((LRU_TS, B_WIDTH), lambda b, t: (b * nts + t, 4)),
                  pl.BlockSpec((None, B_CONV, B_WIDTH), lambda b, t: (e, 0, 0)),
                  vec, gate, vec, gate, vec, vec],
        out_specs=pl.BlockSpec((LRU_TS, B_WIDTH), lambda b, t: (b * nts + t, 0)),
        scratch_shapes=[pltpu.VMEM((LRU_HALO + LRU_TS, B_WIDTH), F32),
                        pltpu.VMEM((LRU_TS, B_WIDTH), F32),
                        pltpu.VMEM((LRU_TS, B_WIDTH), F32),
                        pltpu.VMEM((SUBLANES, B_WIDTH), F32)],
        compiler_params=_cparams(("parallel", "arbitrary")),
        name="rg_lru",
    )(proj, proj, conv_w, r3(conv_b), ga_w, r3(ga_b), gx_w, r3(gx_b), r3(lam))


SWA_T = 128
SWA_PAIRS = C_GROUP // 2


def _swa_bias_tiles():
    qi = np.tile(np.arange(SWA_T), SWA_PAIRS)[:, None]
    kj = np.arange(2 * SWA_T)[None, :]
    delta = qi + SWA_T - kj
    band = (delta >= 0) & (delta <= C_WINDOW - 1)
    tiles = np.zeros((2, SWA_PAIRS * SWA_T, 2 * SWA_T), np.float32)
    tiles[0] = np.where(band & (kj >= SWA_T), 0.0, NEG)
    tiles[1] = np.where(band, 0.0, NEG)
    return tiles


def _swa_kernel(sink_ref, q_ref, kv_ref, cos_ref, sp_ref, sm_ref, bias_ref, o_ref,
                qs_ref, ka_ref, kb_ref, va_ref, vb_ref):
    kvh = pl.program_id(1)
    cos = cos_ref[...]
    s_plus = sp_ref[...]
    s_minus = sm_ref[...]
    quarter = C_HEAD_DIM // 2

    def rope(x):
        return x * cos + pltpu.roll(x, quarter, 1) * s_plus + pltpu.roll(x, LANES - quarter, 1) * s_minus

    scale = C_HEAD_DIM ** -0.5
    for j in range(SWA_PAIRS):
        sl = slice(j * LANES, (j + 1) * LANES)
        qs_ref[:, sl] = (rope(q_ref[:, sl].astype(F32)) * scale).astype(BF16)

    lane = lax.broadcasted_iota(jnp.int32, (SEQ, LANES), 1)
    low = lane < C_HEAD_DIM
    kk = rope(kv_ref[:, 0:LANES].astype(F32))
    vv = kv_ref[:, LANES:2 * LANES].astype(F32)
    kk = jnp.where(kvh == 0, kk, pltpu.roll(kk, C_HEAD_DIM, 1))
    vv = jnp.where(kvh == 0, vv, pltpu.roll(vv, C_HEAD_DIM, 1))
    k_lo = jnp.where(low, kk, 0.0)
    v_lo = jnp.where(low, vv, 0.0)
    zeros = jnp.zeros((SWA_T, LANES), BF16)
    for ref, val in ((ka_ref, k_lo), (kb_ref, pltpu.roll(k_lo, C_HEAD_DIM, 1)),
                     (va_ref, v_lo), (vb_ref, pltpu.roll(v_lo, C_HEAD_DIM, 1))):
        ref[0:SWA_T, :] = zeros
        ref[SWA_T:, :] = val.astype(BF16)

    rows_st = SWA_PAIRS * SWA_T
    pair = lax.broadcasted_iota(jnp.int32, (rows_st, 1), 0) // SWA_T
    sink_a = jnp.zeros((rows_st, 1), F32)
    sink_b = jnp.zeros((rows_st, 1), F32)
    for j in range(SWA_PAIRS):
        sink_a = jnp.where(pair == j, sink_ref[kvh * C_GROUP + 2 * j], sink_a)
        sink_b = jnp.where(pair == j, sink_ref[kvh * C_GROUP + 2 * j + 1], sink_b)

    def q_body(i, carry):
        r0 = pl.multiple_of(i * SWA_T, SWA_T)
        q_st = jnp.concatenate([qs_ref[pl.ds(r0, SWA_T), j * LANES:(j + 1) * LANES]
                                for j in range(SWA_PAIRS)], axis=0)
        bias = bias_ref[jnp.minimum(i, 1)]
        win = pl.ds(r0, 2 * SWA_T)
        out = jnp.zeros((rows_st, LANES), F32)
        for k_ref, v_ref, sink in ((ka_ref, va_ref, sink_a), (kb_ref, vb_ref, sink_b)):
            s = lax.dot_general(q_st, k_ref[win, :], (((1,), (1,)), ((), ())),
                                preferred_element_type=F32) + bias
            m = jnp.maximum(jnp.max(s, axis=-1, keepdims=True), sink)
            p = jnp.exp(s - m)
            den = jnp.sum(p, axis=-1, keepdims=True) + jnp.exp(sink - m)
            out = out + jnp.dot(p.astype(BF16), v_ref[win, :], preferred_element_type=F32) / den
        for j in range(SWA_PAIRS):
            o_ref[pl.ds(r0, SWA_T), j * LANES:(j + 1) * LANES] = out[j * SWA_T:(j + 1) * SWA_T, :].astype(o_ref.dtype)
        return carry

    lax.fori_loop(0, SEQ // SWA_T, q_body, 0)


def _swa(proj, sinks, cos_c, sin_cp, sin_cm, o):
    bias = jnp.asarray(_swa_bias_tiles())
    qw = C_WIDTH // C_KV_HEADS
    tab = pl.BlockSpec((SEQ, LANES), lambda b, g, *_: (b, 0))
    kv_block = C_WIDTH // (2 * C_KV_WIDTH)
    grid_spec = pltpu.PrefetchScalarGridSpec(
        num_scalar_prefetch=1,
        grid=(BATCH, C_KV_HEADS),
        in_specs=[pl.BlockSpec((SEQ, qw), lambda b, g, *_: (b, g)),
                  pl.BlockSpec((SEQ, 2 * C_KV_WIDTH), lambda b, g, *_: (b, kv_block)),
                  tab, tab, tab,
                  pl.BlockSpec((2, SWA_PAIRS * SWA_T, 2 * SWA_T), lambda b, g, *_: (0, 0, 0))],
        out_specs=pl.BlockSpec((SEQ, qw), lambda b, g, *_: (b, g)),
        scratch_shapes=[pltpu.VMEM((SEQ, qw), BF16)] + [pltpu.VMEM((SWA_T + SEQ, LANES), BF16)] * 4,
    )
    return pl.pallas_call(
        _swa_kernel,
        out_shape=jax.ShapeDtypeStruct((TOKENS, C_WIDTH), BF16),
        grid_spec=grid_spec,
        compiler_params=_cparams(("parallel", "parallel")),
        name="attn_swa",
    )(sinks[o].astype(F32), proj, proj, cos_c, sin_cp, sin_cm, bias)


S5_L = SUBLANES
S5_NC = SEQ // S5_L
S5_GPB = LANES // D_GROUP_DIM
S5_NB = D_WIDTH // LANES
S5_SW = S5_GPB * D_STATE
S5_CW = S5_L * LANES


def _s5_build_operators(are_ref, aim_ref, dt_ref, bre_ref, bim_ref, cre_ref, cim_ref,
                         wz_ref, ki_ref, mit_ref, dec_ref):
    lr, li, dt = are_ref[...], aim_ref[...], dt_ref[...]
    zr, zi = lr * dt, li * dt

    def a_pow(k):
        mag = jnp.exp(k * zr)
        return mag * jnp.cos(k * zi), mag * jnp.sin(k * zi)

    ar, ai = a_pow(1.0)
    den = lr * lr + li * li
    cr = ((ar - 1.0) * lr + ai * li) / den
    ci = (ai * lr - (ar - 1.0) * li) / den
    row = lax.broadcasted_iota(jnp.int32, (LANES, S5_SW), 0)
    col = lax.broadcasted_iota(jnp.int32, (LANES, S5_SW), 1)
    diag = (row // D_GROUP_DIM) == (col // D_STATE)

    def expand(x_ref):
        return jnp.where(diag, jnp.concatenate([x_ref[...]] * S5_GPB, axis=1), 0.0)

    b_r, b_i = expand(bre_ref), expand(bim_ref)
    bb_r = b_r * cr - b_i * ci
    bb_i = b_r * ci + b_i * cr
    c_r, c_i = expand(cre_ref), expand(cim_ref)

    for i in range(S5_L):
        pr, pi = a_pow(float(S5_L - 1 - i))
        rows = slice(i * LANES, (i + 1) * LANES)
        wz_ref[rows, 0:S5_SW] = (bb_r * pr - bb_i * pi).astype(BF16)
        wz_ref[rows, S5_SW:2 * S5_SW] = (bb_r * pi + bb_i * pr).astype(BF16)

    c_pow = []
    for k in range(S5_L + 1):
        pr, pi = a_pow(float(k))
        c_pow.append((c_r * pr - c_i * pi, c_r * pi + c_i * pr))
    for j in range(S5_L):
        rows = slice(j * LANES, (j + 1) * LANES)
        mit_ref[rows, 0:S5_SW] = c_pow[j + 1][0].astype(BF16)
        mit_ref[rows, S5_SW:2 * S5_SW] = (-c_pow[j + 1][1]).astype(BF16)

    nt = (((1,), (1,)), ((), ()))
    hi = lax.Precision.HIGHEST
    k_lag = []
    for t in range(S5_L):
        k_t = (lax.dot_general(bb_r, c_pow[t][0], nt, precision=hi, preferred_element_type=F32)
               - lax.dot_general(bb_i, c_pow[t][1], nt, precision=hi, preferred_element_type=F32))
        k_lag.append(k_t.astype(BF16))
    zero = jnp.zeros((LANES, LANES), BF16)
    for a in range(S5_L):
        for b in range(S5_L):
            ki_ref[a * LANES:(a + 1) * LANES, b * LANES:(b + 1) * LANES] = k_lag[b - a] if b >= a else zero

    rows8 = lax.broadcasted_iota(jnp.int32, (SUBLANES, S5_SW), 0)
    for idx, s in enumerate((1, 2, 4)):
        pr, pi = a_pow(float(S5_L * s))
        dec_ref[2 * idx] = jnp.where(rows8 >= s, pr, 0.0)
        dec_ref[2 * idx + 1] = jnp.where(rows8 >= s, pi, 0.0)
    kk = (S5_L * (rows8 + 1)).astype(F32)
    mag = jnp.exp(kk * zr)
    dec_ref[6] = mag * jnp.cos(kk * zi)
    dec_ref[7] = mag * jnp.sin(kk * zi)


def _s5_kernel(u_ref, are_ref, aim_ref, dt_ref, bre_ref, bim_ref, cre_ref, cim_ref, d_ref, o_ref,
               wz_ref, ki_ref, mit_ref, dec_ref, uf_ref, sre_ref, sim_ref, y_ref):
    @pl.when(pl.program_id(1) == 0)
    def _():
        _s5_build_operators(are_ref, aim_ref, dt_ref, bre_ref, bim_ref, cre_ref, cim_ref,
                            wz_ref, ki_ref, mit_ref, dec_ref)

    uf_ref[...] = u_ref[...].astype(F32)
    u_steps = [uf_ref[pl.ds(j, S5_NC, stride=S5_L), :] for j in range(S5_L)]
    u_all = jnp.concatenate(u_steps, axis=1).astype(BF16)

    z = jnp.dot(u_all, wz_ref[...], preferred_element_type=F32)
    ng = S5_NC // SUBLANES
    x_re = z[:, 0:S5_SW].reshape(ng, SUBLANES, S5_SW)
    x_im = z[:, S5_SW:2 * S5_SW].reshape(ng, SUBLANES, S5_SW)
    for idx, s in enumerate((1, 2, 4)):
        m_re = dec_ref[2 * idx]
        m_im = dec_ref[2 * idx + 1]
        r_re = pltpu.roll(x_re, s, 1)
        r_im = pltpu.roll(x_im, s, 1)
        x_re, x_im = x_re + (m_re * r_re - m_im * r_im), x_im + (m_re * r_im + m_im * r_re)
    sre_ref[...] = x_re.reshape(S5_NC, S5_SW)
    sim_ref[...] = x_im.reshape(S5_NC, S5_SW)
    p_re = dec_ref[6]
    p_im = dec_ref[7]

    def carry_body(g, c):
        c_re, c_im = c
        rows = pl.ds(pl.multiple_of(g * SUBLANES, SUBLANES), SUBLANES)
        s_re = sre_ref[rows, :] + (p_re * c_re - p_im * c_im)
        s_im = sim_ref[rows, :] + (p_re * c_im + p_im * c_re)
        sre_ref[rows, :] = s_re
        sim_ref[rows, :] = s_im
        last = slice(SUBLANES - 1, SUBLANES)
        return (jnp.broadcast_to(s_re[last, :], (SUBLANES, S5_SW)),
                jnp.broadcast_to(s_im[last, :], (SUBLANES, S5_SW)))

    zero = jnp.zeros((SUBLANES, S5_SW), F32)
    lax.fori_loop(0, ng, carry_body, (zero, zero))

    first = lax.broadcasted_iota(jnp.int32, (S5_NC, S5_SW), 0) == 0
    prev_re = jnp.where(first, 0.0, pltpu.roll(sre_ref[...], 1, 0))
    prev_im = jnp.where(first, 0.0, pltpu.roll(sim_ref[...], 1, 0))
    s_prev = jnp.concatenate([prev_re, prev_im], axis=1).astype(BF16)
    y = (lax.dot_general(s_prev, mit_ref[...], (((1,), (1,)), ((), ())), preferred_element_type=F32)
         + jnp.dot(u_all, ki_ref[...], preferred_element_type=F32))
    d = d_ref[...]
    for j in range(S5_L):
        yj = y[:, j * LANES:(j + 1) * LANES] + d * u_steps[j]
        y_ref[pl.ds(j, S5_NC, stride=S5_L), :] = _gelu_tanh(yj)
    o_ref[...] = y_ref[...].astype(o_ref.dtype)


def _s5(proj, a_re, a_im, b_re, b_im, c_re, c_im, log_dt, d_skip, o):
    ublock = (C_WIDTH + 2 * C_KV_WIDTH) // LANES
    flat = lambda t: t[o].astype(F32).reshape(S5_NB, 1, S5_SW)
    dt = jnp.repeat(jnp.exp(log_dt[o].astype(F32)), D_STATE).reshape(S5_NB, 1, S5_SW)
    rows_gc = lambda t: t.astype(F32).reshape(S5_NB, LANES, D_STATE)
    b_gc = lambda t: rows_gc(t[o].transpose(0, 2, 1))
    vec = pl.BlockSpec((None, 1, S5_SW), lambda n, b: (n, 0, 0))
    mat = pl.BlockSpec((None, LANES, D_STATE), lambda n, b: (n, 0, 0))
    return pl.pallas_call(
        _s5_kernel,
        out_shape=jax.ShapeDtypeStruct((TOKENS, D_WIDTH), BF16),
        grid=(S5_NB, BATCH),
        in_specs=[pl.BlockSpec((SEQ, LANES), lambda n, b: (b, ublock + n)),
                  vec, vec, vec, mat, mat, mat, mat,
                  pl.BlockSpec((None, 1, LANES), lambda n, b: (o, 0, n))],
        out_specs=pl.BlockSpec((SEQ, LANES), lambda n, b: (b, n)),
        scratch_shapes=[pltpu.VMEM((S5_CW, 2 * S5_SW), BF16),
                        pltpu.VMEM((S5_CW, S5_CW), BF16),
                        pltpu.VMEM((S5_CW, 2 * S5_SW), BF16),
                        pltpu.VMEM((8, SUBLANES, S5_SW), F32),
                        pltpu.VMEM((SEQ, LANES), F32),
                        pltpu.VMEM((S5_NC, S5_SW), F32), pltpu.VMEM((S5_NC, S5_SW), F32),
                        pltpu.VMEM((SEQ, LANES), F32)],
        compiler_params=_cparams(("parallel", "arbitrary")),
        name="s5_ssm",
    )(proj, flat(a_re), flat(a_im), dt, b_gc(b_re), b_gc(b_im), rows_gc(c_re[o]), rows_gc(c_im[o]),
      d_skip.reshape(-1, 1, D_WIDTH))


GLU_TN = 512


def _glu_kernel(z_ref, w_ref, b_ref, zc_ref, o_ref):
    gate = jax.nn.sigmoid(jnp.dot(z_ref[...], w_ref[...].astype(BF16), preferred_element_type=F32) + b_ref[...])
    o_ref[...] = (zc_ref[...].astype(F32) * gate).astype(o_ref.dtype)


def _glu(z, w, b, o):
    return pl.pallas_call(
        _glu_kernel,
        out_shape=jax.ShapeDtypeStruct((TOKENS, D_WIDTH), BF16),
        grid=(TOKENS // MM_TM, D_WIDTH // GLU_TN),
        in_specs=[pl.BlockSpec((MM_TM, D_WIDTH), lambda i, j: (i, 0)),
                  pl.BlockSpec((None, D_WIDTH, GLU_TN), lambda i, j: (o, 0, j)),
                  pl.BlockSpec((None, 1, GLU_TN), lambda i, j: (o, 0, j)),
                  pl.BlockSpec((MM_TM, GLU_TN), lambda i, j: (i, j))],
        out_specs=pl.BlockSpec((MM_TM, GLU_TN), lambda i, j: (i, j)),
        compiler_params=_cparams(("parallel", "parallel")),
        name="s5_glu",
    )(z, w, b.reshape(-1, 1, D_WIDTH), z)


OUT_TM = 512
DOWN_TM = 256


def kernel(x, c, positions, ada_w, ada_b, norm_mix, norm_ffn, norm_final, ev_w_in, ev_conv_w, ev_conv_b, ev_gate_a_w, ev_gate_a_b, ev_gate_x_w, ev_gate_x_b, ev_lambda, ev_w_out, od_w_in, od_sinks, od_a_re, od_a_im, od_b_re, od_b_im, od_c_re, od_c_im, od_d, od_log_dt, od_glu_w, od_glu_b, od_w_out, ffn_w_in, ffn_conv_w, ffn_conv_b, ffn_w_out):
    ffn_out16 = ffn_w_out.astype(BF16)
    ffn_cw = ffn_conv_w.astype(F32)
    ffn_cb = ffn_conv_b.astype(F32).reshape(DEPTH, 1, 2 * D_FF)

    mod = _ada_mod(c, ada_w, ada_b)
    mod = mod.reshape(DEPTH, SUBLANES, 6, 1, D_MODEL).transpose(0, 2, 1, 3, 4)
    cos_a, sin_a, cos_c, sin_cp, sin_cm = _rope_tables(positions)

    xt = x.reshape(TOKENS, D_MODEL).astype(F32)
    h = _prenorm(xt, norm_mix, mod, 0)
    for layer in range(DEPTH):
        idx = layer // 2
        if layer % 2 == 0:
            proj = _matmul(h, ev_w_in, idx, BF16, "even_in_proj")
            attn = _attn_a(proj, cos_a, sin_a)
            other = _lru(proj, ev_conv_w, ev_conv_b, ev_gate_a_w, ev_gate_a_b, ev_gate_x_w, ev_gate_x_b,
                         ev_lambda, idx)
            w_out = ev_w_out
        else:
            proj = _matmul(h, od_w_in, idx, BF16, "odd_in_proj")
            attn = _swa(proj, od_sinks, cos_c, sin_cp, sin_cm, idx)
            z = _s5(proj, od_a_re, od_a_im, od_b_re, od_b_im, od_c_re, od_c_im, od_log_dt, od_d, idx)
            other = _glu(z, od_glu_w, od_glu_b, idx)
            w_out = od_w_out
        xt, h2 = _mm_res([attn, other], w_out, idx, OUT_TM, xt, mod, layer, 2, norm_ffn, layer, layer, 3,
                         False, "mix_out_proj")
        act = _ffn_up(h2, ffn_w_in, ffn_cw, ffn_cb, layer)
        if layer + 1 < DEPTH:
            xt, h = _mm_res([act], ffn_out16, layer, DOWN_TM, xt, mod, layer, 5, norm_mix, layer + 1, layer + 1,
                            0, False, "ffn_down_proj")
        else:
            out = _mm_res([act], ffn_out16, layer, DOWN_TM, xt, mod, layer, 5, norm_final, 0, layer, 0, True,
                          "ffn_down_final")
    return out.reshape(BATCH, SEQ, D_MODEL).astype(x.dtype)
```

```python
import functools
import math

import jax
import jax.numpy as jnp
import numpy as np
from jax import lax
from jax.experimental import pallas as pl
from jax.experimental.pallas import tpu as pltpu

F32 = jnp.float32
BF16 = jnp.bfloat16

D_MODEL = 2048
BATCH = 4
SEQ = 2048
TOKENS = BATCH * SEQ
DEPTH = 4
ROPE_THETA = 10000.0
NORM_EPS = 1e-6
LANES = 128
SUBLANES = 8
BF16_ROWS = 16

A_HEAD_DIM = 128
A_HEADS = 8
A_WIDTH = 1024
A_PATTERNS = ((128, 1), (512, 4), (2048, 16))
B_WIDTH = 1024
B_BLOCKS = 8
B_CONV = 4
LRU_C = 8.0
EVEN_IN = 3 * A_WIDTH + 2 * B_WIDTH

C_HEAD_DIM = 64
C_HEADS = 16
C_KV_HEADS = 2
C_GROUP = 8
C_WIDTH = 1024
C_KV_WIDTH = 128
C_WINDOW = 128
D_WIDTH = 1024
D_GROUP_DIM = 16
D_GROUPS = 64
D_STATE = 64
ODD_IN = C_WIDTH + 2 * C_KV_WIDTH + D_WIDTH

D_FF = 5504
D_FF_PAD = 5632
FFN_CONV = 3

NEG = -1e30

VMEM_LIMIT = 56 * 1024 * 1024


def _cparams(sem, vmem=VMEM_LIMIT):
    return pltpu.CompilerParams(dimension_semantics=sem, vmem_limit_bytes=vmem)


GELU_C1 = 2.0 * math.sqrt(2.0 / math.pi)
GELU_C2 = 0.044715 * GELU_C1


def _gelu_tanh(x):
    z = x * (GELU_C1 + GELU_C2 * (x * x))
    return x / (1.0 + jnp.exp(-z))


ADA_TN = 1024


def _ada_kernel(c_ref, w_ref, b_ref, o_ref):
    c = c_ref[...]
    cond = (c * jax.nn.sigmoid(c)).astype(BF16)
    o_ref[...] = jnp.dot(cond, w_ref[...].astype(BF16), preferred_element_type=F32) + b_ref[...]


def _ada_mod(c, ada_w, ada_b):
    c8 = jnp.zeros((SUBLANES, D_MODEL), F32).at[:BATCH].set(c.astype(F32))
    n = 6 * D_MODEL
    return pl.pallas_call(
        _ada_kernel,
        out_shape=jax.ShapeDtypeStruct((DEPTH, SUBLANES, n), F32),
        grid=(DEPTH, n // ADA_TN),
        in_specs=[
            pl.BlockSpec((SUBLANES, D_MODEL), lambda l, j: (0, 0)),
            pl.BlockSpec((None, D_MODEL, ADA_TN), lambda l, j: (l, 0, j)),
            pl.BlockSpec((None, 1, ADA_TN), lambda l, j: (l, 0, j)),
        ],
        out_specs=pl.BlockSpec((None, SUBLANES, ADA_TN), lambda l, j: (l, 0, j)),
        compiler_params=_cparams(("parallel", "parallel")),
        name="ada_mod",
    )(c8, ada_w, ada_b.reshape(DEPTH, 1, n))


ROPE_TM = 1024


def _rope_kernel(pos_ref, inva_ref, invc_ref, ca_ref, sa_ref, cc_ref, scp_ref, scm_ref):
    pos = pos_ref[...].astype(F32)
    lane = lax.broadcasted_iota(jnp.int32, (ROPE_TM, LANES), 1)
    ang = pos * inva_ref[...]
    s = jnp.sin(ang)
    ca_ref[...] = jnp.cos(ang)
    sa_ref[...] = jnp.where(lane < A_HEAD_DIM // 2, -s, s)
    ang = pos * invc_ref[...]
    s = jnp.sin(ang)
    cc_ref[...] = jnp.cos(ang)
    second = (lane % C_HEAD_DIM) >= C_HEAD_DIM // 2
    scp_ref[...] = jnp.where(second, s, 0.0)
    scm_ref[...] = jnp.where(second, 0.0, -s)


def _rope_tables(positions):
    half_a, half_c = A_HEAD_DIM // 2, C_HEAD_DIM // 2
    inv_a = ROPE_THETA ** (-jnp.arange(half_a, dtype=F32) / half_a)
    inv_c = ROPE_THETA ** (-jnp.arange(half_c, dtype=F32) / half_c)
    inv_a = jnp.tile(inv_a, LANES // half_a).reshape(1, LANES)
    inv_c = jnp.tile(inv_c, LANES // half_c).reshape(1, LANES)
    tab = jax.ShapeDtypeStruct((TOKENS, LANES), F32)
    row = pl.BlockSpec((ROPE_TM, LANES), lambda i: (i, 0))
    vec = pl.BlockSpec((1, LANES), lambda i: (0, 0))
    return pl.pallas_call(
        _rope_kernel,
        out_shape=(tab,) * 5,
        grid=(TOKENS // ROPE_TM,),
        in_specs=[pl.BlockSpec((ROPE_TM, 1), lambda i: (i, 0)), vec, vec],
        out_specs=(row,) * 5,
        compiler_params=_cparams(("parallel",)),
        name="rope_tables",
    )(positions.reshape(TOKENS, 1), inv_a, inv_c)


def _norm_mod(x, g, sh, sc):
    ms = jnp.mean(x * x, axis=-1, keepdims=True)
    y = x * lax.rsqrt(ms + NORM_EPS) * g
    return y * (1.0 + sc) + sh


def _rmsnorm(x, g):
    ms = jnp.mean(x * x, axis=-1, keepdims=True)
    return x * lax.rsqrt(ms + NORM_EPS) * g


NORM_TM = 512


def _prenorm_kernel(x_ref, g_ref, sh_ref, sc_ref, h_ref):
    h_ref[...] = _norm_mod(x_ref[...], g_ref[...], sh_ref[...], sc_ref[...]).astype(BF16)


def _mod_spec(layer, chunk, tm):
    return pl.BlockSpec((None, None, None, 1, D_MODEL),
                        lambda i, *_: (layer, chunk, (i * tm) // SEQ, 0, 0))


def _prenorm(x, norm_g, mod, layer):
    vec = pl.BlockSpec((None, 1, D_MODEL), lambda i: (layer, 0, 0))
    return pl.pallas_call(
        _prenorm_kernel,
        out_shape=jax.ShapeDtypeStruct((TOKENS, D_MODEL), BF16),
        grid=(TOKENS // NORM_TM,),
        in_specs=[pl.BlockSpec((NORM_TM, D_MODEL), lambda i: (i, 0)), vec,
                  _mod_spec(layer, 0, NORM_TM), _mod_spec(layer, 1, NORM_TM)],
        out_specs=pl.BlockSpec((NORM_TM, D_MODEL), lambda i: (i, 0)),
        compiler_params=_cparams(("parallel",)),
        name="prenorm",
    )(x, norm_g.reshape(DEPTH, 1, D_MODEL), mod, mod)


MM_TM = 2048
MM_TN_CHOICES = (1024, 768, 512, 256)


def _mm_kernel(a_ref, w_ref, o_ref):
    o_ref[...] = jnp.dot(a_ref[...], w_ref[...].astype(BF16), preferred_element_type=F32).astype(o_ref.dtype)


def _matmul(a, w, idx, out_dtype, name):
    m, k = a.shape
    n = w.shape[-1]
    tn = next(t for t in MM_TN_CHOICES if n % t == 0)
    return pl.pallas_call(
        _mm_kernel,
        out_shape=jax.ShapeDtypeStruct((m, n), out_dtype),
        grid=(m // MM_TM, n // tn),
        in_specs=[pl.BlockSpec((MM_TM, k), lambda i, j: (i, 0)),
                  pl.BlockSpec((None, k, tn), lambda i, j: (idx, 0, j))],
        out_specs=pl.BlockSpec((MM_TM, tn), lambda i, j: (i, j)),
        compiler_params=_cparams(("parallel", "parallel")),
        name=name,
    )(a, w)


def _mm_res_kernel(*refs, n_lhs, final, cast_w):
    a_refs = refs[:n_lhs]
    w_ref, x_ref, gate_ref, g_ref, sh_ref, sc_ref = refs[n_lhs:n_lhs + 6]
    outs = refs[n_lhs + 6:]
    if cast_w:
        outs, w16_ref = outs[:-1], outs[-1]

        @pl.when(pl.program_id(0) == 0)
        def _():
            w16_ref[...] = w_ref[...].astype(BF16)

        w_ref = w16_ref
    y = None
    row0 = 0
    for a_ref in a_refs:
        kk = min(a_ref.shape[1], w_ref.shape[0] - row0)
        part = jnp.dot(a_ref[:, 0:kk], w_ref[row0:row0 + kk, :], preferred_element_type=F32)
        y = part if y is None else y + part
        row0 += kk
    xn = x_ref[...] + gate_ref[...] * y
    if final:
        outs[0][...] = _rmsnorm(xn, g_ref[...])
    else:
        outs[0][...] = xn
        outs[1][...] = _norm_mod(xn, g_ref[...], sh_ref[...], sc_ref[...]).astype(BF16)


def _mm_res(lhs, w, widx, tm, x, mod, gate_layer, gate_chunk, norm_g, norm_idx, mod_layer, mod_chunk, final, name):
    m = lhs[0].shape[0]
    kdim = w.shape[1]
    cast_w = w.dtype != BF16
    row = pl.BlockSpec((tm, D_MODEL), lambda i: (i, 0))
    if final:
        gvec = pl.BlockSpec((1, D_MODEL), lambda i: (0, 0))
        g_arr = norm_g.reshape(1, D_MODEL)
        out_shape = jax.ShapeDtypeStruct((m, D_MODEL), F32)
        out_specs = row
    else:
        gvec = pl.BlockSpec((None, 1, D_MODEL), lambda i: (norm_idx, 0, 0))
        g_arr = norm_g.reshape(DEPTH, 1, D_MODEL)
        out_shape = (jax.ShapeDtypeStruct((m, D_MODEL), F32), jax.ShapeDtypeStruct((m, D_MODEL), BF16))
        out_specs = (row, row)
    return pl.pallas_call(
        functools.partial(_mm_res_kernel, n_lhs=len(lhs), final=final, cast_w=cast_w),
        out_shape=out_shape,
        grid=(m // tm,),
        in_specs=[pl.BlockSpec((tm, a.shape[1]), lambda i: (i, 0)) for a in lhs] + [
                  pl.BlockSpec((None, kdim, D_MODEL), lambda i: (widx, 0, 0), pipeline_mode=pl.Buffered(1)),
                  row,
                  _mod_spec(gate_layer, gate_chunk, tm),
                  gvec,
                  _mod_spec(mod_layer, mod_chunk, tm),
                  _mod_spec(mod_layer, mod_chunk + 1, tm)],
        out_specs=out_specs,
        scratch_shapes=[pltpu.VMEM((kdim, D_MODEL), BF16)] if cast_w else [],
        compiler_params=_cparams(("arbitrary",)),
        name=name,
    )(*lhs, w, x, mod, g_arr, mod, mod)


FFN_TM = 1024
FFN_TF = 512
FFN_NT = TOKENS // FFN_TM
FFN_NF = D_FF_PAD // FFN_TF
FFN_EDGE = D_FF_PAD - D_FF
FFN_TAIL = SUBLANES


def _ffn_up_kernel(h_ref, wg_ref, wv_win_ref, cwg_ref, cwv_win_ref, cbg_ref, cbv_win_ref,
                   o_ref, w16_ref, cv_ref, tail_ref):
    j = pl.program_id(0)
    i = pl.program_id(1)
    keep = FFN_TF - FFN_EDGE
    wg16_ref = w16_ref.at[:, 0:FFN_TF]
    wv_ref = w16_ref.at[:, FFN_TF:2 * FFN_TF]

    @pl.when(i == 0)
    def _():
        wg16_ref[...] = wg_ref[...].astype(BF16)

    @pl.when((i == 0) & (j == FFN_NF - 1))
    def _():
        wv_ref[:, 0:keep] = wv_win_ref[:, FFN_EDGE:FFN_TF].astype(BF16)
        wv_ref[:, keep:FFN_TF] = wv_win_ref[:, 0:FFN_EDGE].astype(BF16)
        cv_ref[0:FFN_CONV, 0:keep] = cwv_win_ref[:, FFN_EDGE:FFN_TF]
        cv_ref[FFN_CONV:FFN_CONV + 1, 0:keep] = cbv_win_ref[:, FFN_EDGE:FFN_TF]
        cv_ref[:, keep:FFN_TF] = jnp.zeros((SUBLANES, FFN_EDGE), F32)

    @pl.when((i == 0) & (j != FFN_NF - 1))
    def _():
        wv_ref[...] = wv_win_ref[...].astype(BF16)
        cv_ref[0:FFN_CONV, :] = cwv_win_ref[...]
        cv_ref[FFN_CONV:FFN_CONV + 1, :] = cbv_win_ref[...]

    lhs = h_ref[...]
    starts_seq = (i * FFN_TM) % SEQ == 0

    def conv(u, k, sl, w, b):
        prev = jnp.where(starts_seq, 0.0, tail_ref[k, :, sl])
        tail_ref[k, :, sl] = u[FFN_TM - FFN_TAIL:, :]
        ext = jnp.concatenate([prev, u], axis=0)
        out = b + w[FFN_CONV - 1:FFN_CONV, :] * u
        for t in range(1, FFN_CONV):
            out = out + w[FFN_CONV - 1 - t:FFN_CONV - t, :] * pltpu.roll(ext, t, 0)[FFN_TAIL:, :]
        return out

    u = jnp.dot(lhs, w16_ref[...], preferred_element_type=F32)
    sl = slice(0, FFN_TF)
    g = conv(u[:, 0:FFN_TF], 0, sl, cwg_ref[...], cbg_ref[...])
    v = conv(u[:, FFN_TF:2 * FFN_TF], 1, sl, cv_ref[0:FFN_CONV, :], cv_ref[FFN_CONV:FFN_CONV + 1, :])
    o_ref[...] = (_gelu_tanh(g) * v).astype(BF16)


def _ffn_up(h, w_in, conv_w, conv_b, layer):
    def voff(j):
        return LANES * jnp.minimum(D_FF // LANES + (FFN_TF // LANES) * j, (2 * D_FF - FFN_TF) // LANES)

    def win(rows):
        return pl.BlockSpec((None, pl.Element(rows), pl.Element(FFN_TF)), lambda j, i: (layer, 0, voff(j)))

    return pl.pallas_call(
        _ffn_up_kernel,
        out_shape=jax.ShapeDtypeStruct((TOKENS, D_FF_PAD), BF16),
        grid=(FFN_NF, FFN_NT),
        in_specs=[pl.BlockSpec((FFN_TM, D_MODEL), lambda j, i: (i, 0)),
                  pl.BlockSpec((None, D_MODEL, FFN_TF), lambda j, i: (layer, 0, j)),
                  win(D_MODEL),
                  pl.BlockSpec((None, FFN_CONV, FFN_TF), lambda j, i: (layer, 0, j)),
                  win(FFN_CONV),
                  pl.BlockSpec((None, 1, FFN_TF), lambda j, i: (layer, 0, j)),
                  win(1)],
        out_specs=pl.BlockSpec((FFN_TM, FFN_TF), lambda j, i: (i, j)),
        scratch_shapes=[pltpu.VMEM((D_MODEL, 2 * FFN_TF), BF16),
                        pltpu.VMEM((SUBLANES, FFN_TF), F32),
                        pltpu.VMEM((2, FFN_TAIL, FFN_TF), F32)],
        compiler_params=_cparams(("arbitrary", "arbitrary")),
        name="ffn_up",
    )(h, w_in, w_in, conv_w, conv_w, conv_b, conv_b)


ATT_T = 256
ATT_NBIAS = 4


def _dilated_bias_tiles():
    tiles = np.zeros((ATT_NBIAS, ATT_T, ATT_T), np.float32)
    qi = np.arange(ATT_T)[:, None]
    kj = np.arange(ATT_T)[None, :]
    for off in range(ATT_NBIAS):
        delta = off * ATT_T + qi - kj
        count = np.zeros_like(delta)
        for window, dil in A_PATTERNS:
            count += ((delta >= 0) & (delta <= window) & (delta % dil == 0)).astype(delta.dtype)
        tiles[off] = np.where(count > 0, np.log(np.maximum(count, 1)), NEG)
    return tiles


def _attn_a_kernel(q_ref, k_ref, v_ref, cos_ref, sin_ref, bias_ref, o_ref, qs_ref, ks_ref, vs_ref):
    cos = cos_ref[...]
    sin = sin_ref[...]
    half = A_HEAD_DIM // 2
    q = q_ref[...].astype(F32)
    k = k_ref[...].astype(F32)
    scale = A_HEAD_DIM ** -0.5
    qs_ref[...] = ((q * cos + pltpu.roll(q, half, 1) * sin) * scale).astype(BF16)
    ks_ref[...] = (k * cos + pltpu.roll(k, half, 1) * sin).astype(BF16)
    vs_ref[...] = v_ref[...].astype(BF16)

    for i in range(SEQ // ATT_T):
        n = (i + 1) * ATT_T
        q_blk = qs_ref[i * ATT_T:n, :]
        s = lax.dot_general(q_blk, ks_ref[0:n, :], (((1,), (1,)), ((), ())), preferred_element_type=F32)
        s = s + jnp.concatenate([bias_ref[min(i - j, ATT_NBIAS - 1)] for j in range(i + 1)], axis=1)
        m = jnp.max(s, axis=-1, keepdims=True)
        p = jnp.exp(s - m)
        l = jnp.sum(p, axis=-1, keepdims=True)
        acc = jnp.dot(p.astype(BF16), vs_ref[0:n, :], preferred_element_type=F32)
        o_ref[i * ATT_T:n, :] = (acc / l).astype(o_ref.dtype)


def _attn_a(proj, cos_a, sin_a):
    bias = jnp.asarray(_dilated_bias_tiles())
    tab = pl.BlockSpec((SEQ, LANES), lambda b, h: (b, 0))
    return pl.pallas_call(
        _attn_a_kernel,
        out_shape=jax.ShapeDtypeStruct((TOKENS, A_WIDTH), BF16),
        grid=(BATCH, A_HEADS),
        in_specs=[pl.BlockSpec((SEQ, A_HEAD_DIM), lambda b, h: (b, h)),
                  pl.BlockSpec((SEQ, A_HEAD_DIM), lambda b, h: (b, A_HEADS + h)),
                  pl.BlockSpec((SEQ, A_HEAD_DIM), lambda b, h: (b, 2 * A_HEADS + h)),
                  tab, tab,
                  pl.BlockSpec((ATT_NBIAS, ATT_T, ATT_T), lambda b, h: (0, 0, 0))],
        out_specs=pl.BlockSpec((SEQ, A_HEAD_DIM), lambda b, h: (b, h)),
        scratch_shapes=[pltpu.VMEM((SEQ, A_HEAD_DIM), BF16)] * 3,
        compiler_params=_cparams(("parallel", "parallel")),
        name="attn_dilated",
    )(proj, proj, proj, cos_a, sin_a, bias)


LRU_TS = 512
LRU_HALO = SUBLANES


def _lru_kernel(xb_ref, yb_ref, cw_ref, cb_ref, ga_ref, gab_ref, gx_ref, gxb_ref, lam_ref, o_ref,
                ext_ref, a_ref, b_ref, carry_ref):
    t = pl.program_id(1)

    @pl.when(t == 0)
    def _():
        ext_ref[0:LRU_HALO, :] = jnp.zeros((LRU_HALO, B_WIDTH), F32)
        carry_ref[...] = jnp.zeros_like(carry_ref)

    ext_ref[LRU_HALO:, :] = xb_ref[...].astype(F32)
    ext = ext_ref[...]
    base = LRU_HALO - (B_CONV - 1)
    xc = cb_ref[...]
    for i in range(B_CONV):
        xc = xc + cw_ref[i:i + 1, :] * ext[base + i:base + i + LRU_TS, :]
    ext_ref[0:LRU_HALO, :] = ext[LRU_TS:LRU_TS + LRU_HALO, :]

    lam = lam_ref[...]
    neg_sp = -LRU_C * (jnp.maximum(-lam, 0.0) + jnp.log1p(jnp.exp(-jnp.abs(lam))))
    width = B_WIDTH // B_BLOCKS
    for blk in range(B_BLOCKS):
        sl = slice(blk * width, (blk + 1) * width)
        xh = xc[:, sl]
        xh16 = xh.astype(BF16)
        r = jax.nn.sigmoid(jnp.dot(xh16, ga_ref[blk].astype(BF16), preferred_element_type=F32) + gab_ref[:, sl])
        gi = jax.nn.sigmoid(jnp.dot(xh16, gx_ref[blk].astype(BF16), preferred_element_type=F32) + gxb_ref[:, sl])
        log_a = r * neg_sp[:, sl]
        a_ref[:, sl] = jnp.exp(log_a)
        th = jnp.tanh(log_a)
        b_ref[:, sl] = jnp.sqrt(-2.0 * th / (1.0 - th)) * (gi * xh)

    row = lax.broadcasted_iota(jnp.int32, (SUBLANES, B_WIDTH), 0)

    def scan_body(g, h_prev):
        rows = pl.ds(pl.multiple_of(g * SUBLANES, SUBLANES), SUBLANES)
        a = a_ref[rows, :]
        b = b_ref[rows, :]
        for s in (1, 2, 4):
            keep = row >= s
            a_sh = jnp.where(keep, pltpu.roll(a, s, 0), 1.0)
            b_sh = jnp.where(keep, pltpu.roll(b, s, 0), 0.0)
            b = a * b_sh + b
            a = a * a_sh
        h = a * h_prev + b
        b_ref[rows, :] = h
        return jnp.broadcast_to(h[SUBLANES - 1:SUBLANES, :], (SUBLANES, B_WIDTH))

    carry_ref[...] = lax.fori_loop(0, LRU_TS // SUBLANES, scan_body, carry_ref[...])
    o_ref[...] = (b_ref[...] * _gelu_tanh(yb_ref[...].astype(F32))).astype(o_ref.dtype)


def _lru(proj, conv_w, conv_b, ga_w, ga_b, gx_w, gx_b, lam, e):
    nts = SEQ // LRU_TS
    vec = pl.BlockSpec((None, 1, B_WIDTH), lambda b, t: (e, 0, 0))
    gate = pl.BlockSpec((None, B_BLOCKS, B_WIDTH // B_BLOCKS, B_WIDTH // B_BLOCKS), lambda b, t: (e, 0, 0, 0))
    r3 = lambda a: a.reshape(a.shape[0], 1, B_WIDTH)
    return pl.pallas_call(
        _lru_kernel,
        out_shape=jax.ShapeDtypeStruct((TOKENS, B_WIDTH), BF16),
        grid=(BATCH, nts),
        in_specs=[pl.BlockSpec((LRU_TS, B_WIDTH), lambda b, t: (b * nts + t, 3)),
                  pl.BlockSpec((LRU_TS, B_WIDTH), lambda b, t: (b * nts + t, 4)),
                  pl.BlockSpec((None, B_CONV, B_WIDTH), lambda b, t: (e, 0, 0)),
                  vec, gate, vec, gate, vec, vec],
        out_specs=pl.BlockSpec((LRU_TS, B_WIDTH), lambda b, t: (b * nts + t, 0)),
        scratch_shapes=[pltpu.VMEM((LRU_HALO + LRU_TS, B_WIDTH), F32),
                        pltpu.VMEM((LRU_TS, B_WIDTH), F32),
                        pltpu.VMEM((LRU_TS, B_WIDTH), F32),
                        pltpu.VMEM((SUBLANES, B_WIDTH), F32)],
        compiler_params=_cparams(("parallel", "arbitrary")),
        name="rg_lru",
    )(proj, proj, conv_w, r3(conv_b), ga_w, r3(ga_b), gx_w, r3(gx_b), r3(lam))


SWA_T = 128
SWA_PAIRS = C_GROUP // 2


def _swa_bias_tiles():
    qi = np.tile(np.arange(SWA_T), SWA_PAIRS)[:, None]
    kj = np.arange(2 * SWA_T)[None, :]
    delta = qi + SWA_T - kj
    band = (delta >= 0) & (delta <= C_WINDOW - 1)
    tiles = np.zeros((2, SWA_PAIRS * SWA_T, 2 * SWA_T), np.float32)
    tiles[0] = np.where(band & (kj >= SWA_T), 0.0, NEG)
    tiles[1] = np.where(band, 0.0, NEG)
    return tiles


def _swa_kernel(sink_ref, q_ref, kv_ref, cos_ref, sp_ref, sm_ref, bias_ref, o_ref,
                qs_ref, ka_ref, kb_ref, va_ref, vb_ref):
    kvh = pl.program_id(1)
    cos = cos_ref[...]
    s_plus = sp_ref[...]
    s_minus = sm_ref[...]
    quarter = C_HEAD_DIM // 2

    def rope(x):
        return x * cos + pltpu.roll(x, quarter, 1) * s_plus + pltpu.roll(x, LANES - quarter, 1) * s_minus

    scale = C_HEAD_DIM ** -0.5
    for j in range(SWA_PAIRS):
        sl = slice(j * LANES, (j + 1) * LANES)
        qs_ref[:, sl] = (rope(q_ref[:, sl].astype(F32)) * scale).astype(BF16)

    lane = lax.broadcasted_iota(jnp.int32, (SEQ, LANES), 1)
    low = lane < C_HEAD_DIM
    kk = rope(kv_ref[:, 0:LANES].astype(F32))
    vv = kv_ref[:, LANES:2 * LANES].astype(F32)
    kk = jnp.where(kvh == 0, kk, pltpu.roll(kk, C_HEAD_DIM, 1))
    vv = jnp.where(kvh == 0, vv, pltpu.roll(vv, C_HEAD_DIM, 1))
    k_lo = jnp.where(low, kk, 0.0)
    v_lo = jnp.where(low, vv, 0.0)
    zeros = jnp.zeros((SWA_T, LANES), BF16)
    for ref, val in ((ka_ref, k_lo), (kb_ref, pltpu.roll(k_lo, C_HEAD_DIM, 1)),
                     (va_ref, v_lo), (vb_ref, pltpu.roll(v_lo, C_HEAD_DIM, 1))):
        ref[0:SWA_T, :] = zeros
        ref[SWA_T:, :] = val.astype(BF16)

    rows_st = SWA_PAIRS * SWA_T
    pair = lax.broadcasted_iota(jnp.int32, (rows_st, 1), 0) // SWA_T
    sink_a = jnp.zeros((rows_st, 1), F32)
    sink_b = jnp.zeros((rows_st, 1), F32)
    for j in range(SWA_PAIRS):
        sink_a = jnp.where(pair == j, sink_ref[kvh * C_GROUP + 2 * j], sink_a)
        sink_b = jnp.where(pair == j, sink_ref[kvh * C_GROUP + 2 * j + 1], sink_b)

    def q_body(i, carry):
        r0 = pl.multiple_of(i * SWA_T, SWA_T)
        q_st = jnp.concatenate([qs_ref[pl.ds(r0, SWA_T), j * LANES:(j + 1) * LANES]
                                for j in range(SWA_PAIRS)], axis=0)
        bias = bias_ref[jnp.minimum(i, 1)]
        win = pl.ds(r0, 2 * SWA_T)
        out = jnp.zeros((rows_st, LANES), F32)
        for k_ref, v_ref, sink in ((ka_ref, va_ref, sink_a), (kb_ref, vb_ref, sink_b)):
            s = lax.dot_general(q_st, k_ref[win, :], (((1,), (1,)), ((), ())),
                                preferred_element_type=F32) + bias
            m = jnp.maximum(jnp.max(s, axis=-1, keepdims=True), sink)
            p = jnp.exp(s - m)
            den = jnp.sum(p, axis=-1, keepdims=True) + jnp.exp(sink - m)
            out = out + jnp.dot(p.astype(BF16), v_ref[win, :], preferred_element_type=F32) / den
        for j in range(SWA_PAIRS):
            o_ref[pl.ds(r0, SWA_T), j * LANES:(j + 1) * LANES] = out[j * SWA_T:(j + 1) * SWA_T, :].astype(o_ref.dtype)
        return carry

    lax.fori_loop(0, SEQ // SWA_T, q_body, 0)


def _swa(proj, sinks, cos_c, sin_cp, sin_cm, o):
    bias = jnp.asarray(_swa_bias_tiles())
    qw = C_WIDTH // C_KV_HEADS
    tab = pl.BlockSpec((SEQ, LANES), lambda b, g, *_: (b, 0))
    kv_block = C_WIDTH // (2 * C_KV_WIDTH)
    grid_spec = pltpu.PrefetchScalarGridSpec(
        num_scalar_prefetch=1,
        grid=(BATCH, C_KV_HEADS),
        in_specs=[pl.BlockSpec((SEQ, qw), lambda b, g, *_: (b, g)),
                  pl.BlockSpec((SEQ, 2 * C_KV_WIDTH), lambda b, g, *_: (b, kv_block)),
                  tab, tab, tab,
                  pl.BlockSpec((2, SWA_PAIRS * SWA_T, 2 * SWA_T), lambda b, g, *_: (0, 0, 0))],
        out_specs=pl.BlockSpec((SEQ, qw), lambda b, g, *_: (b, g)),
        scratch_shapes=[pltpu.VMEM((SEQ, qw), BF16)] + [pltpu.VMEM((SWA_T + SEQ, LANES), BF16)] * 4,
    )
    return pl.pallas_call(
        _swa_kernel,
        out_shape=jax.ShapeDtypeStruct((TOKENS, C_WIDTH), BF16),
        grid_spec=grid_spec,
        compiler_params=_cparams(("parallel", "parallel")),
        name="attn_swa",
    )(sinks[o].astype(F32), proj, proj, cos_c, sin_cp, sin_cm, bias)


S5_L = SUBLANES
S5_NC = SEQ // S5_L
S5_GPB = LANES // D_GROUP_DIM
S5_NB = D_WIDTH // LANES
S5_SW = S5_GPB * D_STATE
S5_CW = S5_L * LANES


def _s5_build_operators(are_ref, aim_ref, dt_ref, bre_ref, bim_ref, cre_ref, cim_ref,
                         wz_ref, ki_ref, mit_ref, dec_ref):
    lr, li, dt = are_ref[...], aim_ref[...], dt_ref[...]
    zr, zi = lr * dt, li * dt

    def a_pow(k):
        mag = jnp.exp(k * zr)
        return mag * jnp.cos(k * zi), mag * jnp.sin(k * zi)

    ar, ai = a_pow(1.0)
    den = lr * lr + li * li
    cr = ((ar - 1.0) * lr + ai * li) / den
    ci = (ai * lr - (ar - 1.0) * li) / den
    row = lax.broadcasted_iota(jnp.int32, (LANES, S5_SW), 0)
    col = lax.broadcasted_iota(jnp.int32, (LANES, S5_SW), 1)
    diag = (row // D_GROUP_DIM) == (col // D_STATE)

    def expand(x_ref):
        return jnp.where(diag, jnp.concatenate([x_ref[...]] * S5_GPB, axis=1), 0.0)

    b_r, b_i = expand(bre_ref), expand(bim_ref)
    bb_r = b_r * cr - b_i * ci
    bb_i = b_r * ci + b_i * cr
    c_r, c_i = expand(cre_ref), expand(cim_ref)

    for i in range(S5_L):
        pr, pi = a_pow(float(S5_L - 1 - i))
        rows = slice(i * LANES, (i + 1) * LANES)
        wz_ref[rows, 0:S5_SW] = (bb_r * pr - bb_i * pi).astype(BF16)
        wz_ref[rows, S5_SW:2 * S5_SW] = (bb_r * pi + bb_i * pr).astype(BF16)

    c_pow = []
    for k in range(S5_L + 1):
        pr, pi = a_pow(float(k))
        c_pow.append((c_r * pr - c_i * pi, c_r * pi + c_i * pr))
    for j in range(S5_L):
        rows = slice(j * LANES, (j + 1) * LANES)
        mit_ref[rows, 0:S5_SW] = c_pow[j + 1][0].astype(BF16)
        mit_ref[rows, S5_SW:2 * S5_SW] = (-c_pow[j + 1][1]).astype(BF16)

    nt = (((1,), (1,)), ((), ()))
    bb_r16, bb_i16 = bb_r.astype(BF16), bb_i.astype(BF16)
    k_lag = []
    for t in range(S5_L):
        k_t = (lax.dot_general(bb_r16, c_pow[t][0].astype(BF16), nt, preferred_element_type=F32)
               - lax.dot_general(bb_i16, c_pow[t][1].astype(BF16), nt, preferred_element_type=F32))
        k_lag.append(k_t.astype(BF16))
    zero = jnp.zeros((LANES, LANES), BF16)
    for a in range(S5_L):
        for b in range(S5_L):
            ki_ref[a * LANES:(a + 1) * LANES, b * LANES:(b + 1) * LANES] = k_lag[b - a] if b >= a else zero

    rows8 = lax.broadcasted_iota(jnp.int32, (SUBLANES, S5_SW), 0)
    for idx, s in enumerate((1, 2, 4)):
        pr, pi = a_pow(float(S5_L * s))
        dec_ref[2 * idx] = jnp.where(rows8 >= s, pr, 0.0)
        dec_ref[2 * idx + 1] = jnp.where(rows8 >= s, pi, 0.0)
    kk = (S5_L * (rows8 + 1)).astype(F32)
    mag = jnp.exp(kk * zr)
    dec_ref[6] = mag * jnp.cos(kk * zi)
    dec_ref[7] = mag * jnp.sin(kk * zi)


def _s5_kernel(u_ref, are_ref, aim_ref, dt_ref, bre_ref, bim_ref, cre_ref, cim_ref, d_ref, o_ref,
               wz_ref, ki_ref, mit_ref, dec_ref, uf_ref, sre_ref, sim_ref, y_ref):
    @pl.when(pl.program_id(1) == 0)
    def _():
        _s5_build_operators(are_ref, aim_ref, dt_ref, bre_ref, bim_ref, cre_ref, cim_ref,
                            wz_ref, ki_ref, mit_ref, dec_ref)

    uf_ref[...] = u_ref[...].astype(F32)
    u_steps = [uf_ref[pl.ds(j, S5_NC, stride=S5_L), :] for j in range(S5_L)]
    u_all = jnp.concatenate(u_steps, axis=1).astype(BF16)

    z = jnp.dot(u_all, wz_ref[...], preferred_element_type=F32)
    ng = S5_NC // SUBLANES
    x_re = z[:, 0:S5_SW].reshape(ng, SUBLANES, S5_SW)
    x_im = z[:, S5_SW:2 * S5_SW].reshape(ng, SUBLANES, S5_SW)
    for idx, s in enumerate((1, 2, 4)):
        m_re = dec_ref[2 * idx]
        m_im = dec_ref[2 * idx + 1]
        r_re = pltpu.roll(x_re, s, 1)
        r_im = pltpu.roll(x_im, s, 1)
        x_re, x_im = x_re + (m_re * r_re - m_im * r_im), x_im + (m_re * r_im + m_im * r_re)
    sre_ref[...] = x_re.reshape(S5_NC, S5_SW)
    sim_ref[...] = x_im.reshape(S5_NC, S5_SW)
    p_re = dec_ref[6]
    p_im = dec_ref[7]

    def carry_body(g, c):
        c_re, c_im = c
        rows = pl.ds(pl.multiple_of(g * SUBLANES, SUBLANES), SUBLANES)
        s_re = sre_ref[rows, :] + (p_re * c_re - p_im * c_im)
        s_im = sim_ref[rows, :] + (p_re * c_im + p_im * c_re)
        sre_ref[rows, :] = s_re
        sim_ref[rows, :] = s_im
        last = slice(SUBLANES - 1, SUBLANES)
        return (jnp.broadcast_to(s_re[last, :], (SUBLANES, S5_SW)),
                jnp.broadcast_to(s_im[last, :], (SUBLANES, S5_SW)))

    zero = jnp.zeros((SUBLANES, S5_SW), F32)
    lax.fori_loop(0, ng, carry_body, (zero, zero))

    first = lax.broadcasted_iota(jnp.int32, (S5_NC, S5_SW), 0) == 0
    prev_re = jnp.where(first, 0.0, pltpu.roll(sre_ref[...], 1, 0))
    prev_im = jnp.where(first, 0.0, pltpu.roll(sim_ref[...], 1, 0))
    s_prev = jnp.concatenate([prev_re, prev_im], axis=1).astype(BF16)
    y = (lax.dot_general(s_prev, mit_ref[...], (((1,), (1,)), ((), ())), preferred_element_type=F32)
         + jnp.dot(u_all, ki_ref[...], preferred_element_type=F32))
    d = d_ref[...]
    for j in range(S5_L):
        yj = y[:, j * LANES:(j + 1) * LANES] + d * u_steps[j]
        y_ref[pl.ds(j, S5_NC, stride=S5_L), :] = _gelu_tanh(yj)
    o_ref[...] = y_ref[...].astype(o_ref.dtype)


def _s5(proj, a_re, a_im, b_re, b_im, c_re, c_im, log_dt, d_skip, o):
    ublock = (C_WIDTH + 2 * C_KV_WIDTH) // LANES
    flat = lambda t: t[o].astype(F32).reshape(S5_NB, 1, S5_SW)
    dt = jnp.repeat(jnp.exp(log_dt[o].astype(F32)), D_STATE).reshape(S5_NB, 1, S5_SW)
    rows_gc = lambda t: t.astype(F32).reshape(S5_NB, LANES, D_STATE)
    b_gc = lambda t: rows_gc(t[o].transpose(0, 2, 1))
    vec = pl.BlockSpec((None, 1, S5_SW), lambda n, b: (n, 0, 0))
    mat = pl.BlockSpec((None, LANES, D_STATE), lambda n, b: (n, 0, 0))
    return pl.pallas_call(
        _s5_kernel,
        out_shape=jax.ShapeDtypeStruct((TOKENS, D_WIDTH), BF16),
        grid=(S5_NB, BATCH),
        in_specs=[pl.BlockSpec((SEQ, LANES), lambda n, b: (b, ublock + n)),
                  vec, vec, vec, mat, mat, mat, mat,
                  pl.BlockSpec((None, 1, LANES), lambda n, b: (o, 0, n))],
        out_specs=pl.BlockSpec((SEQ, LANES), lambda n, b: (b, n)),
        scratch_shapes=[pltpu.VMEM((S5_CW, 2 * S5_SW), BF16),
                        pltpu.VMEM((S5_CW, S5_CW), BF16),
                        pltpu.VMEM((S5_CW, 2 * S5_SW), BF16),
                        pltpu.VMEM((8, SUBLANES, S5_SW), F32),
                        pltpu.VMEM((SEQ, LANES), F32),
                        pltpu.VMEM((S5_NC, S5_SW), F32), pltpu.VMEM((S5_NC, S5_SW), F32),
                        pltpu.VMEM((SEQ, LANES), F32)],
        compiler_params=_cparams(("parallel", "arbitrary")),
        name="s5_ssm",
    )(proj, flat(a_re), flat(a_im), dt, b_gc(b_re), b_gc(b_im), rows_gc(c_re[o]), rows_gc(c_im[o]),
      d_skip.reshape(-1, 1, D_WIDTH))


GLU_TN = 512


def _glu_kernel(z_ref, w_ref, b_ref, zc_ref, o_ref):
    gate = jax.nn.sigmoid(jnp.dot(z_ref[...], w_ref[...].astype(BF16), preferred_element_type=F32) + b_ref[...])
    o_ref[...] = (zc_ref[...].astype(F32) * gate).astype(o_ref.dtype)


def _glu(z, w, b, o):
    return pl.pallas_call(
        _glu_kernel,
        out_shape=jax.ShapeDtypeStruct((TOKENS, D_WIDTH), BF16),
        grid=(TOKENS // MM_TM, D_WIDTH // GLU_TN),
        in_specs=[pl.BlockSpec((MM_TM, D_WIDTH), lambda i, j: (i, 0)),
                  pl.BlockSpec((None, D_WIDTH, GLU_TN), lambda i, j: (o, 0, j)),
                  pl.BlockSpec((None, 1, GLU_TN), lambda i, j: (o, 0, j)),
                  pl.BlockSpec((MM_TM, GLU_TN), lambda i, j: (i, j))],
        out_specs=pl.BlockSpec((MM_TM, GLU_TN), lambda i, j: (i, j)),
        compiler_params=_cparams(("parallel", "parallel")),
        name="s5_glu",
    )(z, w, b.reshape(-1, 1, D_WIDTH), z)


OUT_TM = 512
DOWN_TM = 256


def kernel(x, c, positions, ada_w, ada_b, norm_mix, norm_ffn, norm_final, ev_w_in, ev_conv_w, ev_conv_b, ev_gate_a_w, ev_gate_a_b, ev_gate_x_w, ev_gate_x_b, ev_lambda, ev_w_out, od_w_in, od_sinks, od_a_re, od_a_im, od_b_re, od_b_im, od_c_re, od_c_im, od_d, od_log_dt, od_glu_w, od_glu_b, od_w_out, ffn_w_in, ffn_conv_w, ffn_conv_b, ffn_w_out):
    ffn_out16 = ffn_w_out.astype(BF16)
    ffn_cw = ffn_conv_w.astype(F32)
    ffn_cb = ffn_conv_b.astype(F32).reshape(DEPTH, 1, 2 * D_FF)

    mod = _ada_mod(c, ada_w, ada_b)
    mod = mod.reshape(DEPTH, SUBLANES, 6, 1, D_MODEL).transpose(0, 2, 1, 3, 4)
    cos_a, sin_a, cos_c, sin_cp, sin_cm = _rope_tables(positions)

    xt = x.reshape(TOKENS, D_MODEL).astype(F32)
    h = _prenorm(xt, norm_mix, mod, 0)
    for layer in range(DEPTH):
        idx = layer // 2
        if layer % 2 == 0:
            proj = _matmul(h, ev_w_in, idx, BF16, "even_in_proj")
            attn = _attn_a(proj, cos_a, sin_a)
            other = _lru(proj, ev_conv_w, ev_conv_b, ev_gate_a_w, ev_gate_a_b, ev_gate_x_w, ev_gate_x_b,
                         ev_lambda, idx)
            w_out = ev_w_out
        else:
            proj = _matmul(h, od_w_in, idx, BF16, "odd_in_proj")
            attn = _swa(proj, od_sinks, cos_c, sin_cp, sin_cm, idx)
            z = _s5(proj, od_a_re, od_a_im, od_b_re, od_b_im, od_c_re, od_c_im, od_log_dt, od_d, idx)
            other = _glu(z, od_glu_w, od_glu_b, idx)
            w_out = od_w_out
        xt, h2 = _mm_res([attn, other], w_out, idx, OUT_TM, xt, mod, layer, 2, norm_ffn, layer, layer, 3,
                         False, "mix_out_proj")
        act = _ffn_up(h2, ffn_w_in, ffn_cw, ffn_cb, layer)
        if layer + 1 < DEPTH:
            xt, h = _mm_res([act], ffn_out16, layer, DOWN_TM, xt, mod, layer, 5, norm_mix, layer + 1, layer + 1,
                            0, False, "ffn_down_proj")
        else:
            out = _mm_res([act], ffn_out16, layer, DOWN_TM, xt, mod, layer, 5, norm_final, 0, layer, 0, True,
                          "ffn_down_final")
    return out.reshape(BATCH, SEQ, D_MODEL).astype(x.dtype)
```

```python
import functools
import math

import jax
import jax.numpy as jnp
import numpy as np
from jax import lax
from jax.experimental import pallas as pl
from jax.experimental.pallas import tpu as pltpu

F32 = jnp.float32
BF16 = jnp.bfloat16

D_MODEL = 2048
BATCH = 4
SEQ = 2048
TOKENS = BATCH * SEQ
DEPTH = 4
ROPE_THETA = 10000.0
NORM_EPS = 1e-6
LANES = 128
SUBLANES = 8
BF16_ROWS = 16

A_HEAD_DIM = 128
A_HEADS = 8
A_WIDTH = 1024
A_PATTERNS = ((128, 1), (512, 4), (2048, 16))
B_WIDTH = 1024
B_BLOCKS = 8
B_CONV = 4
LRU_C = 8.0
EVEN_IN = 3 * A_WIDTH + 2 * B_WIDTH

C_HEAD_DIM = 64
C_HEADS = 16
C_KV_HEADS = 2
C_GROUP = 8
C_WIDTH = 1024
C_KV_WIDTH = 128
C_WINDOW = 128
D_WIDTH = 1024
D_GROUP_DIM = 16
D_GROUPS = 64
D_STATE = 64
ODD_IN = C_WIDTH + 2 * C_KV_WIDTH + D_WIDTH

D_FF = 5504
D_FF_PAD = 5632
FFN_CONV = 3

NEG = -1e30

VMEM_LIMIT = 56 * 1024 * 1024


def _cparams(sem, vmem=VMEM_LIMIT):
    return pltpu.CompilerParams(dimension_semantics=sem, vmem_limit_bytes=vmem)


GELU_C1 = 2.0 * math.sqrt(2.0 / math.pi)
GELU_C2 = 0.044715 * GELU_C1


def _gelu_tanh(x):
    z = x * (GELU_C1 + GELU_C2 * (x * x))
    return x / (1.0 + jnp.exp(-z))


ADA_TN = 1024


def _ada_kernel(c_ref, w_ref, b_ref, o_ref):
    c = c_ref[...]
    cond = (c * jax.nn.sigmoid(c)).astype(BF16)
    o_ref[...] = jnp.dot(cond, w_ref[...].astype(BF16), preferred_element_type=F32) + b_ref[...]


def _ada_mod(c, ada_w, ada_b):
    c8 = jnp.zeros((SUBLANES, D_MODEL), F32).at[:BATCH].set(c.astype(F32))
    n = 6 * D_MODEL
    return pl.pallas_call(
        _ada_kernel,
        out_shape=jax.ShapeDtypeStruct((DEPTH, SUBLANES, n), F32),
        grid=(DEPTH, n // ADA_TN),
        in_specs=[
            pl.BlockSpec((SUBLANES, D_MODEL), lambda l, j: (0, 0)),
            pl.BlockSpec((None, D_MODEL, ADA_TN), lambda l, j: (l, 0, j)),
            pl.BlockSpec((None, 1, ADA_TN), lambda l, j: (l, 0, j)),
        ],
        out_specs=pl.BlockSpec((None, SUBLANES, ADA_TN), lambda l, j: (l, 0, j)),
        compiler_params=_cparams(("parallel", "parallel")),
        name="ada_mod",
    )(c8, ada_w, ada_b.reshape(DEPTH, 1, n))


ROPE_TM = 1024


def _rope_kernel(pos_ref, inva_ref, invc_ref, ca_ref, sa_ref, cc_ref, scp_ref, scm_ref):
    pos = pos_ref[...].astype(F32)
    lane = lax.broadcasted_iota(jnp.int32, (ROPE_TM, LANES), 1)
    ang = pos * inva_ref[...]
    s = jnp.sin(ang)
    ca_ref[...] = jnp.cos(ang)
    sa_ref[...] = jnp.where(lane < A_HEAD_DIM // 2, -s, s)
    ang = pos * invc_ref[...]
    s = jnp.sin(ang)
    cc_ref[...] = jnp.cos(ang)
    second = (lane % C_HEAD_DIM) >= C_HEAD_DIM // 2
    scp_ref[...] = jnp.where(second, s, 0.0)
    scm_ref[...] = jnp.where(second, 0.0, -s)


def _rope_tables(positions):
    half_a, half_c = A_HEAD_DIM // 2, C_HEAD_DIM // 2
    inv_a = ROPE_THETA ** (-jnp.arange(half_a, dtype=F32) / half_a)
    inv_c = ROPE_THETA ** (-jnp.arange(half_c, dtype=F32) / half_c)
    inv_a = jnp.tile(inv_a, LANES // half_a).reshape(1, LANES)
    inv_c = jnp.tile(inv_c, LANES // half_c).reshape(1, LANES)
    tab = jax.ShapeDtypeStruct((TOKENS, LANES), F32)
    row = pl.BlockSpec((ROPE_TM, LANES), lambda i: (i, 0))
    vec = pl.BlockSpec((1, LANES), lambda i: (0, 0))
    return pl.pallas_call(
        _rope_kernel,
        out_shape=(tab,) * 5,
        grid=(TOKENS // ROPE_TM,),
        in_specs=[pl.BlockSpec((ROPE_TM, 1), lambda i: (i, 0)), vec, vec],
        out_specs=(row,) * 5,
        compiler_params=_cparams(("parallel",)),
        name="rope_tables",
    )(positions.reshape(TOKENS, 1), inv_a, inv_c)


def _norm_mod(x, g, sh, sc):
    ms = jnp.mean(x * x, axis=-1, keepdims=True)
    y = x * lax.rsqrt(ms + NORM_EPS) * g
    return y * (1.0 + sc) + sh


def _rmsnorm(x, g):
    ms = jnp.mean(x * x, axis=-1, keepdims=True)
    return x * lax.rsqrt(ms + NORM_EPS) * g


def _mod_spec(layer, chunk, tm):
    return pl.BlockSpec((None, None, None, 1, D_MODEL),
                        lambda i, *_: (layer, chunk, (i * tm) // SEQ, 0, 0))


MM_TM = 2048
MM_TN_CHOICES = (1024, 768, 512, 256)


def _mm_kernel(a_ref, w_ref, o_ref):
    o_ref[...] = jnp.dot(a_ref[...], w_ref[...].astype(BF16), preferred_element_type=F32).astype(o_ref.dtype)


def _matmul(a, w, idx, out_dtype, name):
    m, k = a.shape
    n = w.shape[-1]
    tn = next(t for t in MM_TN_CHOICES if n % t == 0)
    return pl.pallas_call(
        _mm_kernel,
        out_shape=jax.ShapeDtypeStruct((m, n), out_dtype),
        grid=(m // MM_TM, n // tn),
        in_specs=[pl.BlockSpec((MM_TM, k), lambda i, j: (i, 0)),
                  pl.BlockSpec((None, k, tn), lambda i, j: (idx, 0, j))],
        out_specs=pl.BlockSpec((MM_TM, tn), lambda i, j: (i, j)),
        compiler_params=_cparams(("parallel", "parallel")),
        name=name,
    )(a, w)


NMM_TM = 1024


def _norm_mm_kernel(x_ref, g_ref, sh_ref, sc_ref, w_ref, o_ref, h_ref):
    @pl.when(pl.program_id(1) == 0)
    def _():
        h_ref[...] = _norm_mod(x_ref[...], g_ref[...], sh_ref[...], sc_ref[...]).astype(BF16)

    o_ref[...] = jnp.dot(h_ref[...], w_ref[...].astype(BF16), preferred_element_type=F32).astype(o_ref.dtype)


def _norm_matmul(x, norm_g, mod, layer, w, idx, out_dtype, name):
    m, k = x.shape
    n = w.shape[-1]
    tn = next(t for t in MM_TN_CHOICES if n % t == 0)
    return pl.pallas_call(
        _norm_mm_kernel,
        out_shape=jax.ShapeDtypeStruct((m, n), out_dtype),
        grid=(m // NMM_TM, n // tn),
        in_specs=[pl.BlockSpec((NMM_TM, k), lambda i, j: (i, 0)),
                  pl.BlockSpec((None, 1, D_MODEL), lambda i, j: (layer, 0, 0)),
                  _mod_spec(layer, 0, NMM_TM), _mod_spec(layer, 1, NMM_TM),
                  pl.BlockSpec((None, k, tn), lambda i, j: (idx, 0, j))],
        out_specs=pl.BlockSpec((NMM_TM, tn), lambda i, j: (i, j)),
        scratch_shapes=[pltpu.VMEM((NMM_TM, k), BF16)],
        compiler_params=_cparams(("parallel", "arbitrary")),
        name=name,
    )(x, norm_g.reshape(DEPTH, 1, D_MODEL), mod, mod, w)


def _mm_res_kernel(*refs, n_lhs, final, cast_w):
    a_refs = refs[:n_lhs]
    w_ref, x_ref, gate_ref, g_ref, sh_ref, sc_ref = refs[n_lhs:n_lhs + 6]
    outs = refs[n_lhs + 6:]
    if cast_w:
        outs, w16_ref = outs[:-1], outs[-1]

        @pl.when(pl.program_id(0) == 0)
        def _():
            w16_ref[...] = w_ref[...].astype(BF16)

        w_ref = w16_ref
    y = None
    row0 = 0
    for a_ref in a_refs:
        kk = min(a_ref.shape[1], w_ref.shape[0] - row0)
        part = jnp.dot(a_ref[:, 0:kk], w_ref[row0:row0 + kk, :], preferred_element_type=F32)
        y = part if y is None else y + part
        row0 += kk
    xn = x_ref[...] + gate_ref[...] * y
    if final:
        outs[0][...] = _rmsnorm(xn, g_ref[...])
    else:
        outs[0][...] = xn
        outs[1][...] = _norm_mod(xn, g_ref[...], sh_ref[...], sc_ref[...]).astype(BF16)


def _mm_res(lhs, w, widx, tm, x, mod, gate_layer, gate_chunk, norm_g, norm_idx, mod_layer, mod_chunk, final, name):
    m = lhs[0].shape[0]
    kdim = w.shape[1]
    cast_w = w.dtype != BF16
    row = pl.BlockSpec((tm, D_MODEL), lambda i: (i, 0))
    if final:
        gvec = pl.BlockSpec((1, D_MODEL), lambda i: (0, 0))
        g_arr = norm_g.reshape(1, D_MODEL)
        out_shape = jax.ShapeDtypeStruct((m, D_MODEL), F32)
        out_specs = row
    else:
        gvec = pl.BlockSpec((None, 1, D_MODEL), lambda i: (norm_idx, 0, 0))
        g_arr = norm_g.reshape(DEPTH, 1, D_MODEL)
        out_shape = (jax.ShapeDtypeStruct((m, D_MODEL), F32), jax.ShapeDtypeStruct((m, D_MODEL), BF16))
        out_specs = (row, row)
    return pl.pallas_call(
        functools.partial(_mm_res_kernel, n_lhs=len(lhs), final=final, cast_w=cast_w),
        out_shape=out_shape,
        grid=(m // tm,),
        in_specs=[pl.BlockSpec((tm, a.shape[1]), lambda i: (i, 0)) for a in lhs] + [
                  pl.BlockSpec((None, kdim, D_MODEL), lambda i: (widx, 0, 0), pipeline_mode=pl.Buffered(1)),
                  row,
                  _mod_spec(gate_layer, gate_chunk, tm),
                  gvec,
                  _mod_spec(mod_layer, mod_chunk, tm),
                  _mod_spec(mod_layer, mod_chunk + 1, tm)],
        out_specs=out_specs,
        scratch_shapes=[pltpu.VMEM((kdim, D_MODEL), BF16)] if cast_w else [],
        compiler_params=_cparams(("arbitrary",)),
        name=name,
    )(*lhs, w, x, mod, g_arr, mod, mod)


FFN_TM = 1024
FFN_TF = 512
FFN_NT = TOKENS // FFN_TM
FFN_NF = D_FF_PAD // FFN_TF
FFN_EDGE = D_FF_PAD - D_FF
FFN_TAIL = SUBLANES


def _ffn_up_kernel(h_ref, wg_ref, wv_win_ref, cwg_ref, cwv_win_ref, cbg_ref, cbv_win_ref,
                   o_ref, w16_ref, cv_ref, tail_ref):
    j = pl.program_id(0)
    i = pl.program_id(1)
    keep = FFN_TF - FFN_EDGE
    wg16_ref = w16_ref.at[:, 0:FFN_TF]
    wv_ref = w16_ref.at[:, FFN_TF:2 * FFN_TF]

    @pl.when(i == 0)
    def _():
        wg16_ref[...] = wg_ref[...].astype(BF16)

    @pl.when((i == 0) & (j == FFN_NF - 1))
    def _():
        wv_ref[:, 0:keep] = wv_win_ref[:, FFN_EDGE:FFN_TF].astype(BF16)
        wv_ref[:, keep:FFN_TF] = wv_win_ref[:, 0:FFN_EDGE].astype(BF16)
        cv_ref[0:FFN_CONV, 0:keep] = cwv_win_ref[:, FFN_EDGE:FFN_TF]
        cv_ref[FFN_CONV:FFN_CONV + 1, 0:keep] = cbv_win_ref[:, FFN_EDGE:FFN_TF]
        cv_ref[:, keep:FFN_TF] = jnp.zeros((SUBLANES, FFN_EDGE), F32)

    @pl.when((i == 0) & (j != FFN_NF - 1))
    def _():
        wv_ref[...] = wv_win_ref[...].astype(BF16)
        cv_ref[0:FFN_CONV, :] = cwv_win_ref[...]
        cv_ref[FFN_CONV:FFN_CONV + 1, :] = cbv_win_ref[...]

    lhs = h_ref[...]
    starts_seq = (i * FFN_TM) % SEQ == 0

    def conv(u, k, sl, w, b):
        prev = jnp.where(starts_seq, 0.0, tail_ref[k, :, sl])
        tail_ref[k, :, sl] = u[FFN_TM - FFN_TAIL:, :]
        ext = jnp.concatenate([prev, u], axis=0)
        out = b + w[FFN_CONV - 1:FFN_CONV, :] * u
        for t in range(1, FFN_CONV):
            out = out + w[FFN_CONV - 1 - t:FFN_CONV - t, :] * pltpu.roll(ext, t, 0)[FFN_TAIL:, :]
        return out

    u = jnp.dot(lhs, w16_ref[...], preferred_element_type=F32)
    sl = slice(0, FFN_TF)
    g = conv(u[:, 0:FFN_TF], 0, sl, cwg_ref[...], cbg_ref[...])
    v = conv(u[:, FFN_TF:2 * FFN_TF], 1, sl, cv_ref[0:FFN_CONV, :], cv_ref[FFN_CONV:FFN_CONV + 1, :])
    o_ref[...] = (_gelu_tanh(g) * v).astype(BF16)


def _ffn_up(h, w_in, conv_w, conv_b, layer):
    def voff(j):
        return LANES * jnp.minimum(D_FF // LANES + (FFN_TF // LANES) * j, (2 * D_FF - FFN_TF) // LANES)

    def win(rows):
        return pl.BlockSpec((None, pl.Element(rows), pl.Element(FFN_TF)), lambda j, i: (layer, 0, voff(j)))

    return pl.pallas_call(
        _ffn_up_kernel,
        out_shape=jax.ShapeDtypeStruct((TOKENS, D_FF_PAD), BF16),
        grid=(FFN_NF, FFN_NT),
        in_specs=[pl.BlockSpec((FFN_TM, D_MODEL), lambda j, i: (i, 0)),
                  pl.BlockSpec((None, D_MODEL, FFN_TF), lambda j, i: (layer, 0, j)),
                  win(D_MODEL),
                  pl.BlockSpec((None, FFN_CONV, FFN_TF), lambda j, i: (layer, 0, j)),
                  win(FFN_CONV),
                  pl.BlockSpec((None, 1, FFN_TF), lambda j, i: (layer, 0, j)),
                  win(1)],
        out_specs=pl.BlockSpec((FFN_TM, FFN_TF), lambda j, i: (i, j)),
        scratch_shapes=[pltpu.VMEM((D_MODEL, 2 * FFN_TF), BF16),
                        pltpu.VMEM((SUBLANES, FFN_TF), F32),
                        pltpu.VMEM((2, FFN_TAIL, FFN_TF), F32)],
        compiler_params=_cparams(("arbitrary", "arbitrary")),
        name="ffn_up",
    )(h, w_in, w_in, conv_w, conv_w, conv_b, conv_b)


ATT_T = 256
ATT_NBIAS = 4


def _dilated_bias_tiles():
    tiles = np.zeros((ATT_NBIAS, ATT_T, ATT_T), np.float32)
    qi = np.arange(ATT_T)[:, None]
    kj = np.arange(ATT_T)[None, :]
    for off in range(ATT_NBIAS):
        delta = off * ATT_T + qi - kj
        count = np.zeros_like(delta)
        for window, dil in A_PATTERNS:
            count += ((delta >= 0) & (delta <= window) & (delta % dil == 0)).astype(delta.dtype)
        tiles[off] = np.where(count > 0, np.log(np.maximum(count, 1)), NEG)
    return tiles


def _attn_a_kernel(q_ref, k_ref, v_ref, cos_ref, sin_ref, bias_ref, o_ref, qs_ref, ks_ref, vs_ref):
    cos = cos_ref[...]
    sin = sin_ref[...]
    half = A_HEAD_DIM // 2
    q = q_ref[...].astype(F32)
    k = k_ref[...].astype(F32)
    scale = A_HEAD_DIM ** -0.5
    qs_ref[...] = ((q * cos + pltpu.roll(q, half, 1) * sin) * scale).astype(BF16)
    ks_ref[...] = (k * cos + pltpu.roll(k, half, 1) * sin).astype(BF16)
    vs_ref[...] = v_ref[...].astype(BF16)

    for i in range(SEQ // ATT_T):
        n = (i + 1) * ATT_T
        q_blk = qs_ref[i * ATT_T:n, :]
        s = lax.dot_general(q_blk, ks_ref[0:n, :], (((1,), (1,)), ((), ())), preferred_element_type=F32)
        s = s + jnp.concatenate([bias_ref[min(i - j, ATT_NBIAS - 1)] for j in range(i + 1)], axis=1)
        m = jnp.max(s, axis=-1, keepdims=True)
        p = jnp.exp(s - m)
        l = jnp.sum(p, axis=-1, keepdims=True)
        acc = jnp.dot(p.astype(BF16), vs_ref[0:n, :], preferred_element_type=F32)
        o_ref[i * ATT_T:n, :] = (acc / l).astype(o_ref.dtype)


def _attn_a(proj, cos_a, sin_a):
    bias = jnp.asarray(_dilated_bias_tiles())
    tab = pl.BlockSpec((SEQ, LANES), lambda b, h: (b, 0))
    return pl.pallas_call(
        _attn_a_kernel,
        out_shape=jax.ShapeDtypeStruct((TOKENS, A_WIDTH), BF16),
        grid=(BATCH, A_HEADS),
        in_specs=[pl.BlockSpec((SEQ, A_HEAD_DIM), lambda b, h: (b, h)),
                  pl.BlockSpec((SEQ, A_HEAD_DIM), lambda b, h: (b, A_HEADS + h)),
                  pl.BlockSpec((SEQ, A_HEAD_DIM), lambda b, h: (b, 2 * A_HEADS + h)),
                  tab, tab,
                  pl.BlockSpec((ATT_NBIAS, ATT_T, ATT_T), lambda b, h: (0, 0, 0))],
        out_specs=pl.BlockSpec((SEQ, A_HEAD_DIM), lambda b, h: (b, h)),
        scratch_shapes=[pltpu.VMEM((SEQ, A_HEAD_DIM), BF16)] * 3,
        compiler_params=_cparams(("parallel", "parallel")),
        name="attn_dilated",
    )(proj, proj, proj, cos_a, sin_a, bias)


LRU_TS = 512
LRU_HALO = SUBLANES


def _lru_kernel(xb_ref, yb_ref, cw_ref, cb_ref, ga_ref, gab_ref, gx_ref, gxb_ref, lam_ref, o_ref,
                ext_ref, a_ref, b_ref, carry_ref):
    t = pl.program_id(1)

    @pl.when(t == 0)
    def _():
        ext_ref[0:LRU_HALO, :] = jnp.zeros((LRU_HALO, B_WIDTH), F32)
        carry_ref[...] = jnp.zeros_like(carry_ref)

    ext_ref[LRU_HALO:, :] = xb_ref[...].astype(F32)
    ext = ext_ref[...]
    base = LRU_HALO - (B_CONV - 1)
    xc = cb_ref[...]
    for i in range(B_CONV):
        xc = xc + cw_ref[i:i + 1, :] * ext[base + i:base + i + LRU_TS, :]
    ext_ref[0:LRU_HALO, :] = ext[LRU_TS:LRU_TS + LRU_HALO, :]

    lam = lam_ref[...]
    neg_sp = -LRU_C * (jnp.maximum(-lam, 0.0) + jnp.log1p(jnp.exp(-jnp.abs(lam))))
    width = B_WIDTH // B_BLOCKS
    for blk in range(B_BLOCKS):
        sl = slice(blk * width, (blk + 1) * width)
        xh = xc[:, sl]
        xh16 = xh.astype(BF16)
        r = jax.nn.sigmoid(jnp.dot(xh16, ga_ref[blk].astype(BF16), preferred_element_type=F32) + gab_ref[:, sl])
        gi = jax.nn.sigmoid(jnp.dot(xh16, gx_ref[blk].astype(BF16), preferred_element_type=F32) + gxb_ref[:, sl])
        log_a = r * neg_sp[:, sl]
        a_ref[:, sl] = jnp.exp(log_a)
        th = jnp.tanh(log_a)
        b_ref[:, sl] = jnp.sqrt(-2.0 * th / (1.0 - th)) * (gi * xh)

    row = lax.broadcasted_iota(jnp.int32, (SUBLANES, B_WIDTH), 0)

    def scan_body(g, h_prev):
        rows = pl.ds(pl.multiple_of(g * SUBLANES, SUBLANES), SUBLANES)
        a = a_ref[rows, :]
        b = b_ref[rows, :]
        for s in (1, 2, 4):
            keep = row >= s
            a_sh = jnp.where(keep, pltpu.roll(a, s, 0), 1.0)
            b_sh = jnp.where(keep, pltpu.roll(b, s, 0), 0.0)
            b = a * b_sh + b
            a = a * a_sh
        h = a * h_prev + b
        b_ref[rows, :] = h
        return jnp.broadcast_to(h[SUBLANES - 1:SUBLANES, :], (SUBLANES, B_WIDTH))

    carry_ref[...] = lax.fori_loop(0, LRU_TS // SUBLANES, scan_body, carry_ref[...])
    o_ref[...] = (b_ref[...] * _gelu_tanh(yb_ref[...].astype(F32))).astype(o_ref.dtype)


def _lru(proj, conv_w, conv_b, ga_w, ga_b, gx_w, gx_b, lam, e):
    nts = SEQ // LRU_TS
    vec = pl.BlockSpec((None, 1, B_WIDTH), lambda b, t: (e, 0, 0))
    gate = pl.BlockSpec((None, B_BLOCKS, B_WIDTH // B_BLOCKS, B_WIDTH // B_BLOCKS), lambda b, t: (e, 0, 0, 0))
    r3 = lambda a: a.reshape(a.shape[0], 1, B_WIDTH)
    return pl.pallas_call(
        _lru_kernel,
        out_shape=jax.ShapeDtypeStruct((TOKENS, B_WIDTH), BF16),
        grid=(BATCH, nts),
        in_specs=[pl.BlockSpec((LRU_TS, B_WIDTH), lambda b, t: (b * nts + t, 3)),
                  pl.BlockSpec((LRU_TS, B_WIDTH), lambda b, t: (b * nts + t, 4)),
                  pl.BlockSpec((None, B_CONV, B_WIDTH), lambda b, t: (e, 0, 0)),
                  vec, gate, vec, gate, vec, vec],
        out_specs=pl.BlockSpec((LRU_TS, B_WIDTH), lambda b, t: (b * nts + t, 0)),
        scratch_shapes=[pltpu.VMEM((LRU_HALO + LRU_TS, B_WIDTH), F32),
                        pltpu.VMEM((LRU_TS, B_WIDTH), F32),
                        pltpu.VMEM((LRU_TS, B_WIDTH), F32),
                        pltpu.VMEM((SUBLANES, B_WIDTH), F32)],
        compiler_params=_cparams(("parallel", "arbitrary")),
        name="rg_lru",
    )(proj, proj, conv_w, r3(conv_b), ga_w, r3(ga_b), gx_w, r3(gx_b), r3(lam))


SWA_T = 128
SWA_PAIRS = C_GROUP // 2


def _swa_bias_tiles():
    qi = np.tile(np.arange(SWA_T), SWA_PAIRS)[:, None]
    kj = np.arange(2 * SWA_T)[None, :]
    delta = qi + SWA_T - kj
    band = (delta >= 0) & (delta <= C_WINDOW - 1)
    tiles = np.zeros((2, SWA_PAIRS * SWA_T, 2 * SWA_T), np.float32)
    tiles[0] = np.where(band & (kj >= SWA_T), 0.0, NEG)
    tiles[1] = np.where(band, 0.0, NEG)
    return tiles


def _swa_kernel(sink_ref, q_ref, kv_ref, cos_ref, sp_ref, sm_ref, bias_ref, o_ref,
                qs_ref, ka_ref, kb_ref, va_ref, vb_ref):
    kvh = pl.program_id(1)
    cos = cos_ref[...]
    s_plus = sp_ref[...]
    s_minus = sm_ref[...]
    quarter = C_HEAD_DIM // 2

    def rope(x):
        return x * cos + pltpu.roll(x, quarter, 1) * s_plus + pltpu.roll(x, LANES - quarter, 1) * s_minus

    scale = C_HEAD_DIM ** -0.5
    for j in range(SWA_PAIRS):
        sl = slice(j * LANES, (j + 1) * LANES)
        qs_ref[:, sl] = (rope(q_ref[:, sl].astype(F32)) * scale).astype(BF16)

    lane = lax.broadcasted_iota(jnp.int32, (SEQ, LANES), 1)
    low = lane < C_HEAD_DIM
    kk = rope(kv_ref[:, 0:LANES].astype(F32))
    vv = kv_ref[:, LANES:2 * LANES].astype(F32)
    kk = jnp.where(kvh == 0, kk, pltpu.roll(kk, C_HEAD_DIM, 1))
    vv = jnp.where(kvh == 0, vv, pltpu.roll(vv, C_HEAD_DIM, 1))
    k_lo = jnp.where(low, kk, 0.0)
    v_lo = jnp.where(low, vv, 0.0)
    zeros = jnp.zeros((SWA_T, LANES), BF16)
    for ref, val in ((ka_ref, k_lo), (kb_ref, pltpu.roll(k_lo, C_HEAD_DIM, 1)),
                     (va_ref, v_lo), (vb_ref, pltpu.roll(v_lo, C_HEAD_DIM, 1))):
        ref[0:SWA_T, :] = zeros
        ref[SWA_T:, :] = val.astype(BF16)

    rows_st = SWA_PAIRS * SWA_T
    pair = lax.broadcasted_iota(jnp.int32, (rows_st, 1), 0) // SWA_T
    sink_a = jnp.zeros((rows_st, 1), F32)
    sink_b = jnp.zeros((rows_st, 1), F32)
    for j in range(SWA_PAIRS):
        sink_a = jnp.where(pair == j, sink_ref[kvh * C_GROUP + 2 * j], sink_a)
        sink_b = jnp.where(pair == j, sink_ref[kvh * C_GROUP + 2 * j + 1], sink_b)

    def q_body(i, carry):
        r0 = pl.multiple_of(i * SWA_T, SWA_T)
        q_st = jnp.concatenate([qs_ref[pl.ds(r0, SWA_T), j * LANES:(j + 1) * LANES]
                                for j in range(SWA_PAIRS)], axis=0)
        bias = bias_ref[jnp.minimum(i, 1)]
        win = pl.ds(r0, 2 * SWA_T)
        out = jnp.zeros((rows_st, LANES), F32)
        for k_ref, v_ref, sink in ((ka_ref, va_ref, sink_a), (kb_ref, vb_ref, sink_b)):
            s = lax.dot_general(q_st, k_ref[win, :], (((1,), (1,)), ((), ())),
                                preferred_element_type=F32) + bias
            m = jnp.maximum(jnp.max(s, axis=-1, keepdims=True), sink)
            p = jnp.exp(s - m)
            den = jnp.sum(p, axis=-1, keepdims=True) + jnp.exp(sink - m)
            out = out + jnp.dot(p.astype(BF16), v_ref[win, :], preferred_element_type=F32) / den
        for j in range(SWA_PAIRS):
            o_ref[pl.ds(r0, SWA_T), j * LANES:(j + 1) * LANES] = out[j * SWA_T:(j + 1) * SWA_T, :].astype(o_ref.dtype)
        return carry

    lax.fori_loop(0, SEQ // SWA_T, q_body, 0)


def _swa(proj, sinks, cos_c, sin_cp, sin_cm, o):
    bias = jnp.asarray(_swa_bias_tiles())
    qw = C_WIDTH // C_KV_HEADS
    tab = pl.BlockSpec((SEQ, LANES), lambda b, g, *_: (b, 0))
    kv_block = C_WIDTH // (2 * C_KV_WIDTH)
    grid_spec = pltpu.PrefetchScalarGridSpec(
        num_scalar_prefetch=1,
        grid=(BATCH, C_KV_HEADS),
        in_specs=[pl.BlockSpec((SEQ, qw), lambda b, g, *_: (b, g)),
                  pl.BlockSpec((SEQ, 2 * C_KV_WIDTH), lambda b, g, *_: (b, kv_block)),
                  tab, tab, tab,
                  pl.BlockSpec((2, SWA_PAIRS * SWA_T, 2 * SWA_T), lambda b, g, *_: (0, 0, 0))],
        out_specs=pl.BlockSpec((SEQ, qw), lambda b, g, *_: (b, g)),
        scratch_shapes=[pltpu.VMEM((SEQ, qw), BF16)] + [pltpu.VMEM((SWA_T + SEQ, LANES), BF16)] * 4,
    )
    return pl.pallas_call(
        _swa_kernel,
        out_shape=jax.ShapeDtypeStruct((TOKENS, C_WIDTH), BF16),
        grid_spec=grid_spec,
        compiler_params=_cparams(("parallel", "parallel")),
        name="attn_swa",
    )(sinks[o].astype(F32), proj, proj, cos_c, sin_cp, sin_cm, bias)


S5_L = SUBLANES
S5_NC = SEQ // S5_L
S5_GPB = LANES // D_GROUP_DIM
S5_NB = D_WIDTH // LANES
S5_SW = S5_GPB * D_STATE
S5_CW = S5_L * LANES


def _s5_build_operators(are_ref, aim_ref, dt_ref, bre_ref, bim_ref, cre_ref, cim_ref,
                         wz_ref, ki_ref, mit_ref, dec_ref):
    lr, li, dt = are_ref[...], aim_ref[...], dt_ref[...]
    zr, zi = lr * dt, li * dt

    def a_pow(k):
        mag = jnp.exp(k * zr)
        return mag * jnp.cos(k * zi), mag * jnp.sin(k * zi)

    ar, ai = a_pow(1.0)
    den = lr * lr + li * li
    cr = ((ar - 1.0) * lr + ai * li) / den
    ci = (ai * lr - (ar - 1.0) * li) / den
    row = lax.broadcasted_iota(jnp.int32, (LANES, S5_SW), 0)
    col = lax.broadcasted_iota(jnp.int32, (LANES, S5_SW), 1)
    diag = (row // D_GROUP_DIM) == (col // D_STATE)

    def expand(x_ref):
        return jnp.where(diag, jnp.concatenate([x_ref[...]] * S5_GPB, axis=1), 0.0)

    b_r, b_i = expand(bre_ref), expand(bim_ref)
    bb_r = b_r * cr - b_i * ci
    bb_i = b_r * ci + b_i * cr
    c_r, c_i = expand(cre_ref), expand(cim_ref)

    for i in range(S5_L):
        pr, pi = a_pow(float(S5_L - 1 - i))
        rows = slice(i * LANES, (i + 1) * LANES)
        wz_ref[rows, 0:S5_SW] = (bb_r * pr - bb_i * pi).astype(BF16)
        wz_ref[rows, S5_SW:2 * S5_SW] = (bb_r * pi + bb_i * pr).astype(BF16)

    c_pow = []
    for k in range(S5_L + 1):
        pr, pi = a_pow(float(k))
        c_pow.append((c_r * pr - c_i * pi, c_r * pi + c_i * pr))
    for j in range(S5_L):
        rows = slice(j * LANES, (j + 1) * LANES)
        mit_ref[rows, 0:S5_SW] = c_pow[j + 1][0].astype(BF16)
        mit_ref[rows, S5_SW:2 * S5_SW] = (-c_pow[j + 1][1]).astype(BF16)

    nt = (((1,), (1,)), ((), ()))
    bb_r16, bb_i16 = bb_r.astype(BF16), bb_i.astype(BF16)
    k_lag = []
    for t in range(S5_L):
        k_t = (lax.dot_general(bb_r16, c_pow[t][0].astype(BF16), nt, preferred_element_type=F32)
               - lax.dot_general(bb_i16, c_pow[t][1].astype(BF16), nt, preferred_element_type=F32))
        k_lag.append(k_t.astype(BF16))
    zero = jnp.zeros((LANES, LANES), BF16)
    for a in range(S5_L):
        for b in range(S5_L):
            ki_ref[a * LANES:(a + 1) * LANES, b * LANES:(b + 1) * LANES] = k_lag[b - a] if b >= a else zero

    rows8 = lax.broadcasted_iota(jnp.int32, (SUBLANES, S5_SW), 0)
    for idx, s in enumerate((1, 2, 4)):
        pr, pi = a_pow(float(S5_L * s))
        dec_ref[2 * idx] = jnp.where(rows8 >= s, pr, 0.0)
        dec_ref[2 * idx + 1] = jnp.where(rows8 >= s, pi, 0.0)
    kk = (S5_L * (rows8 + 1)).astype(F32)
    mag = jnp.exp(kk * zr)
    dec_ref[6] = mag * jnp.cos(kk * zi)
    dec_ref[7] = mag * jnp.sin(kk * zi)


def _s5_kernel(u_ref, are_ref, aim_ref, dt_ref, bre_ref, bim_ref, cre_ref, cim_ref, d_ref, o_ref,
               wz_ref, ki_ref, mit_ref, dec_ref, uf_ref, sre_ref, sim_ref, y_ref):
    @pl.when(pl.program_id(1) == 0)
    def _():
        _s5_build_operators(are_ref, aim_ref, dt_ref, bre_ref, bim_ref, cre_ref, cim_ref,
                            wz_ref, ki_ref, mit_ref, dec_ref)

    uf_ref[...] = u_ref[...].astype(F32)
    u_steps = [uf_ref[pl.ds(j, S5_NC, stride=S5_L), :] for j in range(S5_L)]
    u_all = jnp.concatenate(u_steps, axis=1).astype(BF16)

    z = jnp.dot(u_all, wz_ref[...], preferred_element_type=F32)
    ng = S5_NC // SUBLANES
    x_re = z[:, 0:S5_SW].reshape(ng, SUBLANES, S5_SW)
    x_im = z[:, S5_SW:2 * S5_SW].reshape(ng, SUBLANES, S5_SW)
    for idx, s in enumerate((1, 2, 4)):
        m_re = dec_ref[2 * idx]
        m_im = dec_ref[2 * idx + 1]
        r_re = pltpu.roll(x_re, s, 1)
        r_im = pltpu.roll(x_im, s, 1)
        x_re, x_im = x_re + (m_re * r_re - m_im * r_im), x_im + (m_re * r_im + m_im * r_re)
    sre_ref[...] = x_re.reshape(S5_NC, S5_SW)
    sim_ref[...] = x_im.reshape(S5_NC, S5_SW)
    p_re = dec_ref[6]
    p_im = dec_ref[7]

    def carry_body(g, c):
        c_re, c_im = c
        rows = pl.ds(pl.multiple_of(g * SUBLANES, SUBLANES), SUBLANES)
        s_re = sre_ref[rows, :] + (p_re * c_re - p_im * c_im)
        s_im = sim_ref[rows, :] + (p_re * c_im + p_im * c_re)
        sre_ref[rows, :] = s_re
        sim_ref[rows, :] = s_im
        last = slice(SUBLANES - 1, SUBLANES)
        return (jnp.broadcast_to(s_re[last, :], (SUBLANES, S5_SW)),
                jnp.broadcast_to(s_im[last, :], (SUBLANES, S5_SW)))

    zero = jnp.zeros((SUBLANES, S5_SW), F32)
    lax.fori_loop(0, ng, carry_body, (zero, zero))

    first = lax.broadcasted_iota(jnp.int32, (S5_NC, S5_SW), 0) == 0
    prev_re = jnp.where(first, 0.0, pltpu.roll(sre_ref[...], 1, 0))
    prev_im = jnp.where(first, 0.0, pltpu.roll(sim_ref[...], 1, 0))
    s_prev = jnp.concatenate([prev_re, prev_im], axis=1).astype(BF16)
    y = (lax.dot_general(s_prev, mit_ref[...], (((1,), (1,)), ((), ())), preferred_element_type=F32)
         + jnp.dot(u_all, ki_ref[...], preferred_element_type=F32))
    d = d_ref[...]
    for j in range(S5_L):
        yj = y[:, j * LANES:(j + 1) * LANES] + d * u_steps[j]
        y_ref[pl.ds(j, S5_NC, stride=S5_L), :] = _gelu_tanh(yj)
    o_ref[...] = y_ref[...].astype(o_ref.dtype)


def _s5(proj, a_re, a_im, b_re, b_im, c_re, c_im, log_dt, d_skip, o):
    ublock = (C_WIDTH + 2 * C_KV_WIDTH) // LANES
    flat = lambda t: t[o].astype(F32).reshape(S5_NB, 1, S5_SW)
    dt = jnp.repeat(jnp.exp(log_dt[o].astype(F32)), D_STATE).reshape(S5_NB, 1, S5_SW)
    rows_gc = lambda t: t.astype(F32).reshape(S5_NB, LANES, D_STATE)
    b_gc = lambda t: rows_gc(t[o].transpose(0, 2, 1))
    vec = pl.BlockSpec((None, 1, S5_SW), lambda n, b: (n, 0, 0))
    mat = pl.BlockSpec((None, LANES, D_STATE), lambda n, b: (n, 0, 0))
    return pl.pallas_call(
        _s5_kernel,
        out_shape=jax.ShapeDtypeStruct((TOKENS, D_WIDTH), BF16),
        grid=(S5_NB, BATCH),
        in_specs=[pl.BlockSpec((SEQ, LANES), lambda n, b: (b, ublock + n)),
                  vec, vec, vec, mat, mat, mat, mat,
                  pl.BlockSpec((None, 1, LANES), lambda n, b: (o, 0, n))],
        out_specs=pl.BlockSpec((SEQ, LANES), lambda n, b: (b, n)),
        scratch_shapes=[pltpu.VMEM((S5_CW, 2 * S5_SW), BF16),
                        pltpu.VMEM((S5_CW, S5_CW), BF16),
                        pltpu.VMEM((S5_CW, 2 * S5_SW), BF16),
                        pltpu.VMEM((8, SUBLANES, S5_SW), F32),
                        pltpu.VMEM((SEQ, LANES), F32),
                        pltpu.VMEM((S5_NC, S5_SW), F32), pltpu.VMEM((S5_NC, S5_SW), F32),
                        pltpu.VMEM((SEQ, LANES), F32)],
        compiler_params=_cparams(("parallel", "arbitrary")),
        name="s5_ssm",
    )(proj, flat(a_re), flat(a_im), dt, b_gc(b_re), b_gc(b_im), rows_gc(c_re[o]), rows_gc(c_im[o]),
      d_skip.reshape(-1, 1, D_WIDTH))


GLU_TN = 512


def _glu_kernel(z_ref, w_ref, b_ref, zc_ref, o_ref):
    gate = jax.nn.sigmoid(jnp.dot(z_ref[...], w_ref[...].astype(BF16), preferred_element_type=F32) + b_ref[...])
    o_ref[...] = (zc_ref[...].astype(F32) * gate).astype(o_ref.dtype)


def _glu(z, w, b, o):
    return pl.pallas_call(
        _glu_kernel,
        out_shape=jax.ShapeDtypeStruct((TOKENS, D_WIDTH), BF16),
        grid=(TOKENS // MM_TM, D_WIDTH // GLU_TN),
        in_specs=[pl.BlockSpec((MM_TM, D_WIDTH), lambda i, j: (i, 0)),
                  pl.BlockSpec((None, D_WIDTH, GLU_TN), lambda i, j: (o, 0, j)),
                  pl.BlockSpec((None, 1, GLU_TN), lambda i, j: (o, 0, j)),
                  pl.BlockSpec((MM_TM, GLU_TN), lambda i, j: (i, j))],
        out_specs=pl.BlockSpec((MM_TM, GLU_TN), lambda i, j: (i, j)),
        compiler_params=_cparams(("parallel", "parallel")),
        name="s5_glu",
    )(z, w, b.reshape(-1, 1, D_WIDTH), z)


OUT_TM = 512
DOWN_TM = 256


def kernel(x, c, positions, ada_w, ada_b, norm_mix, norm_ffn, norm_final, ev_w_in, ev_conv_w, ev_conv_b, ev_gate_a_w, ev_gate_a_b, ev_gate_x_w, ev_gate_x_b, ev_lambda, ev_w_out, od_w_in, od_sinks, od_a_re, od_a_im, od_b_re, od_b_im, od_c_re, od_c_im, od_d, od_log_dt, od_glu_w, od_glu_b, od_w_out, ffn_w_in, ffn_conv_w, ffn_conv_b, ffn_w_out):
    ffn_out16 = ffn_w_out.astype(BF16)
    ffn_cw = ffn_conv_w.astype(F32)
    ffn_cb = ffn_conv_b.astype(F32).reshape(DEPTH, 1, 2 * D_FF)

    mod = _ada_mod(c, ada_w, ada_b)
    mod = mod.reshape(DEPTH, SUBLANES, 6, 1, D_MODEL).transpose(0, 2, 1, 3, 4)
    cos_a, sin_a, cos_c, sin_cp, sin_cm = _rope_tables(positions)

    xt = x.reshape(TOKENS, D_MODEL).astype(F32)
    h = None
    for layer in range(DEPTH):
        idx = layer // 2
        if layer == 0:
            proj = _norm_matmul(xt, norm_mix, mod, 0, ev_w_in, idx, BF16, "even_in_proj")
        elif layer % 2 == 0:
            proj = _matmul(h, ev_w_in, idx, BF16, "even_in_proj")
        else:
            proj = _matmul(h, od_w_in, idx, BF16, "odd_in_proj")
        if layer % 2 == 0:
            attn = _attn_a(proj, cos_a, sin_a)
            other = _lru(proj, ev_conv_w, ev_conv_b, ev_gate_a_w, ev_gate_a_b, ev_gate_x_w, ev_gate_x_b,
                         ev_lambda, idx)
            w_out = ev_w_out
        else:
            attn = _swa(proj, od_sinks, cos_c, sin_cp, sin_cm, idx)
            z = _s5(proj, od_a_re, od_a_im, od_b_re, od_b_im, od_c_re, od_c_im, od_log_dt, od_d, idx)
            other = _glu(z, od_glu_w, od_glu_b, idx)
            w_out = od_w_out
        xt, h2 = _mm_res([attn, other], w_out, idx, OUT_TM, xt, mod, layer, 2, norm_ffn, layer, layer, 3,
                         False, "mix_out_proj")
        act = _ffn_up(h2, ffn_w_in, ffn_cw, ffn_cb, layer)
        if layer + 1 < DEPTH:
            xt, h = _mm_res([act], ffn_out16, layer, DOWN_TM, xt, mod, layer, 5, norm_mix, layer + 1, layer + 1,
                            0, False, "ffn_down_proj")
        else:
            out = _mm_res([act], ffn_out16, layer, DOWN_TM, xt, mod, layer, 5, norm_final, 0, layer, 0, True,
                          "ffn_down_final")
    return out.reshape(BATCH, SEQ, D_MODEL).astype(x.dtype)
```
